```python
import math
import jax, jax.numpy as jnp
from jax import lax
import numpy as np

D_MODEL = 1024
BATCH = 8
SEQ = 4096
DEPTH = 2

HEAD_DIM = 64
N_MOBA_HEADS = 6
N_FOX_HEADS = 6
CONV_CH = 256
MOBA_W = N_MOBA_HEADS * HEAD_DIM
FOX_W = N_FOX_HEADS * HEAD_DIM
MIX_W = MOBA_W + FOX_W + CONV_CH
SPLIT_SIZES = [MOBA_W, MOBA_W, MOBA_W, FOX_W, FOX_W, FOX_W, N_FOX_HEADS, CONV_CH, CONV_CH, CONV_CH]
IN_W = int(sum(SPLIT_SIZES))
SPLIT_IDX = [int(v) for v in np.cumsum(SPLIT_SIZES)[:-1]]
MOBA_BLOCK = 256
MOBA_TOPK = 3
MOBA_Q_CHUNK = 32
FOX_Q_BLOCK = 128
CONV_WIDTH = 3
REL_BUCKETS = 32
REL_MAX_DIST = 128
D_FF = -(-8 * D_MODEL // (3 * 256)) * 256
RMS_EPS = 1e-6

kernel_name = "hybrid_moba_fox_shortconv_block"


def rms_norm(x, g):
    xf = x.astype(jnp.float32)
    y = xf * lax.rsqrt(jnp.mean(xf * xf, axis=-1, keepdims=True) + RMS_EPS)
    return (y * g.astype(jnp.float32)).astype(x.dtype)


def t5_bucket(dist):
    max_exact = REL_BUCKETS // 2
    n = jnp.maximum(dist, 0)
    nf = jnp.maximum(n, 1).astype(jnp.float32)
    large = max_exact + (jnp.log(nf / max_exact) / math.log(REL_MAX_DIST / max_exact)
                         * (REL_BUCKETS - max_exact)).astype(jnp.int32)
    large = jnp.minimum(large, REL_BUCKETS - 1)
    return jnp.where(n < max_exact, n, large)


def moba_attention(q, k, v, rel_bias):
    B, H, S, Dh = q.shape
    nb = -(-S // MOBA_BLOCK)
    pad = nb * MOBA_BLOCK - S
    kp = jnp.pad(k, ((0, 0), (0, 0), (0, pad), (0, 0)))
    vp = jnp.pad(v, ((0, 0), (0, 0), (0, pad), (0, 0)))
    kb = kp.reshape(B, H, nb, MOBA_BLOCK, Dh)
    vb = vp.reshape(B, H, nb, MOBA_BLOCK, Dh)
    kmean = jnp.mean(kb.astype(jnp.float32), axis=3)
    topk = min(MOBA_TOPK, nb)
    scale = Dh ** -0.5
    rb = rel_bias.astype(jnp.float32)
    rb_t = rb.T
    b_idx = jnp.arange(B)[:, None, None, None]
    h_idx = jnp.arange(H)[None, :, None, None]
    offs = jnp.arange(MOBA_BLOCK)
    C = MOBA_Q_CHUNK
    nC = S // C
    q_chunks = q.reshape(B, H, nC, C, Dh).transpose(2, 0, 1, 3, 4)

    def chunk(args):
        qc, c = args
        t = c * C + jnp.arange(C)
        own = (c * C) // MOBA_BLOCK
        gate = jnp.einsum('bhcd,bhnd->bhcn', qc.astype(jnp.float32), kmean)
        gate = jnp.where(jnp.arange(nb) < own, gate, -jnp.inf)
        _, sel = lax.top_k(gate, topk)
        sel_ok = jnp.arange(topk) < own
        kg = kb[b_idx, h_idx, sel]
        vg = vb[b_idx, h_idx, sel]
        kpos = sel[..., None] * MOBA_BLOCK + offs
        lg = jnp.einsum('bhcd,bhcjpd->bhcjp', qc, kg).astype(jnp.float32) * scale
        lg = lg + rb_t[h_idx[..., None], t5_bucket(t[None, None, :, None, None] - kpos)]
        lg = jnp.where(sel_ok[:, None], lg, -jnp.inf)
        ko = lax.dynamic_slice_in_dim(kp, own * MOBA_BLOCK, MOBA_BLOCK, axis=2)
        vo = lax.dynamic_slice_in_dim(vp, own * MOBA_BLOCK, MOBA_BLOCK, axis=2)
        dist = t[:, None] - (own * MOBA_BLOCK + offs)[None, :]
        lo = jnp.einsum('bhcd,bhpd->bhcp', qc, ko).astype(jnp.float32) * scale
        lo = lo + jnp.transpose(rb[t5_bucket(dist)], (2, 0, 1))
        lo = jnp.where(dist >= 0, lo, -jnp.inf)
        logits = jnp.concatenate([lo, lg.reshape(B, H, C, topk * MOBA_BLOCK)], axis=-1)
        p = jax.nn.softmax(logits, axis=-1).astype(v.dtype)
        po = p[..., :MOBA_BLOCK]
        pg = p[..., MOBA_BLOCK:].reshape(B, H, C, topk, MOBA_BLOCK)
        return (jnp.einsum('bhcp,bhpd->bhcd', po, vo)
                + jnp.einsum('bhcjp,bhcjpd->bhcd', pg, vg))

    out = lax.map(chunk, (q_chunks, jnp.arange(nC)))
    return out.transpose(1, 2, 0, 3, 4).reshape(B, H, S, Dh)


def forgetting_attention(q, k, v, log_f):
    B, H, S, Dh = q.shape
    scale = Dh ** -0.5
    cum = lax.cumsum(log_f, axis=2)
    nQ = S // FOX_Q_BLOCK
    qb = q.reshape(B, H, nQ, FOX_Q_BLOCK, Dh).transpose(2, 0, 1, 3, 4)
    cb = cum.reshape(B, H, nQ, FOX_Q_BLOCK).transpose(2, 0, 1, 3)
    kpos = jnp.arange(S)

    def blk(args):
        qi, ci, i = args
        t = i * FOX_Q_BLOCK + jnp.arange(FOX_Q_BLOCK)
        lg = (jnp.einsum('bhqd,bhsd->bhqs', qi, k).astype(jnp.float32) * scale
              + ci[..., None] - cum[:, :, None, :])
        lg = jnp.where(kpos[None, :] <= t[:, None], lg, -jnp.inf)
        p = jax.nn.softmax(lg, axis=-1).astype(v.dtype)
        return jnp.einsum('bhqs,bhsd->bhqd', p, v)

    out = lax.map(blk, (qb, cb, jnp.arange(nQ)))
    return out.transpose(1, 2, 0, 3, 4).reshape(B, H, S, Dh)


def short_conv_mixer(gate_b, gate_c, h, conv_w):
    u = gate_c * h
    y = lax.conv_general_dilated(u, conv_w[:, None, :].astype(u.dtype), window_strides=(1,),
                                 padding=[(CONV_WIDTH - 1, 0)],
                                 dimension_numbers=('NWC', 'WIO', 'NWC'),
                                 feature_group_count=CONV_CH)
    return gate_b * y


def hybrid_layer(x, w_in, b_f, conv_w, w_out, rel_bias, g_pre_mix, g_post_mix,
                 g_pre_ffn, g_post_ffn, w_gate, w_up, w_down):
    B, S, _ = x.shape
    hn = rms_norm(x, g_pre_mix)
    proj = hn @ w_in
    mq, mk, mv, fq, fk, fv, f_logit, cvb, cvc, cvh = jnp.split(proj, SPLIT_IDX, axis=-1)

    def heads(t, n):
        return t.reshape(B, S, n, HEAD_DIM).transpose(0, 2, 1, 3)

    def merge(o):
        return o.transpose(0, 2, 1, 3).reshape(B, S, -1)

    moba = moba_attention(heads(mq, N_MOBA_HEADS), heads(mk, N_MOBA_HEADS),
                          heads(mv, N_MOBA_HEADS), rel_bias)
    log_f = jax.nn.log_sigmoid((f_logit + b_f).astype(jnp.float32)).transpose(0, 2, 1)
    fox = forgetting_attention(heads(fq, N_FOX_HEADS), heads(fk, N_FOX_HEADS),
                               heads(fv, N_FOX_HEADS), log_f)
    conv = short_conv_mixer(cvb, cvc, cvh, conv_w)
    mixed = jnp.concatenate([merge(moba), merge(fox), conv], axis=-1) @ w_out
    x = x + rms_norm(mixed, g_post_mix)
    h2 = rms_norm(x, g_pre_ffn)
    ff = (jax.nn.silu(h2 @ w_gate) * (h2 @ w_up)) @ w_down
    return x + rms_norm(ff, g_post_ffn)


def setup_inputs(seed: int = 0) -> dict:
    key = jax.random.key(seed)
    ks = jax.random.split(key, 14)
    f32 = jnp.float32
    nrm = lambda k, shape, s: (jax.random.normal(k, shape, f32) * s).astype(f32)
    return {
        "x": nrm(ks[0], (BATCH, SEQ, D_MODEL), 1.0),
        "w_in": nrm(ks[1], (DEPTH, D_MODEL, IN_W), D_MODEL ** -0.5),
        "b_f": 2.0 + nrm(ks[2], (DEPTH, N_FOX_HEADS), 0.1),
        "conv_w": nrm(ks[3], (DEPTH, CONV_WIDTH, CONV_CH), CONV_WIDTH ** -0.5),
        "w_out": nrm(ks[4], (DEPTH, MIX_W, D_MODEL), MIX_W ** -0.5),
        "rel_bias": nrm(ks[5], (REL_BUCKETS, N_MOBA_HEADS), 0.5),
        "g_pre_mix": 1.0 + nrm(ks[6], (DEPTH, D_MODEL), 0.05),
        "g_post_mix": 1.0 + nrm(ks[7], (DEPTH, D_MODEL), 0.05),
        "g_pre_ffn": 1.0 + nrm(ks[8], (DEPTH, D_MODEL), 0.05),
        "g_post_ffn": 1.0 + nrm(ks[9], (DEPTH, D_MODEL), 0.05),
        "w_gate": nrm(ks[10], (DEPTH, D_MODEL, D_FF), D_MODEL ** -0.5),
        "w_up": nrm(ks[11], (DEPTH, D_MODEL, D_FF), D_MODEL ** -0.5),
        "w_down": nrm(ks[12], (DEPTH, D_FF, D_MODEL), D_FF ** -0.5),
    }


def reference(x, w_in, b_f, conv_w, w_out, rel_bias, g_pre_mix, g_post_mix,
              g_pre_ffn, g_post_ffn, w_gate, w_up, w_down):
    for l in range(DEPTH):
        x = hybrid_layer(x, w_in[l], b_f[l], conv_w[l], w_out[l], rel_bias,
                         g_pre_mix[l], g_post_mix[l], g_pre_ffn[l], g_post_ffn[l],
                         w_gate[l], w_up[l], w_down[l])
    return x
```

```python
import functools
import math

import jax
import jax.numpy as jnp
import numpy as np
from jax import lax
from jax.experimental import pallas as pl
from jax.experimental.pallas import tpu as pltpu

D_MODEL = 1024
HEAD_DIM = 64
N_HEADS = 6
ATT_W = N_HEADS * HEAD_DIM
PAIR_W = 2 * HEAD_DIM
N_PAIRS = N_HEADS // 2
CONV_CH = 256
CONV_WIDTH = 3
BLK = 256
MAX_BLOCKS = 16
TOPK = 3
REL_BUCKETS = 32
REL_MAX_DIST = 128
D_FF = 2816
RMS_EPS = 1e-6
NEG = -1e30
LANES = 128
N_SPLIT = 3

F32 = jnp.float32
BF16 = jnp.bfloat16
NT_DIMS = (((1,), (1,)), ((), ()))

INPROJ_TILE = 512
FFN_TILE = 512
FFN_CHUNK = 1408
VMEM_LIMIT = 50 * 1024 * 1024


def _split_bf16(v, n):
    terms = []
    for _ in range(n):
        t = v.astype(BF16)
        terms.append(t)
        v = v - t.astype(F32)
    return terms


def _rms(v, g):
    return v * lax.rsqrt(jnp.mean(v * v, axis=-1, keepdims=True) + RMS_EPS) * g


def _bucket_tiles():
    max_exact = REL_BUCKETS // 2

    def bucket(dist):
        n = np.maximum(dist, 0)
        nf = np.maximum(n, 1).astype(np.float32)
        large = max_exact + (np.log(nf / np.float32(max_exact))
                             / np.float32(math.log(REL_MAX_DIST / max_exact))
                             * np.float32(REL_BUCKETS - max_exact)).astype(np.int32)
        large = np.minimum(large, REL_BUCKETS - 1)
        return np.where(n < max_exact, n, large).astype(np.int32)

    key = np.arange(BLK, dtype=np.int32)[:, None]
    qry = np.arange(BLK, dtype=np.int32)[None, :]
    d0 = qry - key
    own = np.where(d0 >= 0, bucket(d0), -1)
    prev = bucket(d0 + BLK)
    older = bucket(d0 + 2 * BLK)
    return np.stack([own, prev, older]).astype(np.int32)


def _bias_tile_kernel(rb_ref, bm_ref, out_ref):
    h = pl.program_id(0)
    for kind in range(3):
        bm = bm_ref[kind]
        t = jnp.full(bm.shape, NEG, F32)
        for b in range(REL_BUCKETS):
            t = jnp.where(bm == b, rb_ref[b, h], t)
        out_ref[0, kind] = t


def _bias_tiles(rel_bias):
    return pl.pallas_call(
        _bias_tile_kernel,
        grid=(N_HEADS,),
        in_specs=[pl.BlockSpec(memory_space=pltpu.SMEM),
                  pl.BlockSpec((3, BLK, BLK), lambda h: (0, 0, 0))],
        out_specs=pl.BlockSpec((1, 3, BLK, BLK), lambda h: (h, 0, 0, 0)),
        out_shape=jax.ShapeDtypeStruct((N_HEADS, 3, BLK, BLK), F32),
        name="bias_tiles",
    )(rel_bias.astype(F32), jnp.asarray(_bucket_tiles()))


def _inproj_kernel(x_ref, g_ref, wqk_ref, wvt_ref, wf_ref, bf_ref, wc_ref, cw_ref, tri_ref, perm_ref,
                   mq_ref, mk_ref, fq_ref, fkp_ref, vt_ref, sel_ref, conv_ref,
                   km_ref, cum_ref, uc_ref, *, tm):
    st = pl.program_id(1)
    blocks_per_tile = tm // BLK

    @pl.when(st == 0)
    def _():
        km_ref[...] = jnp.zeros_like(km_ref)
        cum_ref[...] = jnp.zeros_like(cum_ref)
        uc_ref[...] = jnp.zeros_like(uc_ref)

    hn = _rms(x_ref[0], g_ref[...]).astype(BF16)

    qk = jnp.dot(hn, wqk_ref[...], preferred_element_type=F32)
    mq = qk[:, 0 * ATT_W:1 * ATT_W]
    mk = qk[:, 1 * ATT_W:2 * ATT_W]
    fq = qk[:, 2 * ATT_W:3 * ATT_W]
    fk = qk[:, 3 * ATT_W:4 * ATT_W]
    scale = HEAD_DIM ** -0.5
    mq_ref[0] = (mq * scale).astype(BF16)
    mk_ref[0] = mk.astype(BF16)
    fq_ref[0] = (fq * scale).astype(BF16)

    vt = lax.dot_general(wvt_ref[...], hn, NT_DIMS, preferred_element_type=F32).astype(BF16)
    for r in range(blocks_per_tile):
        vt_ref[0, r] = vt[:, r * BLK:(r + 1) * BLK]

    rows = lax.broadcasted_iota(jnp.int32, km_ref.shape, 0)
    cols = lax.broadcasted_iota(jnp.int32, km_ref.shape, 1)
    same_head = (rows >> 4) == (cols >> 6)
    km = km_ref[...]
    for r in range(blocks_per_tile):
        kmean = jnp.mean(mk[r * BLK:(r + 1) * BLK], axis=0, keepdims=True)
        n = st * blocks_per_tile + r
        km = jnp.where(same_head & ((rows & (MAX_BLOCKS - 1)) == n), kmean, km)
    km_ref[...] = km

    km_hi, km_lo = _split_bf16(km, 2)
    q_hi, q_lo = _split_bf16(mq, 2)
    gate = (lax.dot_general(km_hi, q_hi, NT_DIMS, preferred_element_type=F32)
            + lax.dot_general(km_hi, q_lo, NT_DIMS, preferred_element_type=F32)
            + lax.dot_general(km_lo, q_hi, NT_DIMS, preferred_element_type=F32))

    nidx = lax.broadcasted_iota(jnp.int32, (MAX_BLOCKS, tm), 0)
    own = st * blocks_per_tile + (lax.broadcasted_iota(jnp.int32, (MAX_BLOCKS, tm), 1) >> 8)
    valid = nidx < own
    for h in range(N_HEADS):
        gh = jnp.where(valid, gate[h * MAX_BLOCKS:(h + 1) * MAX_BLOCKS], -jnp.inf)
        rank = jnp.zeros((MAX_BLOCKS, tm), jnp.int32)
        for m in range(MAX_BLOCKS):
            gm = gh[m:m + 1, :]
            gt = (gm > gh).astype(jnp.int32)
            ge = (gm >= gh).astype(jnp.int32)
            rank = rank + jnp.where(nidx > m, ge, gt)
        sel = (rank < TOPK) & valid
        sel_ref[0, h * MAX_BLOCKS:(h + 1) * MAX_BLOCKS, :] = jnp.where(sel, 0.0, NEG)

    fl = jnp.dot(hn, wf_ref[...], preferred_element_type=F32) + bf_ref[...]
    lf = jnp.minimum(fl, 0.0) - jnp.log1p(jnp.exp(-jnp.abs(fl)))
    tri = tri_ref[...]
    cum = cum_ref[...]
    for t in _split_bf16(lf, N_SPLIT):
        cum = cum + jnp.dot(tri, t, preferred_element_type=F32)
    cum_ref[...] = cum[tm - 1:tm, :]
    kb = jnp.zeros((tm, ATT_W), F32)
    for s, t in enumerate(_split_bf16(-cum, N_SPLIT)):
        kb = kb + jnp.dot(t, perm_ref[s], preferred_element_type=F32)
    kb = kb.astype(BF16)
    fkb = fk.astype(BF16)
    for p in range(N_PAIRS):
        fkp_ref[0, :, 2 * p * PAIR_W:(2 * p + 1) * PAIR_W] = fkb[:, p * PAIR_W:(p + 1) * PAIR_W]
        fkp_ref[0, :, (2 * p + 1) * PAIR_W:(2 * p + 2) * PAIR_W] = kb[:, p * PAIR_W:(p + 1) * PAIR_W]

    cv = jnp.dot(hn, wc_ref[...], preferred_element_type=F32)
    cvb = cv[:, 0:CONV_CH]
    u = cv[:, CONV_CH:2 * CONV_CH] * cv[:, 2 * CONV_CH:3 * CONV_CH]
    uc = uc_ref[...]
    prev1 = uc[7:8]
    prev2 = uc[6:7]
    ridx = lax.broadcasted_iota(jnp.int32, u.shape, 0)
    u1 = jnp.where(ridx == 0, prev1, pltpu.roll(u, 1, 0))
    u2 = jnp.where(ridx == 0, prev2, jnp.where(ridx == 1, prev1, pltpu.roll(u, 2, 0)))
    cw = cw_ref[...]
    y = cw[0:1] * u2 + cw[1:2] * u1 + cw[2:3] * u
    conv_ref[0] = (cvb * y).astype(BF16)
    uc_ref[...] = u[tm - 8:tm]


def _inproj(x, g, wqk, wvt, wf, bfp, wc, cw, tri, perm):
    B, S, D = x.shape
    tm = INPROJ_TILE
    nblk = S // BLK
    const = lambda shape: pl.BlockSpec(shape, lambda b, s: (0,) * len(shape))
    row = lambda w: pl.BlockSpec((1, tm, w), lambda b, s: (b, s, 0))
    return pl.pallas_call(
        functools.partial(_inproj_kernel, tm=tm),
        grid=(B, S // tm),
        in_specs=[row(D), const((1, D)), const(wqk.shape), const(wvt.shape), const(wf.shape),
                  const(bfp.shape), const(wc.shape), const(cw.shape), const(tri.shape), const(perm.shape)],
        out_specs=[row(ATT_W), row(ATT_W), row(ATT_W), row(2 * ATT_W),
                   pl.BlockSpec((1, tm // BLK, 2 * ATT_W, BLK), lambda b, s: (b, s, 0, 0)),
                   pl.BlockSpec((1, N_HEADS * MAX_BLOCKS, tm), lambda b, s: (b, 0, s)),
                   row(CONV_CH)],
        out_shape=[jax.ShapeDtypeStruct((B, S, ATT_W), BF16),
                   jax.ShapeDtypeStruct((B, S, ATT_W), BF16),
                   jax.ShapeDtypeStruct((B, S, ATT_W), BF16),
                   jax.ShapeDtypeStruct((B, S, 2 * ATT_W), BF16),
                   jax.ShapeDtypeStruct((B, nblk, 2 * ATT_W, BLK), BF16),
                   jax.ShapeDtypeStruct((B, N_HEADS * MAX_BLOCKS, S), F32),
                   jax.ShapeDtypeStruct((B, S, CONV_CH), BF16)],
        scratch_shapes=[pltpu.VMEM((N_HEADS * MAX_BLOCKS, ATT_W), F32),
                        pltpu.VMEM((1, LANES), F32),
                        pltpu.VMEM((8, CONV_CH), F32)],
        compiler_params=pltpu.CompilerParams(
            dimension_semantics=("arbitrary", "arbitrary"), vmem_limit_bytes=VMEM_LIMIT),
        name="inproj",
    )(x, g, wqk, wvt, wf, bfp, wc, cw, tri, perm)


def _attn_kernel(*refs, moba):
    if moba:
        q_ref, k_ref, vt_ref, sel_ref, tb_ref, o_ref = refs
    else:
        q_ref, k_ref, vt_ref, o_ref = refs
    i = pl.program_id(2)
    q = q_ref[0]
    lane = lax.broadcasted_iota(jnp.int32, q.shape, 1)
    zero = jnp.zeros_like(q)
    qs = []
    for hh in range(2):
        qh = jnp.where((lane >= hh * HEAD_DIM) & (lane < (hh + 1) * HEAD_DIM), q, zero)
        if not moba:
            pick = (lane >= N_SPLIT * hh) & (lane < N_SPLIT * (hh + 1))
            qh = jnp.concatenate([qh, jnp.where(pick, 1.0, 0.0).astype(BF16)], axis=1)
        qs.append(qh)

    def scores(j, hh):
        kj = k_ref[0, pl.ds(pl.multiple_of(j * BLK, BLK), BLK), :]
        return lax.dot_general(kj, qs[hh], NT_DIMS, preferred_element_type=F32)

    def step(s, vth, state):
        mt = jnp.max(s, axis=0, keepdims=True)
        m_new = mt if state is None else jnp.maximum(state[0], mt)
        p = jnp.exp(s - m_new)
        ps = jnp.sum(p, axis=0, keepdims=True)
        pv = jnp.dot(vth, p.astype(BF16), preferred_element_type=F32)
        if state is None:
            return m_new, ps, pv
        alpha = jnp.exp(state[0] - m_new)
        return m_new, alpha * state[1] + ps, alpha * state[2] + pv

    def vt_tile(j, hh):
        return vt_ref[0, j, hh * HEAD_DIM:(hh + 1) * HEAD_DIM, :]

    init = []
    for hh in range(2):
        s = scores(i, hh)
        if moba:
            s = s + tb_ref[hh, 0]
        else:
            kidx = lax.broadcasted_iota(jnp.int32, s.shape, 0)
            qidx = lax.broadcasted_iota(jnp.int32, s.shape, 1)
            s = jnp.where(kidx <= qidx, s, NEG)
        init.append(step(s, vt_tile(i, hh), None))

    def body(j, carry):
        new = []
        for hh in range(2):
            s = scores(j, hh)
            if moba:
                kind = jnp.where(j == i - 1, 1, 2)
                s = s + tb_ref[hh, kind] + sel_ref[0, pl.ds(hh * MAX_BLOCKS + j, 1), :]
            new.append(step(s, vt_tile(j, hh), carry[hh]))
        return tuple(new)

    final = lax.fori_loop(0, i, body, tuple(init))
    ot = jnp.concatenate([acc * (1.0 / l) for (_, l, acc) in final], axis=0)
    o_ref[0] = ot.T.astype(BF16)


def _attention(q, k, vt, sel=None, tb=None):
    moba = sel is not None
    B, S, _ = q.shape
    nblk = S // BLK
    kw = k.shape[-1] // N_PAIRS
    vt_off = 0 if moba else N_PAIRS
    in_specs = [pl.BlockSpec((1, BLK, PAIR_W), lambda b, p, i: (b, i, p)),
                pl.BlockSpec((1, S, kw), lambda b, p, i: (b, 0, p)),
                pl.BlockSpec((1, nblk, PAIR_W, BLK), lambda b, p, i: (b, 0, p + vt_off, 0))]
    args = [q, k, vt]
    if moba:
        in_specs += [pl.BlockSpec((1, 2 * MAX_BLOCKS, BLK), lambda b, p, i: (b, p, i)),
                     pl.BlockSpec((2, 3, BLK, BLK), lambda b, p, i: (p, 0, 0, 0))]
        args += [sel, tb]
    return pl.pallas_call(
        functools.partial(_attn_kernel, moba=moba),
        grid=(B, N_PAIRS, nblk),
        in_specs=in_specs,
        out_specs=pl.BlockSpec((1, BLK, PAIR_W), lambda b, p, i: (b, i, p)),
        out_shape=jax.ShapeDtypeStruct((B, S, ATT_W), BF16),
        compiler_params=pltpu.CompilerParams(
            dimension_semantics=("arbitrary", "arbitrary", "arbitrary"), vmem_limit_bytes=VMEM_LIMIT),
        name="moba_attn" if moba else "fox_attn",
    )(*args)


def _out_ffn_kernel(moba_ref, fox_ref, conv_ref, x_ref, wo_ref, wg_ref, wu_ref, wd_ref,
                    gpm_ref, gpf_ref, gqf_ref, o_ref):
    mixed = (jnp.dot(moba_ref[...], wo_ref[0:ATT_W, :], preferred_element_type=F32)
             + jnp.dot(fox_ref[...], wo_ref[ATT_W:2 * ATT_W, :], preferred_element_type=F32)
             + jnp.dot(conv_ref[...], wo_ref[2 * ATT_W:, :], preferred_element_type=F32))
    x1 = x_ref[...] + _rms(mixed, gpm_ref[...])
    h2 = _rms(x1, gpf_ref[...]).astype(BF16)
    ff = jnp.zeros(x1.shape, F32)
    for c in range(0, D_FF, FFN_CHUNK):
        gate = jnp.dot(h2, wg_ref[:, c:c + FFN_CHUNK], preferred_element_type=F32)
        up = jnp.dot(h2, wu_ref[:, c:c + FFN_CHUNK], preferred_element_type=F32)
        act = (gate * (1.0 / (1.0 + jnp.exp(-gate))) * up).astype(BF16)
        ff = ff + jnp.dot(act, wd_ref[c:c + FFN_CHUNK, :], preferred_element_type=F32)
    o_ref[...] = x1 + _rms(ff, gqf_ref[...])


def _out_ffn(moba, fox, conv, x, wo, wg, wu, wd, gpm, gpf, gqf):
    M, D = x.shape
    tm = FFN_TILE
    row = lambda w: pl.BlockSpec((tm, w), lambda m: (m, 0))
    const = lambda shape: pl.BlockSpec(shape, lambda m: (0, 0), pipeline_mode=pl.Buffered(1))
    return pl.pallas_call(
        _out_ffn_kernel,
        grid=(M // tm,),
        in_specs=[row(ATT_W), row(ATT_W), row(CONV_CH), row(D),
                  const(wo.shape), const(wg.shape), const(wu.shape), const(wd.shape),
                  const((1, D)), const((1, D)), const((1, D))],
        out_specs=row(D),
        out_shape=jax.ShapeDtypeStruct((M, D), F32),
        compiler_params=pltpu.CompilerParams(
            dimension_semantics=("arbitrary",), vmem_limit_bytes=VMEM_LIMIT),
        name="out_ffn",
    )(moba, fox, conv, x, wo, wg, wu, wd, gpm, gpf, gqf)


def _cumsum_constants(tm):
    tri = np.tril(np.ones((tm, tm), np.float32))
    perm = np.zeros((N_SPLIT, LANES, ATT_W), np.float32)
    for s in range(N_SPLIT):
        for h in range(N_HEADS):
            perm[s, h, PAIR_W * (h // 2) + N_SPLIT * (h % 2) + s] = 1.0
    return jnp.asarray(tri, BF16), jnp.asarray(perm, BF16)


def kernel(x, w_in, b_f, conv_w, w_out, rel_bias, g_pre_mix, g_post_mix, g_pre_ffn, g_post_ffn,
           w_gate, w_up, w_down):
    B, S, D = x.shape
    depth = w_in.shape[0]
    assert D == D_MODEL and S % INPROJ_TILE == 0 and S // BLK <= MAX_BLOCKS
    assert (B * S) % FFN_TILE == 0

    tb = _bias_tiles(rel_bias)
    tri, perm = _cumsum_constants(INPROJ_TILE)
    a = ATT_W
    for l in range(depth):
        wl = w_in[l]
        wqk = jnp.concatenate([wl[:, 0:2 * a], wl[:, 3 * a:5 * a]], axis=1).astype(BF16)
        wvt = jnp.concatenate([wl[:, 2 * a:3 * a], wl[:, 5 * a:6 * a]], axis=1).T.astype(BF16)
        wf = jnp.pad(wl[:, 6 * a:6 * a + N_HEADS], ((0, 0), (0, LANES - N_HEADS))).astype(BF16)
        bfp = jnp.pad(b_f[l].astype(F32), (0, LANES - N_HEADS)).reshape(1, LANES)
        wc = wl[:, 6 * a + N_HEADS:].astype(BF16)
        row = lambda g: g[l].astype(F32).reshape(1, D)

        mq, mk, fq, fkp, vt, sel, conv = _inproj(
            x, row(g_pre_mix), wqk, wvt, wf, bfp, wc, conv_w[l].astype(F32), tri, perm)
        moba = _attention(mq, mk, vt, sel, tb)
        fox = _attention(fq, fkp, vt)
        x = _out_ffn(moba.reshape(B * S, a), fox.reshape(B * S, a), conv.reshape(B * S, CONV_CH),
                     x.reshape(B * S, D), w_out[l].astype(BF16), w_gate[l].astype(BF16),
                     w_up[l].astype(BF16), w_down[l].astype(BF16),
                     row(g_post_mix), row(g_pre_ffn), row(g_post_ffn)).reshape(B, S, D)
    return x
```

```python
import functools
import math

import jax
import jax.numpy as jnp
import numpy as np
from jax import lax
from jax.experimental import pallas as pl
from jax.experimental.pallas import tpu as pltpu

D_MODEL = 1024
HEAD_DIM = 64
N_HEADS = 6
ATT_W = N_HEADS * HEAD_DIM
PAIR_W = 2 * HEAD_DIM
N_PAIRS = N_HEADS // 2
CONV_CH = 256
CONV_WIDTH = 3
BLK = 256
BLK_SHIFT = BLK.bit_length() - 1
CHUNK = 512
BLOCKS_PER_CHUNK = CHUNK // BLK
MAX_BLOCKS = 16
TOPK = 3
REL_BUCKETS = 32
REL_MAX_DIST = 128
D_FF = 2816
RMS_EPS = 1e-6
NEG = -1e30
LANES = 128
N_SPLIT = 3
LOG2E = math.log2(math.e)
CORNER = REL_MAX_DIST
ONES_ROWS = 16
HEAD_ROWS = HEAD_DIM + ONES_ROWS
PAIR_ROWS = 2 * HEAD_ROWS

F32 = jnp.float32
BF16 = jnp.bfloat16
NT_DIMS = (((1,), (1,)), ((), ()))

FFN_TILE = 512
FFN_CHUNK = 1408
VMEM_LIMIT = 50 * 1024 * 1024


def _split_bf16(v, n):
    terms = []
    for _ in range(n):
        t = v.astype(BF16)
        terms.append(t)
        v = v - t.astype(F32)
    return terms


def _rms(v, g):
    return v * lax.rsqrt(jnp.mean(v * v, axis=-1, keepdims=True) + RMS_EPS) * g


def _bucket_tiles():
    max_exact = REL_BUCKETS // 2

    def bucket(dist):
        n = np.maximum(dist, 0)
        nf = np.maximum(n, 1).astype(np.float32)
        large = max_exact + (np.log(nf / np.float32(max_exact))
                             / np.float32(math.log(REL_MAX_DIST / max_exact))
                             * np.float32(REL_BUCKETS - max_exact)).astype(np.int32)
        large = np.minimum(large, REL_BUCKETS - 1)
        return np.where(n < max_exact, n, large).astype(np.int32)

    key = np.arange(CHUNK, dtype=np.int32)[:, None]
    qry = np.arange(CHUNK, dtype=np.int32)[None, :]
    d0 = qry - key
    own = np.where(d0 >= 0, bucket(d0), -1).astype(np.int32)
    prev = bucket(d0 + CHUNK)
    corner = prev[CHUNK - CORNER:, :CORNER]
    outside = prev.copy()
    outside[CHUNK - CORNER:, :CORNER] = REL_BUCKETS - 1
    assert (outside == REL_BUCKETS - 1).all()
    return own, np.ascontiguousarray(corner)


def _bias_tile_kernel(rb_ref, own_bm_ref, corner_bm_ref, own_ref, corner_ref):
    h = pl.program_id(0)
    for bm_ref, out_ref in ((own_bm_ref, own_ref), (corner_bm_ref, corner_ref)):
        bm = bm_ref[...]
        t = jnp.full(bm.shape, NEG, F32)
        for b in range(REL_BUCKETS):
            t = jnp.where(bm == b, rb_ref[b, h] * LOG2E, t)
        out_ref[0] = t


def _bias_tiles(rel_bias):
    own_bm, corner_bm = _bucket_tiles()
    return pl.pallas_call(
        _bias_tile_kernel,
        grid=(N_HEADS,),
        in_specs=[pl.BlockSpec(memory_space=pltpu.SMEM),
                  pl.BlockSpec((CHUNK, CHUNK), lambda h: (0, 0)),
                  pl.BlockSpec((CORNER, CORNER), lambda h: (0, 0))],
        out_specs=[pl.BlockSpec((1, CHUNK, CHUNK), lambda h: (h, 0, 0)),
                   pl.BlockSpec((1, CORNER, CORNER), lambda h: (h, 0, 0))],
        out_shape=[jax.ShapeDtypeStruct((N_HEADS, CHUNK, CHUNK), F32),
                   jax.ShapeDtypeStruct((N_HEADS, CORNER, CORNER), F32)],
        name="bias_tiles",
    )(rel_bias, jnp.asarray(own_bm), jnp.asarray(corner_bm))


def _inproj_kernel(x_ref, g_ref, wqkv_ref, wf_ref, bf_ref, wc_ref, cw_ref, tri_ref, perm_ref,
                   mq_ref, mk_ref, fq_ref, fkp_ref, vt_ref, sel_ref, conv_ref,
                   km_ref, cum_ref, uc_ref):
    tm = CHUNK
    st = pl.program_id(1)

    @pl.when(st == 0)
    def _():
        km_ref[...] = jnp.zeros_like(km_ref)
        cum_ref[...] = jnp.zeros_like(cum_ref)
        uc_ref[...] = jnp.zeros_like(uc_ref)

    hn = _rms(x_ref[0], g_ref[...]).astype(BF16)

    qkv = jnp.dot(hn, wqkv_ref[...], preferred_element_type=F32)
    mq = qkv[:, 0 * ATT_W:1 * ATT_W]
    mk = qkv[:, 1 * ATT_W:2 * ATT_W]
    mv = qkv[:, 2 * ATT_W:3 * ATT_W]
    fq = qkv[:, 3 * ATT_W:4 * ATT_W]
    fk = qkv[:, 4 * ATT_W:5 * ATT_W]
    fv = qkv[:, 5 * ATT_W:6 * ATT_W]
    scale = HEAD_DIM ** -0.5 * LOG2E
    mq_ref[0] = (mq * scale).astype(BF16)
    mk_ref[0] = mk.astype(BF16)
    fq_ref[0] = (fq * scale).astype(BF16)
    ones = jnp.ones((ONES_ROWS, tm), BF16)
    for kind, v in enumerate((mv, fv)):
        vt = v.T.astype(BF16)
        for h in range(N_HEADS):
            base = (kind * N_HEADS + h) * HEAD_ROWS
            vt_ref[0, 0, base:base + HEAD_DIM, :] = vt[h * HEAD_DIM:(h + 1) * HEAD_DIM]
            vt_ref[0, 0, base + HEAD_DIM:base + HEAD_ROWS, :] = ones

    rows = lax.broadcasted_iota(jnp.int32, km_ref.shape, 0)
    cols = lax.broadcasted_iota(jnp.int32, km_ref.shape, 1)
    same_head = (rows >> 4) == (cols >> 6)
    km = km_ref[...]
    for r in range(BLOCKS_PER_CHUNK):
        kmean = jnp.mean(mk[r * BLK:(r + 1) * BLK], axis=0, keepdims=True)
        n = st * BLOCKS_PER_CHUNK + r
        km = jnp.where(same_head & ((rows & (MAX_BLOCKS - 1)) == n), kmean, km)
    km_ref[...] = km

    km_hi, km_lo = _split_bf16(km, 2)
    q_hi, q_lo = _split_bf16(mq, 2)
    gate = (lax.dot_general(km_hi, q_hi, NT_DIMS, preferred_element_type=F32)
            + lax.dot_general(km_hi, q_lo, NT_DIMS, preferred_element_type=F32)
            + lax.dot_general(km_lo, q_hi, NT_DIMS, preferred_element_type=F32))

    nidx = lax.broadcasted_iota(jnp.int32, (MAX_BLOCKS, tm), 0)
    own = st * BLOCKS_PER_CHUNK + (lax.broadcasted_iota(jnp.int32, (MAX_BLOCKS, tm), 1) >> BLK_SHIFT)
    valid = nidx < own
    for h in range(N_HEADS):
        gh = jnp.where(valid, gate[h * MAX_BLOCKS:(h + 1) * MAX_BLOCKS], -jnp.inf)
        rank = jnp.zeros((MAX_BLOCKS, tm), jnp.int32)
        for m in range(MAX_BLOCKS):
            gm = gh[m:m + 1, :]
            gt = (gm > gh).astype(jnp.int32)
            ge = (gm >= gh).astype(jnp.int32)
            rank = rank + jnp.where(nidx > m, ge, gt)
        keep = ((rank < TOPK) & valid) | (nidx == own)
        sel_ref[0, h * MAX_BLOCKS:(h + 1) * MAX_BLOCKS, :] = jnp.where(keep, 0.0, NEG)

    fl = jnp.dot(hn, wf_ref[...], preferred_element_type=F32) + bf_ref[...]
    lf = jnp.minimum(fl, 0.0) - jnp.log1p(jnp.exp(-jnp.abs(fl)))
    tri = tri_ref[...]
    cum = cum_ref[...]
    for t in _split_bf16(lf, N_SPLIT):
        cum = cum + jnp.dot(tri, t, preferred_element_type=F32)
    cum_ref[...] = cum[tm - 1:tm, :]
    kb = jnp.zeros((tm, ATT_W), F32)
    for s, t in enumerate(_split_bf16(cum * -LOG2E, N_SPLIT)):
        kb = kb + jnp.dot(t, perm_ref[s], preferred_element_type=F32)
    kb = kb.astype(BF16)
    fkb = fk.astype(BF16)
    for p in range(N_PAIRS):
        fkp_ref[0, :, 2 * p * PAIR_W:(2 * p + 1) * PAIR_W] = fkb[:, p * PAIR_W:(p + 1) * PAIR_W]
        fkp_ref[0, :, (2 * p + 1) * PAIR_W:(2 * p + 2) * PAIR_W] = kb[:, p * PAIR_W:(p + 1) * PAIR_W]

    cv = jnp.dot(hn, wc_ref[...], preferred_element_type=F32)
    cvb = cv[:, 0:CONV_CH]
    u = cv[:, CONV_CH:2 * CONV_CH] * cv[:, 2 * CONV_CH:3 * CONV_CH]
    uc = uc_ref[...]
    prev1 = uc[7:8]
    prev2 = uc[6:7]
    ridx = lax.broadcasted_iota(jnp.int32, u.shape, 0)
    u1 = jnp.where(ridx == 0, prev1, pltpu.roll(u, 1, 0))
    u2 = jnp.where(ridx == 0, prev2, jnp.where(ridx == 1, prev1, pltpu.roll(u, 2, 0)))
    cw = cw_ref[...]
    y = cw[0:1] * u2 + cw[1:2] * u1 + cw[2:3] * u
    conv_ref[0] = (cvb * y).astype(BF16)
    uc_ref[...] = u[tm - 8:tm]


def _inproj(x, g, wqkv, wf, bfp, wc, cw, tri, perm):
    B, S, D = x.shape
    tm = CHUNK
    const = lambda shape: pl.BlockSpec(shape, lambda b, s: (0,) * len(shape))
    row = lambda w: pl.BlockSpec((1, tm, w), lambda b, s: (b, s, 0))
    return pl.pallas_call(
        _inproj_kernel,
        grid=(B, S // tm),
        in_specs=[row(D), const((1, D)), const(wqkv.shape), const(wf.shape),
                  const(bfp.shape), const(wc.shape), const(cw.shape), const(tri.shape), const(perm.shape)],
        out_specs=[row(ATT_W), row(ATT_W), row(ATT_W), row(2 * ATT_W),
                   pl.BlockSpec((1, 1, 2 * N_HEADS * HEAD_ROWS, tm), lambda b, s: (b, s, 0, 0)),
                   pl.BlockSpec((1, N_HEADS * MAX_BLOCKS, tm), lambda b, s: (b, 0, s)),
                   row(CONV_CH)],
        out_shape=[jax.ShapeDtypeStruct((B, S, ATT_W), BF16),
                   jax.ShapeDtypeStruct((B, S, ATT_W), BF16),
                   jax.ShapeDtypeStruct((B, S, ATT_W), BF16),
                   jax.ShapeDtypeStruct((B, S, 2 * ATT_W), BF16),
                   jax.ShapeDtypeStruct((B, S // tm, 2 * N_HEADS * HEAD_ROWS, tm), BF16),
                   jax.ShapeDtypeStruct((B, N_HEADS * MAX_BLOCKS, S), F32),
                   jax.ShapeDtypeStruct((B, S, CONV_CH), BF16)],
        scratch_shapes=[pltpu.VMEM((N_HEADS * MAX_BLOCKS, ATT_W), F32),
                        pltpu.VMEM((1, LANES), F32),
                        pltpu.VMEM((8, CONV_CH), F32)],
        compiler_params=pltpu.CompilerParams(
            dimension_semantics=("arbitrary", "arbitrary"), vmem_limit_bytes=VMEM_LIMIT),
        name="inproj",
    )(x, g, wqkv, wf, bfp, wc, cw, tri, perm)


def _attn_kernel(*refs, moba):
    if moba:
        rb_ref, q_ref, k_ref, vt_ref, sel_ref, own_ref, corner_ref, o_ref, sa_ref, sb_ref = refs
    else:
        q_ref, k_ref, vt_ref, o_ref, sa_ref, sb_ref = refs
    pair = pl.program_id(1)
    qi = pl.program_id(2)
    q = q_ref[0]
    lane = lax.broadcasted_iota(jnp.int32, q.shape, 1)
    zero = jnp.zeros_like(q)
    qs = []
    for hh in range(2):
        qh = jnp.where((lane >= hh * HEAD_DIM) & (lane < (hh + 1) * HEAD_DIM), q, zero)
        if not moba:
            pick = (lane >= N_SPLIT * hh) & (lane < N_SPLIT * (hh + 1))
            qh = jnp.concatenate([qh, jnp.where(pick, 1.0, 0.0).astype(BF16)], axis=1)
        qs.append(qh)

    if moba:
        far_bias = [rb_ref[REL_BUCKETS - 1, 2 * pair + hh] * LOG2E for hh in range(2)]

    def produce(c, dst_ref, own=False):
        kc = k_ref[0, pl.ds(pl.multiple_of(c * CHUNK, CHUNK), CHUNK), :]
        maxima = []
        for hh in range(2):
            s = lax.dot_general(kc, qs[hh], NT_DIMS, preferred_element_type=F32)
            if moba:
                parts = []
                for r in range(BLOCKS_PER_CHUNK):
                    row = sel_ref[0, pl.ds(hh * MAX_BLOCKS + c * BLOCKS_PER_CHUNK + r, 1), :]
                    sh = s[r * BLK:(r + 1) * BLK]
                    if own:
                        parts.append(sh + own_ref[hh, r * BLK:(r + 1) * BLK, :] + row)
                    elif r < BLOCKS_PER_CHUNK - 1:
                        parts.append(sh + (row + far_bias[hh]))
                    else:
                        far_row = row + far_bias[hh]
                        top = sh[:BLK - CORNER] + far_row
                        near = jnp.where(c == qi - 1, corner_ref[hh], far_bias[hh])
                        bot = jnp.concatenate(
                            [sh[BLK - CORNER:, :CORNER] + near + row[:, :CORNER],
                             sh[BLK - CORNER:, CORNER:] + far_row[:, CORNER:]], axis=1)
                        parts += [top, bot]
                s = jnp.concatenate(parts, axis=0)
            elif own:
                kidx = lax.broadcasted_iota(jnp.int32, s.shape, 0)
                qidx = lax.broadcasted_iota(jnp.int32, s.shape, 1)
                s = jnp.where(kidx <= qidx, s, NEG)
            dst_ref[hh] = s
            maxima.append(jnp.max(s, axis=0, keepdims=True))
        return tuple(maxima)

    def consume(c, src_ref, maxima, state):
        new = []
        for hh in range(2):
            m_old, acc = state[hh]
            m_new = jnp.maximum(m_old, maxima[hh])
            p = jnp.exp2(src_ref[hh] - m_new).astype(BF16)
            vth = vt_ref[0, c, hh * HEAD_ROWS:(hh + 1) * HEAD_ROWS, :]
            pv = jnp.dot(vth, p, preferred_element_type=F32)
            new.append((m_new, jnp.exp2(m_old - m_new) * acc + pv))
        return tuple(new)

    def stage(c, src_ref, dst_ref, maxima, state):
        nxt = produce(c - 1, dst_ref)
        return consume(c, src_ref, maxima, state), nxt

    def two_stages(u, carry):
        state, mx_a = carry
        c = qi - 2 * u
        state, mx_b = stage(c, sa_ref, sb_ref, mx_a, state)
        return stage(c - 1, sb_ref, sa_ref, mx_b, state)

    def odd_tail(carry):
        state, mx_a = carry
        state, mx_b = stage(1, sa_ref, sb_ref, mx_a, state)
        return consume(0, sb_ref, mx_b, state)

    def even_tail(carry):
        state, mx_a = carry
        return consume(0, sa_ref, mx_a, state)

    init = tuple((jnp.full((1, CHUNK), NEG, F32), jnp.zeros((HEAD_ROWS, CHUNK), F32)) for _ in range(2))
    carry = (init, produce(qi, sa_ref, own=True))
    carry = lax.fori_loop(0, qi // 2, two_stages, carry)
    state = lax.cond(qi % 2 == 1, odd_tail, even_tail, carry)
    ot = jnp.concatenate([acc[:HEAD_DIM] * (1.0 / acc[HEAD_DIM:HEAD_DIM + 1]) for (_, acc) in state], axis=0)
    o_ref[0] = ot.T.astype(BF16)


def _attention(q, k, vt, sel=None, tables=None, rel_bias=None):
    moba = sel is not None
    B, S, _ = q.shape
    nchunk = S // CHUNK
    kw = k.shape[-1] // N_PAIRS
    vt_off = 0 if moba else N_PAIRS
    in_specs = [pl.BlockSpec((1, CHUNK, PAIR_W), lambda b, p, i: (b, i, p)),
                pl.BlockSpec((1, S, kw), lambda b, p, i: (b, 0, p)),
                pl.BlockSpec((1, nchunk, PAIR_ROWS, CHUNK), lambda b, p, i: (b, 0, p + vt_off, 0))]
    args = [q, k, vt]
    if moba:
        in_specs = [pl.BlockSpec(memory_space=pltpu.SMEM)] + in_specs
        in_specs += [pl.BlockSpec((1, 2 * MAX_BLOCKS, CHUNK), lambda b, p, i: (b, p, i)),
                     pl.BlockSpec((2, CHUNK, CHUNK), lambda b, p, i: (p, 0, 0)),
                     pl.BlockSpec((2, CORNER, CORNER), lambda b, p, i: (p, 0, 0))]
        args = [rel_bias] + args + [sel, *tables]
    return pl.pallas_call(
        functools.partial(_attn_kernel, moba=moba),
        grid=(B, N_PAIRS, nchunk),
        in_specs=in_specs,
        out_specs=pl.BlockSpec((1, CHUNK, PAIR_W), lambda b, p, i: (b, i, p)),
        out_shape=jax.ShapeDtypeStruct((B, S, ATT_W), BF16),
        scratch_shapes=[pltpu.VMEM((2, CHUNK, CHUNK), F32), pltpu.VMEM((2, CHUNK, CHUNK), F32)],
        compiler_params=pltpu.CompilerParams(
            dimension_semantics=("arbitrary", "arbitrary", "arbitrary"), vmem_limit_bytes=VMEM_LIMIT),
        name="moba_attn" if moba else "fox_attn",
    )(*args)


def _out_ffn_kernel(moba_ref, fox_ref, conv_ref, x_ref, wo_ref, wg_ref, wu_ref, wd_ref,
                    gpm_ref, gpf_ref, gqf_ref, o_ref):
    mixed = (jnp.dot(moba_ref[...], wo_ref[0:ATT_W, :], preferred_element_type=F32)
             + jnp.dot(fox_ref[...], wo_ref[ATT_W:2 * ATT_W, :], preferred_element_type=F32)
             + jnp.dot(conv_ref[...], wo_ref[2 * ATT_W:, :], preferred_element_type=F32))
    x1 = x_ref[...] + _rms(mixed, gpm_ref[...])
    h2 = _rms(x1, gpf_ref[...]).astype(BF16)
    ff = jnp.zeros(x1.shape, F32)
    for c in range(0, D_FF, FFN_CHUNK):
        gate = jnp.dot(h2, wg_ref[:, c:c + FFN_CHUNK], preferred_element_type=F32)
        up = jnp.dot(h2, wu_ref[:, c:c + FFN_CHUNK], preferred_element_type=F32)
        act = (gate * (1.0 / (1.0 + jnp.exp(-gate))) * up).astype(BF16)
        ff = ff + jnp.dot(act, wd_ref[c:c + FFN_CHUNK, :], preferred_element_type=F32)
    o_ref[...] = x1 + _rms(ff, gqf_ref[...])


def _out_ffn(moba, fox, conv, x, wo, wg, wu, wd, gpm, gpf, gqf):
    M, D = x.shape
    tm = FFN_TILE
    row = lambda w: pl.BlockSpec((tm, w), lambda m: (m, 0))
    const = lambda shape: pl.BlockSpec(shape, lambda m: (0, 0), pipeline_mode=pl.Buffered(1))
    return pl.pallas_call(
        _out_ffn_kernel,
        grid=(M // tm,),
        in_specs=[row(ATT_W), row(ATT_W), row(CONV_CH), row(D),
                  const(wo.shape), const(wg.shape), const(wu.shape), const(wd.shape),
                  const((1, D)), const((1, D)), const((1, D))],
        out_specs=row(D),
        out_shape=jax.ShapeDtypeStruct((M, D), F32),
        compiler_params=pltpu.CompilerParams(
            dimension_semantics=("arbitrary",), vmem_limit_bytes=VMEM_LIMIT),
        name="out_ffn",
    )(moba, fox, conv, x, wo, wg, wu, wd, gpm, gpf, gqf)


def _cumsum_constants(tm):
    tri = np.tril(np.ones((tm, tm), np.float32))
    perm = np.zeros((N_SPLIT, LANES, ATT_W), np.float32)
    for s in range(N_SPLIT):
        for h in range(N_HEADS):
            perm[s, h, PAIR_W * (h // 2) + N_SPLIT * (h % 2) + s] = 1.0
    return jnp.asarray(tri, BF16), jnp.asarray(perm, BF16)


def kernel(x, w_in, b_f, conv_w, w_out, rel_bias, g_pre_mix, g_post_mix, g_pre_ffn, g_post_ffn,
           w_gate, w_up, w_down):
    B, S, D = x.shape
    depth = w_in.shape[0]
    assert D == D_MODEL and S % CHUNK == 0 and S // BLK <= MAX_BLOCKS
    assert (B * S) % FFN_TILE == 0

    rel_bias = rel_bias.astype(F32)
    tables = _bias_tiles(rel_bias)
    tri, perm = _cumsum_constants(CHUNK)
    a = ATT_W
    for l in range(depth):
        wl = w_in[l]
        wqkv = wl[:, 0:6 * a].astype(BF16)
        wf = jnp.pad(wl[:, 6 * a:6 * a + N_HEADS], ((0, 0), (0, LANES - N_HEADS))).astype(BF16)
        bfp = jnp.pad(b_f[l].astype(F32), (0, LANES - N_HEADS)).reshape(1, LANES)
        wc = wl[:, 6 * a + N_HEADS:].astype(BF16)
        row = lambda g: g[l].astype(F32).reshape(1, D)

        mq, mk, fq, fkp, vt, sel, conv = _inproj(
            x, row(g_pre_mix), wqkv, wf, bfp, wc, conv_w[l].astype(F32), tri, perm)
        moba = _attention(mq, mk, vt, sel, tables, rel_bias)
        fox = _attention(fq, fkp, vt)
        x = _out_ffn(moba.reshape(B * S, a), fox.reshape(B * S, a), conv.reshape(B * S, CONV_CH),
                     x.reshape(B * S, D), w_out[l].astype(BF16), w_gate[l].astype(BF16),
                     w_up[l].astype(BF16), w_down[l].astype(BF16),
                     row(g_post_mix), row(g_pre_ffn), row(g_post_ffn)).reshape(B, S, D)
    return x
```

```python
import functools
import math

import jax
import jax.numpy as jnp
import numpy as np
from jax import lax
from jax.experimental import pallas as pl
from jax.experimental.pallas import tpu as pltpu

D_MODEL = 1024
HEAD_DIM = 64
N_HEADS = 6
ATT_W = N_HEADS * HEAD_DIM
PAIR_W = 2 * HEAD_DIM
N_PAIRS = N_HEADS // 2
CONV_CH = 256
CONV_WIDTH = 3
BLK = 256
BLK_SHIFT = BLK.bit_length() - 1
CHUNK = 512
BLOCKS_PER_CHUNK = CHUNK // BLK
MAX_BLOCKS = 16
TOPK = 3
REL_BUCKETS = 32
REL_MAX_DIST = 128
D_FF = 2816
RMS_EPS = 1e-6
NEG = -1e30
LANES = 128
N_SPLIT = 3
LOG2E = math.log2(math.e)
CORNER = REL_MAX_DIST
ONES_ROWS = 16
HEAD_ROWS = HEAD_DIM + ONES_ROWS
PAIR_ROWS = 2 * HEAD_ROWS

F32 = jnp.float32
BF16 = jnp.bfloat16
NT_DIMS = (((1,), (1,)), ((), ()))

FFN_TILE = 512
FFN_CHUNKS = (1536, 1280)
assert sum(FFN_CHUNKS) == D_FF
VMEM_LIMIT = 50 * 1024 * 1024


def _split_bf16(v, n):
    terms = []
    for _ in range(n):
        t = v.astype(BF16)
        terms.append(t)
        v = v - t.astype(F32)
    return terms


def _rms(v, g):
    return v * lax.rsqrt(jnp.mean(v * v, axis=-1, keepdims=True) + RMS_EPS) * g


def _bucket_tiles():
    max_exact = REL_BUCKETS // 2

    def bucket(dist):
        n = np.maximum(dist, 0)
        nf = np.maximum(n, 1).astype(np.float32)
        large = max_exact + (np.log(nf / np.float32(max_exact))
                             / np.float32(math.log(REL_MAX_DIST / max_exact))
                             * np.float32(REL_BUCKETS - max_exact)).astype(np.int32)
        large = np.minimum(large, REL_BUCKETS - 1)
        return np.where(n < max_exact, n, large).astype(np.int32)

    key = np.arange(CHUNK, dtype=np.int32)[:, None]
    qry = np.arange(CHUNK, dtype=np.int32)[None, :]
    d0 = qry - key
    own = np.where(d0 >= 0, bucket(d0), -1).astype(np.int32)
    prev = bucket(d0 + CHUNK)
    corner = prev[CHUNK - CORNER:, :CORNER]
    outside = prev.copy()
    outside[CHUNK - CORNER:, :CORNER] = REL_BUCKETS - 1
    assert (outside == REL_BUCKETS - 1).all()
    return own, np.ascontiguousarray(corner)


def _bias_tile_kernel(rb_ref, own_bm_ref, corner_bm_ref, own_ref, corner_ref):
    h = pl.program_id(0)
    for bm_ref, out_ref in ((own_bm_ref, own_ref), (corner_bm_ref, corner_ref)):
        bm = bm_ref[...]
        t = jnp.full(bm.shape, NEG, F32)
        for b in range(REL_BUCKETS):
            t = jnp.where(bm == b, rb_ref[b, h] * LOG2E, t)
        out_ref[0] = t


def _bias_tiles(rel_bias):
    own_bm, corner_bm = _bucket_tiles()
    return pl.pallas_call(
        _bias_tile_kernel,
        grid=(N_HEADS,),
        in_specs=[pl.BlockSpec(memory_space=pltpu.SMEM),
                  pl.BlockSpec((CHUNK, CHUNK), lambda h: (0, 0)),
                  pl.BlockSpec((CORNER, CORNER), lambda h: (0, 0))],
        out_specs=[pl.BlockSpec((1, CHUNK, CHUNK), lambda h: (h, 0, 0)),
                   pl.BlockSpec((1, CORNER, CORNER), lambda h: (h, 0, 0))],
        out_shape=[jax.ShapeDtypeStruct((N_HEADS, CHUNK, CHUNK), F32),
                   jax.ShapeDtypeStruct((N_HEADS, CORNER, CORNER), F32)],
        name="bias_tiles",
    )(rel_bias, jnp.asarray(own_bm), jnp.asarray(corner_bm))


def _inproj_kernel(x_ref, g_ref, wqkv_ref, wtail_ref, bf_ref, cw_ref, tri_ref, perm_ref,
                   mq_ref, mk_ref, fq_ref, fkp_ref, vt_ref, sel_ref, conv_ref,
                   km_ref, cum_ref, uc_ref):
    tm = CHUNK
    st = pl.program_id(1)

    @pl.when(st == 0)
    def _():
        km_ref[...] = jnp.zeros_like(km_ref)
        cum_ref[...] = jnp.zeros_like(cum_ref)
        uc_ref[...] = jnp.zeros_like(uc_ref)

    hn = _rms(x_ref[0], g_ref[...]).astype(BF16)

    qkv = lax.dot_general(hn, wqkv_ref[...], NT_DIMS, preferred_element_type=F32)
    tail = lax.dot_general(hn, wtail_ref[...], NT_DIMS, preferred_element_type=F32)
    mq = qkv[:, 0 * ATT_W:1 * ATT_W]
    mk = qkv[:, 1 * ATT_W:2 * ATT_W]
    mv = qkv[:, 2 * ATT_W:3 * ATT_W]
    fq = qkv[:, 3 * ATT_W:4 * ATT_W]
    fk = qkv[:, 4 * ATT_W:5 * ATT_W]
    fv = qkv[:, 5 * ATT_W:6 * ATT_W]
    scale = HEAD_DIM ** -0.5 * LOG2E
    mq_ref[0] = (mq * scale).astype(BF16)
    mk_ref[0] = mk.astype(BF16)
    fq_ref[0] = (fq * scale).astype(BF16)
    ones = jnp.ones((ONES_ROWS, tm), BF16)
    for kind, v in enumerate((mv, fv)):
        vt = v.T.astype(BF16)
        for h in range(N_HEADS):
            base = (kind * N_HEADS + h) * HEAD_ROWS
            vt_ref[0, 0, base:base + HEAD_DIM, :] = vt[h * HEAD_DIM:(h + 1) * HEAD_DIM]
            vt_ref[0, 0, base + HEAD_DIM:base + HEAD_ROWS, :] = ones

    rows = lax.broadcasted_iota(jnp.int32, km_ref.shape, 0)
    cols = lax.broadcasted_iota(jnp.int32, km_ref.shape, 1)
    same_head = (rows >> 4) == (cols >> 6)
    km = km_ref[...]
    for r in range(BLOCKS_PER_CHUNK):
        kmean = jnp.mean(mk[r * BLK:(r + 1) * BLK], axis=0, keepdims=True)
        n = st * BLOCKS_PER_CHUNK + r
        km = jnp.where(same_head & ((rows & (MAX_BLOCKS - 1)) == n), kmean, km)
    km_ref[...] = km

    km_hi, km_lo = _split_bf16(km, 2)
    q_hi, q_lo = _split_bf16(mq, 2)
    gate = (lax.dot_general(km_hi, q_hi, NT_DIMS, preferred_element_type=F32)
            + lax.dot_general(km_hi, q_lo, NT_DIMS, preferred_element_type=F32)
            + lax.dot_general(km_lo, q_hi, NT_DIMS, preferred_element_type=F32))

    nidx = lax.broadcasted_iota(jnp.int32, (MAX_BLOCKS, tm), 0)
    own = st * BLOCKS_PER_CHUNK + (lax.broadcasted_iota(jnp.int32, (MAX_BLOCKS, tm), 1) >> BLK_SHIFT)
    valid = nidx < own
    for h in range(N_HEADS):
        gh = jnp.where(valid, gate[h * MAX_BLOCKS:(h + 1) * MAX_BLOCKS], -jnp.inf)
        rank = jnp.zeros((MAX_BLOCKS, tm), jnp.int32)
        for m in range(MAX_BLOCKS):
            gm = gh[m:m + 1, :]
            gt = (gm > gh).astype(jnp.int32)
            ge = (gm >= gh).astype(jnp.int32)
            rank = rank + jnp.where(nidx > m, ge, gt)
        keep = ((rank < TOPK) & valid) | (nidx == own)
        sel_ref[0, h * MAX_BLOCKS:(h + 1) * MAX_BLOCKS, :] = jnp.where(keep, 0.0, NEG)

    fl = tail[:, 0:LANES] + bf_ref[...]
    lf = jnp.minimum(fl, 0.0) - jnp.log1p(jnp.exp(-jnp.abs(fl)))
    tri = tri_ref[...]
    cum = cum_ref[...]
    for t in _split_bf16(lf, N_SPLIT):
        cum = cum + jnp.dot(tri, t, preferred_element_type=F32)
    cum_ref[...] = cum[tm - 1:tm, :]
    kb = jnp.zeros((tm, ATT_W), F32)
    for s, t in enumerate(_split_bf16(cum * -LOG2E, N_SPLIT)):
        kb = kb + jnp.dot(t, perm_ref[s], preferred_element_type=F32)
    kb = kb.astype(BF16)
    fkb = fk.astype(BF16)
    for p in range(N_PAIRS):
        fkp_ref[0, :, 2 * p * PAIR_W:(2 * p + 1) * PAIR_W] = fkb[:, p * PAIR_W:(p + 1) * PAIR_W]
        fkp_ref[0, :, (2 * p + 1) * PAIR_W:(2 * p + 2) * PAIR_W] = kb[:, p * PAIR_W:(p + 1) * PAIR_W]

    cv = tail[:, LANES:]
    cvb = cv[:, 0:CONV_CH]
    u = cv[:, CONV_CH:2 * CONV_CH] * cv[:, 2 * CONV_CH:3 * CONV_CH]
    uc = uc_ref[...]
    prev1 = uc[7:8]
    prev2 = uc[6:7]
    ridx = lax.broadcasted_iota(jnp.int32, u.shape, 0)
    u1 = jnp.where(ridx == 0, prev1, pltpu.roll(u, 1, 0))
    u2 = jnp.where(ridx == 0, prev2, jnp.where(ridx == 1, prev1, pltpu.roll(u, 2, 0)))
    cw = cw_ref[...]
    y = cw[0:1] * u2 + cw[1:2] * u1 + cw[2:3] * u
    conv_ref[0] = (cvb * y).astype(BF16)
    uc_ref[...] = u[tm - 8:tm]


def _inproj(x, g, wqkv, wtail, bfp, cw, tri, perm):
    B, S, D = x.shape
    tm = CHUNK
    const = lambda shape: pl.BlockSpec(shape, lambda b, s: (0,) * len(shape))
    row = lambda w: pl.BlockSpec((1, tm, w), lambda b, s: (b, s, 0))
    return pl.pallas_call(
        _inproj_kernel,
        grid=(B, S // tm),
        in_specs=[row(D), const((1, D)), const(wqkv.shape), const(wtail.shape),
                  const(bfp.shape), const(cw.shape), const(tri.shape), const(perm.shape)],
        out_specs=[row(ATT_W), row(ATT_W), row(ATT_W), row(2 * ATT_W),
                   pl.BlockSpec((1, 1, 2 * N_HEADS * HEAD_ROWS, tm), lambda b, s: (b, s, 0, 0)),
                   pl.BlockSpec((1, N_HEADS * MAX_BLOCKS, tm), lambda b, s: (b, 0, s)),
                   row(CONV_CH)],
        out_shape=[jax.ShapeDtypeStruct((B, S, ATT_W), BF16),
                   jax.ShapeDtypeStruct((B, S, ATT_W), BF16),
                   jax.ShapeDtypeStruct((B, S, ATT_W), BF16),
                   jax.ShapeDtypeStruct((B, S, 2 * ATT_W), BF16),
                   jax.ShapeDtypeStruct((B, S // tm, 2 * N_HEADS * HEAD_ROWS, tm), BF16),
                   jax.ShapeDtypeStruct((B, N_HEADS * MAX_BLOCKS, S), F32),
                   jax.ShapeDtypeStruct((B, S, CONV_CH), BF16)],
        scratch_shapes=[pltpu.VMEM((N_HEADS * MAX_BLOCKS, ATT_W), F32),
                        pltpu.VMEM((1, LANES), F32),
                        pltpu.VMEM((8, CONV_CH), F32)],
        compiler_params=pltpu.CompilerParams(
            dimension_semantics=("arbitrary", "arbitrary"), vmem_limit_bytes=VMEM_LIMIT),
        name="inproj",
    )(x, g, wqkv, wtail, bfp, cw, tri, perm)


def _attn_kernel(*refs, moba):
    if moba:
        rb_ref, q_ref, k_ref, vt_ref, sel_ref, own_ref, corner_ref, o_ref, sa_ref, sb_ref = refs
    else:
        q_ref, k_ref, vt_ref, o_ref, sa_ref, sb_ref = refs
    pair = pl.program_id(1)
    qi = pl.program_id(2)
    q = q_ref[0]
    lane = lax.broadcasted_iota(jnp.int32, q.shape, 1)
    zero = jnp.zeros_like(q)
    qs = []
    for hh in range(2):
        qh = jnp.where((lane >= hh * HEAD_DIM) & (lane < (hh + 1) * HEAD_DIM), q, zero)
        if not moba:
            pick = (lane >= N_SPLIT * hh) & (lane < N_SPLIT * (hh + 1))
            qh = jnp.concatenate([qh, jnp.where(pick, 1.0, 0.0).astype(BF16)], axis=1)
        qs.append(qh)

    if moba:
        far_bias = [rb_ref[REL_BUCKETS - 1, 2 * pair + hh] * LOG2E for hh in range(2)]

    def produce(c, dst_ref, own=False):
        kc = k_ref[0, pl.ds(pl.multiple_of(c * CHUNK, CHUNK), CHUNK), :]
        maxima = []
        for hh in range(2):
            s = lax.dot_general(kc, qs[hh], NT_DIMS, preferred_element_type=F32)
            if moba:
                parts = []
                for r in range(BLOCKS_PER_CHUNK):
                    row = sel_ref[0, pl.ds(hh * MAX_BLOCKS + c * BLOCKS_PER_CHUNK + r, 1), :]
                    sh = s[r * BLK:(r + 1) * BLK]
                    if own:
                        parts.append(sh + own_ref[hh, r * BLK:(r + 1) * BLK, :] + row)
                    elif r < BLOCKS_PER_CHUNK - 1:
                        parts.append(sh + (row + far_bias[hh]))
                    else:
                        far_row = row + far_bias[hh]
                        top = sh[:BLK - CORNER] + far_row
                        near = jnp.where(c == qi - 1, corner_ref[hh], far_bias[hh])
                        bot = jnp.concatenate(
                            [sh[BLK - CORNER:, :CORNER] + near + row[:, :CORNER],
                             sh[BLK - CORNER:, CORNER:] + far_row[:, CORNER:]], axis=1)
                        parts += [top, bot]
                s = jnp.concatenate(parts, axis=0)
            elif own:
                kidx = lax.broadcasted_iota(jnp.int32, s.shape, 0)
                qidx = lax.broadcasted_iota(jnp.int32, s.shape, 1)
                s = jnp.where(kidx <= qidx, s, NEG)
            dst_ref[hh] = s
            maxima.append(jnp.max(s, axis=0, keepdims=True))
        return tuple(maxima)

    def consume(c, src_ref, maxima, state):
        new = []
        for hh in range(2):
            m_old, acc = state[hh]
            m_new = jnp.maximum(m_old, maxima[hh])
            p = jnp.exp2(src_ref[hh] - m_new).astype(BF16)
            vth = vt_ref[0, c, hh * HEAD_ROWS:(hh + 1) * HEAD_ROWS, :]
            pv = jnp.dot(vth, p, preferred_element_type=F32)
            new.append((m_new, jnp.exp2(m_old - m_new) * acc + pv))
        return tuple(new)

    def stage(c, src_ref, dst_ref, maxima, state):
        nxt = produce(c - 1, dst_ref)
        return consume(c, src_ref, maxima, state), nxt

    def two_stages(u, carry):
        state, mx_a = carry
        c = qi - 2 * u
        state, mx_b = stage(c, sa_ref, sb_ref, mx_a, state)
        return stage(c - 1, sb_ref, sa_ref, mx_b, state)

    def odd_tail(carry):
        state, mx_a = carry
        state, mx_b = stage(1, sa_ref, sb_ref, mx_a, state)
        return consume(0, sb_ref, mx_b, state)

    def even_tail(carry):
        state, mx_a = carry
        return consume(0, sa_ref, mx_a, state)

    init = tuple((jnp.full((1, CHUNK), NEG, F32), jnp.zeros((HEAD_ROWS, CHUNK), F32)) for _ in range(2))
    carry = (init, produce(qi, sa_ref, own=True))
    carry = lax.fori_loop(0, qi // 2, two_stages, carry)
    state = lax.cond(qi % 2 == 1, odd_tail, even_tail, carry)
    ot = jnp.concatenate([acc[:HEAD_DIM] * (1.0 / acc[HEAD_DIM:HEAD_DIM + 1]) for (_, acc) in state], axis=0)
    o_ref[0] = ot.T.astype(BF16)


def _attention(q, k, vt, sel=None, tables=None, rel_bias=None):
    moba = sel is not None
    B, S, _ = q.shape
    nchunk = S // CHUNK
    kw = k.shape[-1] // N_PAIRS
    vt_off = 0 if moba else N_PAIRS
    in_specs = [pl.BlockSpec((1, CHUNK, PAIR_W), lambda b, p, i: (b, i, p)),
                pl.BlockSpec((1, S, kw), lambda b, p, i: (b, 0, p)),
                pl.BlockSpec((1, nchunk, PAIR_ROWS, CHUNK), lambda b, p, i: (b, 0, p + vt_off, 0))]
    args = [q, k, vt]
    if moba:
        in_specs = [pl.BlockSpec(memory_space=pltpu.SMEM)] + in_specs
        in_specs += [pl.BlockSpec((1, 2 * MAX_BLOCKS, CHUNK), lambda b, p, i: (b, p, i)),
                     pl.BlockSpec((2, CHUNK, CHUNK), lambda b, p, i: (p, 0, 0)),
                     pl.BlockSpec((2, CORNER, CORNER), lambda b, p, i: (p, 0, 0))]
        args = [rel_bias] + args + [sel, *tables]
    return pl.pallas_call(
        functools.partial(_attn_kernel, moba=moba),
        grid=(B, N_PAIRS, nchunk),
        in_specs=in_specs,
        out_specs=pl.BlockSpec((1, CHUNK, PAIR_W), lambda b, p, i: (b, i, p)),
        out_shape=jax.ShapeDtypeStruct((B, S, ATT_W), BF16),
        scratch_shapes=[pltpu.VMEM((2, CHUNK, CHUNK), F32), pltpu.VMEM((2, CHUNK, CHUNK), F32)],
        compiler_params=pltpu.CompilerParams(
            dimension_semantics=("arbitrary", "arbitrary", "arbitrary"), vmem_limit_bytes=VMEM_LIMIT),
        name="moba_attn" if moba else "fox_attn",
    )(*args)


def _out_ffn_kernel(moba_ref, fox_ref, conv_ref, x_ref, wo_ref, wg_ref, wu_ref, wd_ref,
                    gpm_ref, gpf_ref, gqf_ref, o_ref):
    mix_in = jnp.concatenate([moba_ref[...], fox_ref[...], conv_ref[...]], axis=1)
    mixed = jnp.dot(mix_in, wo_ref[...], preferred_element_type=F32)
    x1 = x_ref[...] + _rms(mixed, gpm_ref[...])
    h2 = _rms(x1, gpf_ref[...]).astype(BF16)
    ff = jnp.zeros(x1.shape, F32)
    lo = 0
    for width in FFN_CHUNKS:
        gate = jnp.dot(h2, wg_ref[:, lo:lo + width], preferred_element_type=F32)
        up = jnp.dot(h2, wu_ref[:, lo:lo + width], preferred_element_type=F32)
        act = (gate * (1.0 / (1.0 + jnp.exp(-gate))) * up).astype(BF16)
        ff = ff + jnp.dot(act, wd_ref[lo:lo + width, :], preferred_element_type=F32)
        lo += width
    o_ref[...] = x1 + _rms(ff, gqf_ref[...])


def _out_ffn(moba, fox, conv, x, wo, wg, wu, wd, gpm, gpf, gqf):
    M, D = x.shape
    tm = FFN_TILE
    row = lambda w: pl.BlockSpec((tm, w), lambda m: (m, 0))
    const = lambda shape: pl.BlockSpec(shape, lambda m: (0, 0), pipeline_mode=pl.Buffered(1))
    return pl.pallas_call(
        _out_ffn_kernel,
        grid=(M // tm,),
        in_specs=[row(ATT_W), row(ATT_W), row(CONV_CH), row(D),
                  const(wo.shape), const(wg.shape), const(wu.shape), const(wd.shape),
                  const((1, D)), const((1, D)), const((1, D))],
        out_specs=row(D),
        out_shape=jax.ShapeDtypeStruct((M, D), F32),
        compiler_params=pltpu.CompilerParams(
            dimension_semantics=("arbitrary",), vmem_limit_bytes=VMEM_LIMIT),
        name="out_ffn",
    )(moba, fox, conv, x, wo, wg, wu, wd, gpm, gpf, gqf)


def _cumsum_constants(tm):
    tri = np.tril(np.ones((tm, tm), np.float32))
    perm = np.zeros((N_SPLIT, LANES, ATT_W), np.float32)
    for s in range(N_SPLIT):
        for h in range(N_HEADS):
            perm[s, h, PAIR_W * (h // 2) + N_SPLIT * (h % 2) + s] = 1.0
    return jnp.asarray(tri, BF16), jnp.asarray(perm, BF16)


def kernel(x, w_in, b_f, conv_w, w_out, rel_bias, g_pre_mix, g_post_mix, g_pre_ffn, g_post_ffn,
           w_gate, w_up, w_down):
    B, S, D = x.shape
    depth = w_in.shape[0]
    assert D == D_MODEL and S % CHUNK == 0 and S // BLK <= MAX_BLOCKS
    assert (B * S) % FFN_TILE == 0

    rel_bias = rel_bias.astype(F32)
    tables = _bias_tiles(rel_bias)
    w_in_t = jnp.swapaxes(w_in, 1, 2)
    tri, perm = _cumsum_constants(CHUNK)
    a = ATT_W
    for l in range(depth):
        wl = w_in_t[l]
        wqkv = wl[0:6 * a].astype(BF16)
        wtail = jnp.concatenate([wl[6 * a:6 * a + N_HEADS], jnp.zeros((LANES - N_HEADS, D), wl.dtype),
                                 wl[6 * a + N_HEADS:]], axis=0).astype(BF16)
        bfp = jnp.pad(b_f[l].astype(F32), (0, LANES - N_HEADS)).reshape(1, LANES)
        row = lambda g: g[l].astype(F32).reshape(1, D)

        mq, mk, fq, fkp, vt, sel, conv = _inproj(
            x, row(g_pre_mix), wqkv, wtail, bfp, conv_w[l].astype(F32), tri, perm)
        moba = _attention(mq, mk, vt, sel, tables, rel_bias)
        fox = _attention(fq, fkp, vt)
        x = _out_ffn(moba.reshape(B * S, a), fox.reshape(B * S, a), conv.reshape(B * S, CONV_CH),
                     x.reshape(B * S, D), w_out[l].astype(BF16), w_gate[l].astype(BF16),
                     w_up[l].astype(BF16), w_down[l].astype(BF16),
                     row(g_post_mix), row(g_pre_ffn), row(g_post_ffn)).reshape(B, S, D)
    return x
```

```python
import functools
import math

import jax
import jax.numpy as jnp
import numpy as np
from jax import lax
from jax.experimental import pallas as pl
from jax.experimental.pallas import tpu as pltpu

D_MODEL = 1024
HEAD_DIM = 64
N_HEADS = 6
ATT_W = N_HEADS * HEAD_DIM
PAIR_W = 2 * HEAD_DIM
N_PAIRS = N_HEADS // 2
CONV_CH = 256
CONV_WIDTH = 3
BLK = 256
BLK_SHIFT = BLK.bit_length() - 1
CHUNK = 512
BLOCKS_PER_CHUNK = CHUNK // BLK
MAX_BLOCKS = 16
TOPK = 3
REL_BUCKETS = 32
REL_MAX_DIST = 128
D_FF = 2816
RMS_EPS = 1e-6
NEG = -1e30
LANES = 128
N_SPLIT = 3
LOG2E = math.log2(math.e)
CORNER = REL_MAX_DIST
ONES_ROWS = 16
HEAD_ROWS = HEAD_DIM + ONES_ROWS
PAIR_ROWS = 2 * HEAD_ROWS

F32 = jnp.float32
BF16 = jnp.bfloat16
NT_DIMS = (((1,), (1,)), ((), ()))

FFN_TILE = 512
FFN_CHUNKS = (1536, 1280)
assert sum(FFN_CHUNKS) == D_FF
VMEM_LIMIT = 50 * 1024 * 1024


def _split_bf16(v, n):
    terms = []
    for _ in range(n):
        t = v.astype(BF16)
        terms.append(t)
        v = v - t.astype(F32)
    return terms


def _rms(v, g):
    return v * lax.rsqrt(jnp.mean(v * v, axis=-1, keepdims=True) + RMS_EPS) * g


def _bucket_tiles():
    max_exact = REL_BUCKETS // 2

    def bucket(dist):
        n = np.maximum(dist, 0)
        nf = np.maximum(n, 1).astype(np.float32)
        large = max_exact + (np.log(nf / np.float32(max_exact))
                             / np.float32(math.log(REL_MAX_DIST / max_exact))
                             * np.float32(REL_BUCKETS - max_exact)).astype(np.int32)
        large = np.minimum(large, REL_BUCKETS - 1)
        return np.where(n < max_exact, n, large).astype(np.int32)

    key = np.arange(CHUNK, dtype=np.int32)[:, None]
    qry = np.arange(CHUNK, dtype=np.int32)[None, :]
    d0 = qry - key
    own = np.where(d0 >= 0, bucket(d0), -1).astype(np.int32)
    prev = bucket(d0 + CHUNK)
    corner = prev[CHUNK - CORNER:, :CORNER]
    outside = prev.copy()
    outside[CHUNK - CORNER:, :CORNER] = REL_BUCKETS - 1
    assert (outside == REL_BUCKETS - 1).all()
    return own, np.ascontiguousarray(corner)


def _bias_tile_kernel(rb_ref, own_bm_ref, corner_bm_ref, own_ref, corner_ref):
    h = pl.program_id(0)
    for bm_ref, out_ref in ((own_bm_ref, own_ref), (corner_bm_ref, corner_ref)):
        bm = bm_ref[...]
        t = jnp.full(bm.shape, NEG, F32)
        for b in range(REL_BUCKETS):
            t = jnp.where(bm == b, rb_ref[b, h] * LOG2E, t)
        out_ref[0] = t


def _bias_tiles(rel_bias):
    own_bm, corner_bm = _bucket_tiles()
    return pl.pallas_call(
        _bias_tile_kernel,
        grid=(N_HEADS,),
        in_specs=[pl.BlockSpec(memory_space=pltpu.SMEM),
                  pl.BlockSpec((CHUNK, CHUNK), lambda h: (0, 0)),
                  pl.BlockSpec((CORNER, CORNER), lambda h: (0, 0))],
        out_specs=[pl.BlockSpec((1, CHUNK, CHUNK), lambda h: (h, 0, 0)),
                   pl.BlockSpec((1, CORNER, CORNER), lambda h: (h, 0, 0))],
        out_shape=[jax.ShapeDtypeStruct((N_HEADS, CHUNK, CHUNK), F32),
                   jax.ShapeDtypeStruct((N_HEADS, CORNER, CORNER), F32)],
        name="bias_tiles",
    )(rel_bias, jnp.asarray(own_bm), jnp.asarray(corner_bm))


def _inproj_kernel(x_ref, g_ref, wqkv_ref, wtail_ref, bf_ref, cw_ref, tri_ref, perm_ref,
                   mq_ref, mk_ref, fq_ref, fkp_ref, vt_ref, sel_ref, conv_ref,
                   km_ref, cum_ref, uc_ref):
    tm = CHUNK
    st = pl.program_id(1)

    @pl.when(st == 0)
    def _():
        km_ref[...] = jnp.zeros_like(km_ref)
        cum_ref[...] = jnp.zeros_like(cum_ref)
        uc_ref[...] = jnp.zeros_like(uc_ref)

    hn = _rms(x_ref[0], g_ref[...]).astype(BF16)

    qkv = lax.dot_general(hn, wqkv_ref[...], NT_DIMS, preferred_element_type=F32)
    tail = lax.dot_general(hn, wtail_ref[...], NT_DIMS, preferred_element_type=F32)
    mq = qkv[:, 0 * ATT_W:1 * ATT_W]
    mk = qkv[:, 1 * ATT_W:2 * ATT_W]
    mv = qkv[:, 2 * ATT_W:3 * ATT_W]
    fq = qkv[:, 3 * ATT_W:4 * ATT_W]
    fk = qkv[:, 4 * ATT_W:5 * ATT_W]
    fv = qkv[:, 5 * ATT_W:6 * ATT_W]
    scale = HEAD_DIM ** -0.5 * LOG2E
    mq_ref[0, 0] = (mq * scale).T.astype(BF16)
    mk_ref[0] = mk.astype(BF16)
    fq_ref[0, 0] = (fq * scale).T.astype(BF16)
    ones = jnp.ones((ONES_ROWS, tm), BF16)
    for kind, v in enumerate((mv, fv)):
        vt = v.T.astype(BF16)
        for h in range(N_HEADS):
            base = (kind * N_HEADS + h) * HEAD_ROWS
            vt_ref[0, 0, base:base + HEAD_DIM, :] = vt[h * HEAD_DIM:(h + 1) * HEAD_DIM]
            vt_ref[0, 0, base + HEAD_DIM:base + HEAD_ROWS, :] = ones

    rows = lax.broadcasted_iota(jnp.int32, km_ref.shape, 0)
    cols = lax.broadcasted_iota(jnp.int32, km_ref.shape, 1)
    same_head = (rows >> 4) == (cols >> 6)
    km = km_ref[...]
    for r in range(BLOCKS_PER_CHUNK):
        kmean = jnp.mean(mk[r * BLK:(r + 1) * BLK], axis=0, keepdims=True)
        n = st * BLOCKS_PER_CHUNK + r
        km = jnp.where(same_head & ((rows & (MAX_BLOCKS - 1)) == n), kmean, km)
    km_ref[...] = km

    km_hi, km_lo = _split_bf16(km, 2)
    q_hi, q_lo = _split_bf16(mq, 2)
    gate = (lax.dot_general(km_hi, q_hi, NT_DIMS, preferred_element_type=F32)
            + lax.dot_general(km_hi, q_lo, NT_DIMS, preferred_element_type=F32)
            + lax.dot_general(km_lo, q_hi, NT_DIMS, preferred_element_type=F32))

    nidx = lax.broadcasted_iota(jnp.int32, (MAX_BLOCKS, tm), 0)
    own = st * BLOCKS_PER_CHUNK + (lax.broadcasted_iota(jnp.int32, (MAX_BLOCKS, tm), 1) >> BLK_SHIFT)
    valid = nidx < own
    for h in range(N_HEADS):
        gh = jnp.where(valid, gate[h * MAX_BLOCKS:(h + 1) * MAX_BLOCKS], -jnp.inf)
        rank = jnp.zeros((MAX_BLOCKS, tm), jnp.int32)
        for m in range(MAX_BLOCKS):
            gm = gh[m:m + 1, :]
            gt = (gm > gh).astype(jnp.int32)
            ge = (gm >= gh).astype(jnp.int32)
            rank = rank + jnp.where(nidx > m, ge, gt)
        keep = ((rank < TOPK) & valid) | (nidx == own)
        sel_ref[0, h * MAX_BLOCKS:(h + 1) * MAX_BLOCKS, :] = jnp.where(keep, 0.0, NEG)

    fl = tail[:, 0:LANES] + bf_ref[...]
    lf = jnp.minimum(fl, 0.0) - jnp.log1p(jnp.exp(-jnp.abs(fl)))
    tri = tri_ref[...]
    cum = cum_ref[...]
    for t in _split_bf16(lf, N_SPLIT):
        cum = cum + jnp.dot(tri, t, preferred_element_type=F32)
    cum_ref[...] = cum[tm - 1:tm, :]
    kb = jnp.zeros((tm, ATT_W), F32)
    for s, t in enumerate(_split_bf16(cum * -LOG2E, N_SPLIT)):
        kb = kb + jnp.dot(t, perm_ref[s], preferred_element_type=F32)
    kb = kb.astype(BF16)
    fkb = fk.astype(BF16)
    for p in range(N_PAIRS):
        fkp_ref[0, :, 2 * p * PAIR_W:(2 * p + 1) * PAIR_W] = fkb[:, p * PAIR_W:(p + 1) * PAIR_W]
        fkp_ref[0, :, (2 * p + 1) * PAIR_W:(2 * p + 2) * PAIR_W] = kb[:, p * PAIR_W:(p + 1) * PAIR_W]

    cv = tail[:, LANES:]
    cvb = cv[:, 0:CONV_CH]
    u = cv[:, CONV_CH:2 * CONV_CH] * cv[:, 2 * CONV_CH:3 * CONV_CH]
    uc = uc_ref[...]
    prev1 = uc[7:8]
    prev2 = uc[6:7]
    ridx = lax.broadcasted_iota(jnp.int32, u.shape, 0)
    u1 = jnp.where(ridx == 0, prev1, pltpu.roll(u, 1, 0))
    u2 = jnp.where(ridx == 0, prev2, jnp.where(ridx == 1, prev1, pltpu.roll(u, 2, 0)))
    cw = cw_ref[...]
    y = cw[0:1] * u2 + cw[1:2] * u1 + cw[2:3] * u
    conv_ref[0] = (cvb * y).astype(BF16)
    uc_ref[...] = u[tm - 8:tm]


def _inproj(x, g, wqkv, wtail, bfp, cw, tri, perm):
    B, S, D = x.shape
    tm = CHUNK
    const = lambda shape: pl.BlockSpec(shape, lambda b, s: (0,) * len(shape))
    row = lambda w: pl.BlockSpec((1, tm, w), lambda b, s: (b, s, 0))
    colmajor = lambda w: pl.BlockSpec((1, 1, w, tm), lambda b, s: (b, s, 0, 0))
    return pl.pallas_call(
        _inproj_kernel,
        grid=(B, S // tm),
        in_specs=[row(D), const((1, D)), const(wqkv.shape), const(wtail.shape),
                  const(bfp.shape), const(cw.shape), const(tri.shape), const(perm.shape)],
        out_specs=[colmajor(ATT_W), row(ATT_W), colmajor(ATT_W), row(2 * ATT_W),
                   pl.BlockSpec((1, 1, 2 * N_HEADS * HEAD_ROWS, tm), lambda b, s: (b, s, 0, 0)),
                   pl.BlockSpec((1, N_HEADS * MAX_BLOCKS, tm), lambda b, s: (b, 0, s)),
                   row(CONV_CH)],
        out_shape=[jax.ShapeDtypeStruct((B, S // tm, ATT_W, tm), BF16),
                   jax.ShapeDtypeStruct((B, S, ATT_W), BF16),
                   jax.ShapeDtypeStruct((B, S // tm, ATT_W, tm), BF16),
                   jax.ShapeDtypeStruct((B, S, 2 * ATT_W), BF16),
                   jax.ShapeDtypeStruct((B, S // tm, 2 * N_HEADS * HEAD_ROWS, tm), BF16),
                   jax.ShapeDtypeStruct((B, N_HEADS * MAX_BLOCKS, S), F32),
                   jax.ShapeDtypeStruct((B, S, CONV_CH), BF16)],
        scratch_shapes=[pltpu.VMEM((N_HEADS * MAX_BLOCKS, ATT_W), F32),
                        pltpu.VMEM((1, LANES), F32),
                        pltpu.VMEM((8, CONV_CH), F32)],
        compiler_params=pltpu.CompilerParams(
            dimension_semantics=("arbitrary", "arbitrary"), vmem_limit_bytes=VMEM_LIMIT),
        name="inproj",
    )(x, g, wqkv, wtail, bfp, cw, tri, perm)


def _attn_kernel(*refs, moba, nchunk):
    if moba:
        rb_ref, qt_ref, k_ref, vt_ref, sel_ref, own_ref, corner_ref, o_ref, sa_ref, sb_ref = refs
    else:
        qt_ref, k_ref, vt_ref, o_ref, sa_ref, sb_ref = refs
    pair = pl.program_id(1)
    qt = qt_ref[0, 0]
    frow = lax.broadcasted_iota(jnp.int32, qt.shape, 0)
    zero = jnp.zeros_like(qt)
    ws = []
    for hh in range(2):
        w = jnp.where((frow >= hh * HEAD_DIM) & (frow < (hh + 1) * HEAD_DIM), qt, zero)
        if not moba:
            pick = (frow >= N_SPLIT * hh) & (frow < N_SPLIT * (hh + 1))
            w = jnp.concatenate([w, jnp.where(pick, 1.0, 0.0).astype(BF16)], axis=0)
        ws.append(w)
    if moba:
        far_bias = [rb_ref[REL_BUCKETS - 1, 2 * pair + hh] * LOG2E for hh in range(2)]

    def produce(c, dst_ref, own=False, near=False):
        kc = k_ref[0, c * CHUNK:(c + 1) * CHUNK, :]
        maxima = []
        for hh in range(2):
            s = jnp.dot(kc, ws[hh], preferred_element_type=F32)
            if moba:
                parts = []
                for r in range(BLOCKS_PER_CHUNK):
                    n = hh * MAX_BLOCKS + c * BLOCKS_PER_CHUNK + r
                    row = sel_ref[0, n:n + 1, :]
                    sh = s[r * BLK:(r + 1) * BLK]
                    if own:
                        parts.append(sh + own_ref[hh, r * BLK:(r + 1) * BLK, :] + row)
                    elif near and r == BLOCKS_PER_CHUNK - 1:
                        far_row = row + far_bias[hh]
                        top = sh[:BLK - CORNER] + far_row
                        bot = jnp.concatenate(
                            [sh[BLK - CORNER:, :CORNER] + corner_ref[hh] + row[:, :CORNER],
                             sh[BLK - CORNER:, CORNER:] + far_row[:, CORNER:]], axis=1)
                        parts += [top, bot]
                    else:
                        parts.append(sh + (row + far_bias[hh]))
                s = jnp.concatenate(parts, axis=0)
            elif own:
                kidx = lax.broadcasted_iota(jnp.int32, s.shape, 0)
                qidx = lax.broadcasted_iota(jnp.int32, s.shape, 1)
                s = jnp.where(kidx <= qidx, s, NEG)
            dst_ref[hh] = s
            maxima.append(jnp.max(s, axis=0, keepdims=True))
        return tuple(maxima)

    def consume(c, src_ref, maxima, state):
        new = []
        for hh in range(2):
            m_old, acc = state[hh]
            m_new = jnp.maximum(m_old, maxima[hh])
            p = jnp.exp2(src_ref[hh] - m_new).astype(BF16)
            vth = vt_ref[0, c, hh * HEAD_ROWS:(hh + 1) * HEAD_ROWS, :]
            pv = jnp.dot(vth, p, preferred_element_type=F32)
            new.append((m_new, jnp.exp2(m_old - m_new) * acc + pv))
        return tuple(new)

    def tile_program(qi):
        def run():
            bufs = (sa_ref, sb_ref)
            state = tuple((jnp.full((1, CHUNK), NEG, F32), jnp.zeros((HEAD_ROWS, CHUNK), F32))
                          for _ in range(2))
            maxima = produce(qi, bufs[0], own=True)
            for t in range(qi + 1):
                c = qi - t
                if c > 0:
                    nxt = produce(c - 1, bufs[(t + 1) % 2], near=(t == 0))
                state = consume(c, bufs[t % 2], maxima, state)
                if c > 0:
                    maxima = nxt
            ot = jnp.concatenate([acc[:HEAD_DIM] * (1.0 / acc[HEAD_DIM:HEAD_DIM + 1])
                                  for (_, acc) in state], axis=0)
            o_ref[0] = ot.T.astype(BF16)
        return run

    lax.switch(pl.program_id(2), [tile_program(i) for i in range(nchunk)])


def _attention(qt, k, vt, sel=None, tables=None, rel_bias=None):
    moba = sel is not None
    B, S, _ = k.shape
    nchunk = S // CHUNK
    kw = k.shape[-1] // N_PAIRS
    vt_off = 0 if moba else N_PAIRS
    in_specs = [pl.BlockSpec((1, 1, PAIR_W, CHUNK), lambda b, p, i: (b, i, p, 0)),
                pl.BlockSpec((1, S, kw), lambda b, p, i: (b, 0, p)),
                pl.BlockSpec((1, nchunk, PAIR_ROWS, CHUNK), lambda b, p, i: (b, 0, p + vt_off, 0))]
    args = [qt, k, vt]
    if moba:
        in_specs = [pl.BlockSpec(memory_space=pltpu.SMEM)] + in_specs
        in_specs += [pl.BlockSpec((1, 2 * MAX_BLOCKS, CHUNK), lambda b, p, i: (b, p, i)),
                     pl.BlockSpec((2, CHUNK, CHUNK), lambda b, p, i: (p, 0, 0)),
                     pl.BlockSpec((2, CORNER, CORNER), lambda b, p, i: (p, 0, 0))]
        args = [rel_bias] + args + [sel, *tables]
    return pl.pallas_call(
        functools.partial(_attn_kernel, moba=moba, nchunk=nchunk),
        grid=(B, N_PAIRS, nchunk),
        in_specs=in_specs,
        out_specs=pl.BlockSpec((1, CHUNK, PAIR_W), lambda b, p, i: (b, i, p)),
        out_shape=jax.ShapeDtypeStruct((B, S, ATT_W), BF16),
        scratch_shapes=[pltpu.VMEM((2, CHUNK, CHUNK), F32), pltpu.VMEM((2, CHUNK, CHUNK), F32)],
        compiler_params=pltpu.CompilerParams(
            dimension_semantics=("arbitrary", "arbitrary", "arbitrary"), vmem_limit_bytes=VMEM_LIMIT),
        name="moba_attn" if moba else "fox_attn",
    )(*args)


def _out_ffn_kernel(moba_ref, fox_ref, conv_ref, x_ref, wo_ref, wg_ref, wu_ref, wd_ref,
                    gpm_ref, gpf_ref, gqf_ref, o_ref):
    mix_in = jnp.concatenate([moba_ref[...], fox_ref[...], conv_ref[...]], axis=1)
    mixed = jnp.dot(mix_in, wo_ref[...], preferred_element_type=F32)
    x1 = x_ref[...] + _rms(mixed, gpm_ref[...])
    h2 = _rms(x1, gpf_ref[...]).astype(BF16)
    ff = jnp.zeros(x1.shape, F32)
    lo = 0
    for width in FFN_CHUNKS:
        gate = jnp.dot(h2, wg_ref[:, lo:lo + width], preferred_element_type=F32)
        up = jnp.dot(h2, wu_ref[:, lo:lo + width], preferred_element_type=F32)
        act = (gate * (1.0 / (1.0 + jnp.exp(-gate))) * up).astype(BF16)
        ff = ff + jnp.dot(act, wd_ref[lo:lo + width, :], preferred_element_type=F32)
        lo += width
    o_ref[...] = x1 + _rms(ff, gqf_ref[...])


def _out_ffn(moba, fox, conv, x, wo, wg, wu, wd, gpm, gpf, gqf):
    M, D = x.shape
    tm = FFN_TILE
    row = lambda w: pl.BlockSpec((tm, w), lambda m: (m, 0))
    const = lambda shape: pl.BlockSpec(shape, lambda m: (0, 0), pipeline_mode=pl.Buffered(1))
    return pl.pallas_call(
        _out_ffn_kernel,
        grid=(M // tm,),
        in_specs=[row(ATT_W), row(ATT_W), row(CONV_CH), row(D),
                  const(wo.shape), const(wg.shape), const(wu.shape), const(wd.shape),
                  const((1, D)), const((1, D)), const((1, D))],
        out_specs=row(D),
        out_shape=jax.ShapeDtypeStruct((M, D), F32),
        compiler_params=pltpu.CompilerParams(
            dimension_semantics=("arbitrary",), vmem_limit_bytes=VMEM_LIMIT),
        name="out_ffn",
    )(moba, fox, conv, x, wo, wg, wu, wd, gpm, gpf, gqf)


def _cumsum_constants(tm):
    tri = np.tril(np.ones((tm, tm), np.float32))
    perm = np.zeros((N_SPLIT, LANES, ATT_W), np.float32)
    for s in range(N_SPLIT):
        for h in range(N_HEADS):
            perm[s, h, PAIR_W * (h // 2) + N_SPLIT * (h % 2) + s] = 1.0
    return jnp.asarray(tri, BF16), jnp.asarray(perm, BF16)


def kernel(x, w_in, b_f, conv_w, w_out, rel_bias, g_pre_mix, g_post_mix, g_pre_ffn, g_post_ffn,
           w_gate, w_up, w_down):
    B, S, D = x.shape
    depth = w_in.shape[0]
    assert D == D_MODEL and S % CHUNK == 0 and S // BLK <= MAX_BLOCKS
    assert (B * S) % FFN_TILE == 0

    rel_bias = rel_bias.astype(F32)
    tables = _bias_tiles(rel_bias)
    w_in_t = jnp.swapaxes(w_in, 1, 2)
    tri, perm = _cumsum_constants(CHUNK)
    a = ATT_W
    for l in range(depth):
        wl = w_in_t[l]
        wqkv = wl[0:6 * a].astype(BF16)
        wtail = jnp.concatenate([wl[6 * a:6 * a + N_HEADS], jnp.zeros((LANES - N_HEADS, D), wl.dtype),
                                 wl[6 * a + N_HEADS:]], axis=0).astype(BF16)
        bfp = jnp.pad(b_f[l].astype(F32), (0, LANES - N_HEADS)).reshape(1, LANES)
        row = lambda g: g[l].astype(F32).reshape(1, D)

        mq, mk, fq, fkp, vt, sel, conv = _inproj(
            x, row(g_pre_mix), wqkv, wtail, bfp, conv_w[l].astype(F32), tri, perm)
        moba = _attention(mq, mk, vt, sel, tables, rel_bias)
        fox = _attention(fq, fkp, vt)
        x = _out_ffn(moba.reshape(B * S, a), fox.reshape(B * S, a), conv.reshape(B * S, CONV_CH),
                     x.reshape(B * S, D), w_out[l].astype(BF16), w_gate[l].astype(BF16),
                     w_up[l].astype(BF16), w_down[l].astype(BF16),
                     row(g_post_mix), row(g_pre_ffn), row(g_post_ffn)).reshape(B, S, D)
    return x
```

```python
import functools
import math

import jax
import jax.numpy as jnp
import numpy as np
from jax import lax
from jax.experimental import pallas as pl
from jax.experimental.pallas import tpu as pltpu

D_MODEL = 1024
HEAD_DIM = 64
N_HEADS = 6
ATT_W = N_HEADS * HEAD_DIM
PAIR_W = 2 * HEAD_DIM
N_PAIRS = N_HEADS // 2
CONV_CH = 256
CONV_WIDTH = 3
BLK = 256
BLK_SHIFT = BLK.bit_length() - 1
CHUNK = 512
BLOCKS_PER_CHUNK = CHUNK // BLK
MAX_BLOCKS = 16
TOPK = 3
REL_BUCKETS = 32
REL_MAX_DIST = 128
D_FF = 2816
RMS_EPS = 1e-6
NEG = -1e30
LANES = 128
N_SPLIT = 3
LOG2E = math.log2(math.e)
CORNER = REL_MAX_DIST
ONES_ROWS = 16
HEAD_ROWS = HEAD_DIM + ONES_ROWS
PAIR_ROWS = 2 * HEAD_ROWS

F32 = jnp.float32
BF16 = jnp.bfloat16
NT_DIMS = (((1,), (1,)), ((), ()))

FFN_TILE = 512
FFN_CHUNKS = (1536, 1280)
assert sum(FFN_CHUNKS) == D_FF
VMEM_LIMIT = 50 * 1024 * 1024


def _split_bf16(v, n):
    terms = []
    for _ in range(n):
        t = v.astype(BF16)
        terms.append(t)
        v = v - t.astype(F32)
    return terms


def _rms(v, g):
    return v * lax.rsqrt(jnp.mean(v * v, axis=-1, keepdims=True) + RMS_EPS) * g


def _bucket_tiles():
    max_exact = REL_BUCKETS // 2

    def bucket(dist):
        n = np.maximum(dist, 0)
        nf = np.maximum(n, 1).astype(np.float32)
        large = max_exact + (np.log(nf / np.float32(max_exact))
                             / np.float32(math.log(REL_MAX_DIST / max_exact))
                             * np.float32(REL_BUCKETS - max_exact)).astype(np.int32)
        large = np.minimum(large, REL_BUCKETS - 1)
        return np.where(n < max_exact, n, large).astype(np.int32)

    key = np.arange(CHUNK, dtype=np.int32)[:, None]
    qry = np.arange(CHUNK, dtype=np.int32)[None, :]
    d0 = qry - key
    own = np.where(d0 >= 0, bucket(d0), -1).astype(np.int32)
    prev = bucket(d0 + CHUNK)
    corner = prev[CHUNK - CORNER:, :CORNER]
    outside = prev.copy()
    outside[CHUNK - CORNER:, :CORNER] = REL_BUCKETS - 1
    assert (outside == REL_BUCKETS - 1).all()
    return own, np.ascontiguousarray(corner)


def _bias_tile_kernel(rb_ref, own_bm_ref, corner_bm_ref, own_ref, corner_ref):
    h = pl.program_id(0)
    for bm_ref, out_ref in ((own_bm_ref, own_ref), (corner_bm_ref, corner_ref)):
        bm = bm_ref[...]
        t = jnp.full(bm.shape, NEG, F32)
        for b in range(REL_BUCKETS):
            t = jnp.where(bm == b, rb_ref[b, h] * LOG2E, t)
        out_ref[0] = t


def _bias_tiles(rel_bias):
    own_bm, corner_bm = _bucket_tiles()
    return pl.pallas_call(
        _bias_tile_kernel,
        grid=(N_HEADS,),
        in_specs=[pl.BlockSpec(memory_space=pltpu.SMEM),
                  pl.BlockSpec((CHUNK, CHUNK), lambda h: (0, 0)),
                  pl.BlockSpec((CORNER, CORNER), lambda h: (0, 0))],
        out_specs=[pl.BlockSpec((1, CHUNK, CHUNK), lambda h: (h, 0, 0)),
                   pl.BlockSpec((1, CORNER, CORNER), lambda h: (h, 0, 0))],
        out_shape=[jax.ShapeDtypeStruct((N_HEADS, CHUNK, CHUNK), F32),
                   jax.ShapeDtypeStruct((N_HEADS, CORNER, CORNER), F32)],
        name="bias_tiles",
    )(rel_bias, jnp.asarray(own_bm), jnp.asarray(corner_bm))


def _inproj_kernel(x_ref, g_ref, wqkv_ref, wtail_ref, bf_ref, cw_ref, tri_ref, perm_ref,
                   mq_ref, mk_ref, fq_ref, fkp_ref, vt_ref, sel_ref, conv_ref,
                   km_ref, cum_ref, uc_ref):
    tm = CHUNK
    st = pl.program_id(1)

    @pl.when(st == 0)
    def _():
        km_ref[...] = jnp.zeros_like(km_ref)
        cum_ref[...] = jnp.zeros_like(cum_ref)
        uc_ref[...] = jnp.zeros_like(uc_ref)

    hn = _rms(x_ref[0], g_ref[...]).astype(BF16)

    qkv = lax.dot_general(hn, wqkv_ref[...], NT_DIMS, preferred_element_type=F32)
    tail = lax.dot_general(hn, wtail_ref[...], NT_DIMS, preferred_element_type=F32)
    mq = qkv[:, 0 * ATT_W:1 * ATT_W]
    mk = qkv[:, 1 * ATT_W:2 * ATT_W]
    mv = qkv[:, 2 * ATT_W:3 * ATT_W]
    fq = qkv[:, 3 * ATT_W:4 * ATT_W]
    fk = qkv[:, 4 * ATT_W:5 * ATT_W]
    fv = qkv[:, 5 * ATT_W:6 * ATT_W]
    scale = HEAD_DIM ** -0.5 * LOG2E
    mq_ref[0, 0] = (mq * scale).T.astype(BF16)
    mk_ref[0] = mk.astype(BF16)
    fq_ref[0, 0] = (fq * scale).T.astype(BF16)
    ones = jnp.ones((ONES_ROWS, tm), BF16)
    for kind, v in enumerate((mv, fv)):
        vt = v.T.astype(BF16)
        for h in range(N_HEADS):
            base = (kind * N_HEADS + h) * HEAD_ROWS
            vt_ref[0, 0, base:base + HEAD_DIM, :] = vt[h * HEAD_DIM:(h + 1) * HEAD_DIM]
            vt_ref[0, 0, base + HEAD_DIM:base + HEAD_ROWS, :] = ones

    rows = lax.broadcasted_iota(jnp.int32, km_ref.shape, 0)
    cols = lax.broadcasted_iota(jnp.int32, km_ref.shape, 1)
    same_head = (rows >> 4) == (cols >> 6)
    km = km_ref[...]
    for r in range(BLOCKS_PER_CHUNK):
        kmean = jnp.mean(mk[r * BLK:(r + 1) * BLK], axis=0, keepdims=True)
        n = st * BLOCKS_PER_CHUNK + r
        km = jnp.where(same_head & ((rows & (MAX_BLOCKS - 1)) == n), kmean, km)
    km_ref[...] = km

    km_hi, km_lo = _split_bf16(km, 2)
    q_hi, q_lo = _split_bf16(mq, 2)
    gate = (lax.dot_general(km_hi, q_hi, NT_DIMS, preferred_element_type=F32)
            + lax.dot_general(km_hi, q_lo, NT_DIMS, preferred_element_type=F32)
            + lax.dot_general(km_lo, q_hi, NT_DIMS, preferred_element_type=F32))

    nidx = lax.broadcasted_iota(jnp.int32, (MAX_BLOCKS, tm), 0)
    own = st * BLOCKS_PER_CHUNK + (lax.broadcasted_iota(jnp.int32, (MAX_BLOCKS, tm), 1) >> BLK_SHIFT)
    valid = nidx < own
    for h in range(N_HEADS):
        gh = jnp.where(valid, gate[h * MAX_BLOCKS:(h + 1) * MAX_BLOCKS], -jnp.inf)
        rank = jnp.zeros((MAX_BLOCKS, tm), jnp.int32)
        for m in range(MAX_BLOCKS):
            gm = gh[m:m + 1, :]
            gt = (gm > gh).astype(jnp.int32)
            ge = (gm >= gh).astype(jnp.int32)
            rank = rank + jnp.where(nidx > m, ge, gt)
        keep = ((rank < TOPK) & valid) | (nidx == own)
        sel_ref[0, h * MAX_BLOCKS:(h + 1) * MAX_BLOCKS, :] = jnp.where(keep, 0.0, NEG)

    fl = tail[:, 0:LANES] + bf_ref[...]
    lf = jnp.minimum(fl, 0.0) - jnp.log1p(jnp.exp(-jnp.abs(fl)))
    tri = tri_ref[...]
    cum = cum_ref[...]
    for t in _split_bf16(lf, N_SPLIT):
        cum = cum + jnp.dot(tri, t, preferred_element_type=F32)
    cum_ref[...] = cum[tm - 1:tm, :]
    kb = jnp.zeros((tm, ATT_W), F32)
    for s, t in enumerate(_split_bf16(cum * -LOG2E, N_SPLIT)):
        kb = kb + jnp.dot(t, perm_ref[s], preferred_element_type=F32)
    kb = kb.astype(BF16)
    fkb = fk.astype(BF16)
    for p in range(N_PAIRS):
        fkp_ref[0, :, 2 * p * PAIR_W:(2 * p + 1) * PAIR_W] = fkb[:, p * PAIR_W:(p + 1) * PAIR_W]
        fkp_ref[0, :, (2 * p + 1) * PAIR_W:(2 * p + 2) * PAIR_W] = kb[:, p * PAIR_W:(p + 1) * PAIR_W]

    cv = tail[:, LANES:]
    cvb = cv[:, 0:CONV_CH]
    u = cv[:, CONV_CH:2 * CONV_CH] * cv[:, 2 * CONV_CH:3 * CONV_CH]
    uc = uc_ref[...]
    prev1 = uc[7:8]
    prev2 = uc[6:7]
    ridx = lax.broadcasted_iota(jnp.int32, u.shape, 0)
    u1 = jnp.where(ridx == 0, prev1, pltpu.roll(u, 1, 0))
    u2 = jnp.where(ridx == 0, prev2, jnp.where(ridx == 1, prev1, pltpu.roll(u, 2, 0)))
    cw = cw_ref[...]
    y = cw[0:1] * u2 + cw[1:2] * u1 + cw[2:3] * u
    conv_ref[0] = (cvb * y).astype(BF16)
    uc_ref[...] = u[tm - 8:tm]


def _inproj(x, g, wqkv, wtail, bfp, cw, tri, perm):
    B, S, D = x.shape
    tm = CHUNK
    const = lambda shape: pl.BlockSpec(shape, lambda b, s: (0,) * len(shape))
    row = lambda w: pl.BlockSpec((1, tm, w), lambda b, s: (b, s, 0))
    colmajor = lambda w: pl.BlockSpec((1, 1, w, tm), lambda b, s: (b, s, 0, 0))
    return pl.pallas_call(
        _inproj_kernel,
        grid=(B, S // tm),
        in_specs=[row(D), const((1, D)), const(wqkv.shape), const(wtail.shape),
                  const(bfp.shape), const(cw.shape), const(tri.shape), const(perm.shape)],
        out_specs=[colmajor(ATT_W), row(ATT_W), colmajor(ATT_W), row(2 * ATT_W),
                   pl.BlockSpec((1, 1, 2 * N_HEADS * HEAD_ROWS, tm), lambda b, s: (b, s, 0, 0)),
                   pl.BlockSpec((1, N_HEADS * MAX_BLOCKS, tm), lambda b, s: (b, 0, s)),
                   row(CONV_CH)],
        out_shape=[jax.ShapeDtypeStruct((B, S // tm, ATT_W, tm), BF16),
                   jax.ShapeDtypeStruct((B, S, ATT_W), BF16),
                   jax.ShapeDtypeStruct((B, S // tm, ATT_W, tm), BF16),
                   jax.ShapeDtypeStruct((B, S, 2 * ATT_W), BF16),
                   jax.ShapeDtypeStruct((B, S // tm, 2 * N_HEADS * HEAD_ROWS, tm), BF16),
                   jax.ShapeDtypeStruct((B, N_HEADS * MAX_BLOCKS, S), F32),
                   jax.ShapeDtypeStruct((B, S, CONV_CH), BF16)],
        scratch_shapes=[pltpu.VMEM((N_HEADS * MAX_BLOCKS, ATT_W), F32),
                        pltpu.VMEM((1, LANES), F32),
                        pltpu.VMEM((8, CONV_CH), F32)],
        compiler_params=pltpu.CompilerParams(
            dimension_semantics=("arbitrary", "arbitrary"), vmem_limit_bytes=VMEM_LIMIT),
        name="inproj",
    )(x, g, wqkv, wtail, bfp, cw, tri, perm)


def _attn_kernel(*refs, moba, nchunk):
    if moba:
        rb_ref, qt_ref, k_ref, vt_ref, sel_ref, own_ref, corner_ref, o_ref, sa_ref, sb_ref = refs
        pair = pl.program_id(1)
        far_bias = [rb_ref[REL_BUCKETS - 1, 2 * pair + hh] * LOG2E for hh in range(2)]
    else:
        qt_ref, k_ref, vt_ref, o_ref, sa_ref, sb_ref = refs
    bufs = (sa_ref, sb_ref)
    hidx = [jnp.minimum(pl.program_id(0), 0) + hh for hh in range(2)]

    def weights(qi):
        qt = qt_ref[0, qi]
        frow = lax.broadcasted_iota(jnp.int32, qt.shape, 0)
        zero = jnp.zeros_like(qt)
        ws = []
        for hh in range(2):
            w = jnp.where((frow >= hh * HEAD_DIM) & (frow < (hh + 1) * HEAD_DIM), qt, zero)
            if not moba:
                pick = (frow >= N_SPLIT * hh) & (frow < N_SPLIT * (hh + 1))
                w = jnp.concatenate([w, jnp.where(pick, 1.0, 0.0).astype(BF16)], axis=0)
            ws.append(w)
        return ws

    def produce(qi, ws, c, dst_ref):
        own, near = c == qi, c == qi - 1
        kc = k_ref[0, c * CHUNK:(c + 1) * CHUNK, :]
        maxima = []
        for hh in range(2):
            s = jnp.dot(kc, ws[hh], preferred_element_type=F32)
            if moba:
                parts = []
                for r in range(BLOCKS_PER_CHUNK):
                    n = hh * MAX_BLOCKS + c * BLOCKS_PER_CHUNK + r
                    row = sel_ref[0, n:n + 1, qi * CHUNK:(qi + 1) * CHUNK]
                    sh = s[r * BLK:(r + 1) * BLK]
                    if own:
                        parts.append(sh + own_ref[hh, r * BLK:(r + 1) * BLK, :] + row)
                    elif near and r == BLOCKS_PER_CHUNK - 1:
                        far_row = row + far_bias[hh]
                        top = sh[:BLK - CORNER] + far_row
                        bot = jnp.concatenate(
                            [sh[BLK - CORNER:, :CORNER] + corner_ref[hh] + row[:, :CORNER],
                             sh[BLK - CORNER:, CORNER:] + far_row[:, CORNER:]], axis=1)
                        parts += [top, bot]
                    else:
                        parts.append(sh + (row + far_bias[hh]))
                s = jnp.concatenate(parts, axis=0)
            elif own:
                kidx = lax.broadcasted_iota(jnp.int32, s.shape, 0)
                qidx = lax.broadcasted_iota(jnp.int32, s.shape, 1)
                s = jnp.where(kidx <= qidx, s, NEG)
            dst_ref[hidx[hh]] = s
            maxima.append(jnp.max(s, axis=0, keepdims=True))
        return tuple(maxima)

    def consume(c, src_ref, maxima, state):
        new = []
        for hh in range(2):
            m_old, acc = state[hh]
            m_new = jnp.maximum(m_old, maxima[hh])
            p = jnp.exp2(src_ref[hidx[hh]] - m_new).astype(BF16)
            vth = vt_ref[0, c, hh * HEAD_ROWS:(hh + 1) * HEAD_ROWS, :]
            pv = jnp.dot(vth, p, preferred_element_type=F32)
            new.append((m_new, jnp.exp2(m_old - m_new) * acc + pv))
        return tuple(new)

    g = 0
    ws = weights(0)
    maxima = produce(0, ws, 0, bufs[0])
    for qi in range(nchunk):
        state = tuple((jnp.full((1, CHUNK), NEG, F32), jnp.zeros((HEAD_ROWS, CHUNK), F32))
                      for _ in range(2))
        ws_next = weights(qi + 1) if qi + 1 < nchunk else None
        for c in range(qi, -1, -1):
            nxt = None
            if c > 0:
                nxt = produce(qi, ws, c - 1, bufs[(g + 1) % 2])
            elif ws_next is not None:
                nxt = produce(qi + 1, ws_next, qi + 1, bufs[(g + 1) % 2])
            state = consume(c, bufs[g % 2], maxima, state)
            maxima = nxt
            g += 1
        ws = ws_next
        ot = jnp.concatenate([acc[:HEAD_DIM] * (1.0 / acc[HEAD_DIM:HEAD_DIM + 1])
                              for (_, acc) in state], axis=0)
        o_ref[0, qi * CHUNK:(qi + 1) * CHUNK, :] = ot.T.astype(BF16)


def _attention(qt, k, vt, sel=None, tables=None, rel_bias=None):
    moba = sel is not None
    B, S, _ = k.shape
    nchunk = S // CHUNK
    kw = k.shape[-1] // N_PAIRS
    vt_off = 0 if moba else N_PAIRS
    in_specs = [pl.BlockSpec((1, nchunk, PAIR_W, CHUNK), lambda b, p: (b, 0, p, 0)),
                pl.BlockSpec((1, S, kw), lambda b, p: (b, 0, p)),
                pl.BlockSpec((1, nchunk, PAIR_ROWS, CHUNK), lambda b, p: (b, 0, p + vt_off, 0))]
    args = [qt, k, vt]
    if moba:
        in_specs = [pl.BlockSpec(memory_space=pltpu.SMEM)] + in_specs
        in_specs += [pl.BlockSpec((1, 2 * MAX_BLOCKS, S), lambda b, p: (b, p, 0)),
                     pl.BlockSpec((2, CHUNK, CHUNK), lambda b, p: (p, 0, 0)),
                     pl.BlockSpec((2, CORNER, CORNER), lambda b, p: (p, 0, 0))]
        args = [rel_bias] + args + [sel, *tables]
    return pl.pallas_call(
        functools.partial(_attn_kernel, moba=moba, nchunk=nchunk),
        grid=(B, N_PAIRS),
        in_specs=in_specs,
        out_specs=pl.BlockSpec((1, S, PAIR_W), lambda b, p: (b, 0, p)),
        out_shape=jax.ShapeDtypeStruct((B, S, ATT_W), BF16),
        scratch_shapes=[pltpu.VMEM((2, CHUNK, CHUNK), F32), pltpu.VMEM((2, CHUNK, CHUNK), F32)],
        compiler_params=pltpu.CompilerParams(
            dimension_semantics=("arbitrary", "arbitrary"), vmem_limit_bytes=VMEM_LIMIT),
        name="moba_attn" if moba else "fox_attn",
    )(*args)


def _out_ffn_kernel(moba_ref, fox_ref, conv_ref, x_ref, wo_ref, wg_ref, wu_ref, wd_ref,
                    gpm_ref, gpf_ref, gqf_ref, o_ref):
    mix_in = jnp.concatenate([moba_ref[...], fox_ref[...], conv_ref[...]], axis=1)
    mixed = jnp.dot(mix_in, wo_ref[...], preferred_element_type=F32)
    x1 = x_ref[...] + _rms(mixed, gpm_ref[...])
    h2 = _rms(x1, gpf_ref[...]).astype(BF16)
    ff = jnp.zeros(x1.shape, F32)
    lo = 0
    for width in FFN_CHUNKS:
        gate = jnp.dot(h2, wg_ref[:, lo:lo + width], preferred_element_type=F32)
        up = jnp.dot(h2, wu_ref[:, lo:lo + width], preferred_element_type=F32)
        act = (gate * (1.0 / (1.0 + jnp.exp(-gate))) * up).astype(BF16)
        ff = ff + jnp.dot(act, wd_ref[lo:lo + width, :], preferred_element_type=F32)
        lo += width
    o_ref[...] = x1 + _rms(ff, gqf_ref[...])


def _out_ffn(moba, fox, conv, x, wo, wg, wu, wd, gpm, gpf, gqf):
    M, D = x.shape
    tm = FFN_TILE
    row = lambda w: pl.BlockSpec((tm, w), lambda m: (m, 0))
    const = lambda shape: pl.BlockSpec(shape, lambda m: (0, 0), pipeline_mode=pl.Buffered(1))
    return pl.pallas_call(
        _out_ffn_kernel,
        grid=(M // tm,),
        in_specs=[row(ATT_W), row(ATT_W), row(CONV_CH), row(D),
                  const(wo.shape), const(wg.shape), const(wu.shape), const(wd.shape),
                  const((1, D)), const((1, D)), const((1, D))],
        out_specs=row(D),
        out_shape=jax.ShapeDtypeStruct((M, D), F32),
        compiler_params=pltpu.CompilerParams(
            dimension_semantics=("arbitrary",), vmem_limit_bytes=VMEM_LIMIT),
        name="out_ffn",
    )(moba, fox, conv, x, wo, wg, wu, wd, gpm, gpf, gqf)


def _cumsum_constants(tm):
    tri = np.tril(np.ones((tm, tm), np.float32))
    perm = np.zeros((N_SPLIT, LANES, ATT_W), np.float32)
    for s in range(N_SPLIT):
        for h in range(N_HEADS):
            perm[s, h, PAIR_W * (h // 2) + N_SPLIT * (h % 2) + s] = 1.0
    return jnp.asarray(tri, BF16), jnp.asarray(perm, BF16)


def kernel(x, w_in, b_f, conv_w, w_out, rel_bias, g_pre_mix, g_post_mix, g_pre_ffn, g_post_ffn,
           w_gate, w_up, w_down):
    B, S, D = x.shape
    depth = w_in.shape[0]
    assert D == D_MODEL and S % CHUNK == 0 and S // BLK <= MAX_BLOCKS
    assert (B * S) % FFN_TILE == 0

    rel_bias = rel_bias.astype(F32)
    tables = _bias_tiles(rel_bias)
    w_in_t = jnp.swapaxes(w_in, 1, 2)
    tri, perm = _cumsum_constants(CHUNK)
    a = ATT_W
    for l in range(depth):
        wl = w_in_t[l]
        wqkv = wl[0:6 * a].astype(BF16)
        wtail = jnp.concatenate([wl[6 * a:6 * a + N_HEADS], jnp.zeros((LANES - N_HEADS, D), wl.dtype),
                                 wl[6 * a + N_HEADS:]], axis=0).astype(BF16)
        bfp = jnp.pad(b_f[l].astype(F32), (0, LANES - N_HEADS)).reshape(1, LANES)
        row = lambda g: g[l].astype(F32).reshape(1, D)

        mq, mk, fq, fkp, vt, sel, conv = _inproj(
            x, row(g_pre_mix), wqkv, wtail, bfp, conv_w[l].astype(F32), tri, perm)
        moba = _attention(mq, mk, vt, sel, tables, rel_bias)
        fox = _attention(fq, fkp, vt)
        x = _out_ffn(moba.reshape(B * S, a), fox.reshape(B * S, a), conv.reshape(B * S, CONV_CH),
                     x.reshape(B * S, D), w_out[l].astype(BF16), w_gate[l].astype(BF16),
                     w_up[l].astype(BF16), w_down[l].astype(BF16),
                     row(g_post_mix), row(g_pre_ffn), row(g_post_ffn)).reshape(B, S, D)
    return x
```

```python
import functools
import math

import jax
import jax.numpy as jnp
import numpy as np
from jax import lax
from jax.experimental import pallas as pl
from jax.experimental.pallas import tpu as pltpu

D_MODEL = 1024
HEAD_DIM = 64
N_HEADS = 6
ATT_W = N_HEADS * HEAD_DIM
PAIR_W = 2 * HEAD_DIM
N_PAIRS = N_HEADS // 2
CONV_CH = 256
CONV_WIDTH = 3
BLK = 256
BLK_SHIFT = BLK.bit_length() - 1
CHUNK = 512
BLOCKS_PER_CHUNK = CHUNK // BLK
MAX_BLOCKS = 16
TOPK = 3
REL_BUCKETS = 32
REL_MAX_DIST = 128
D_FF = 2816
RMS_EPS = 1e-6
NEG = -1e30
LANES = 128
N_SPLIT = 3
LOG2E = math.log2(math.e)
CORNER = REL_MAX_DIST
ONES_ROWS = 16
HEAD_ROWS = HEAD_DIM + ONES_ROWS
PAIR_ROWS = 2 * HEAD_ROWS
XROWS = 16

F32 = jnp.float32
BF16 = jnp.bfloat16
NT_DIMS = (((1,), (1,)), ((), ()))

FFN_TILE = 512
FFN_ROWS = 256
FFN_CHUNKS = (1536, 1280)
assert sum(FFN_CHUNKS) == D_FF
VMEM_LIMIT = 50 * 1024 * 1024


def _split_bf16(v, n):
    terms = []
    for _ in range(n):
        t = v.astype(BF16)
        terms.append(t)
        v = v - t.astype(F32)
    return terms


def _rms(v, g):
    return v * lax.rsqrt(jnp.mean(v * v, axis=-1, keepdims=True) + RMS_EPS) * g


def _bucket_tiles():
    max_exact = REL_BUCKETS // 2

    def bucket(dist):
        n = np.maximum(dist, 0)
        nf = np.maximum(n, 1).astype(np.float32)
        large = max_exact + (np.log(nf / np.float32(max_exact))
                             / np.float32(math.log(REL_MAX_DIST / max_exact))
                             * np.float32(REL_BUCKETS - max_exact)).astype(np.int32)
        large = np.minimum(large, REL_BUCKETS - 1)
        return np.where(n < max_exact, n, large).astype(np.int32)

    key = np.arange(CHUNK, dtype=np.int32)[:, None]
    qry = np.arange(CHUNK, dtype=np.int32)[None, :]
    d0 = qry - key
    own = np.where(d0 >= 0, bucket(d0), -1).astype(np.int32)
    prev = bucket(d0 + CHUNK)
    corner = prev[CHUNK - CORNER:, :CORNER]
    outside = prev.copy()
    outside[CHUNK - CORNER:, :CORNER] = REL_BUCKETS - 1
    assert (outside == REL_BUCKETS - 1).all()
    return own, np.ascontiguousarray(corner)


def _bias_tile_kernel(rb_ref, own_bm_ref, corner_bm_ref, own_ref, corner_ref):
    h = pl.program_id(0)
    for bm_ref, out_ref in ((own_bm_ref, own_ref), (corner_bm_ref, corner_ref)):
        bm = bm_ref[...]
        t = jnp.full(bm.shape, NEG, F32)
        for b in range(REL_BUCKETS):
            t = jnp.where(bm == b, rb_ref[b, h] * LOG2E, t)
        out_ref[0] = t


def _bias_tiles(rel_bias):
    own_bm, corner_bm = _bucket_tiles()
    return pl.pallas_call(
        _bias_tile_kernel,
        grid=(N_HEADS,),
        in_specs=[pl.BlockSpec(memory_space=pltpu.SMEM),
                  pl.BlockSpec((CHUNK, CHUNK), lambda h: (0, 0)),
                  pl.BlockSpec((CORNER, CORNER), lambda h: (0, 0))],
        out_specs=[pl.BlockSpec((1, CHUNK, CHUNK), lambda h: (h, 0, 0)),
                   pl.BlockSpec((1, CORNER, CORNER), lambda h: (h, 0, 0))],
        out_shape=[jax.ShapeDtypeStruct((N_HEADS, CHUNK, CHUNK), F32),
                   jax.ShapeDtypeStruct((N_HEADS, CORNER, CORNER), F32)],
        name="bias_tiles",
    )(rel_bias, jnp.asarray(own_bm), jnp.asarray(corner_bm))


def _inproj_kernel(x_ref, g_ref, wqkv_ref, wtail_ref, bf_ref, cw_ref, tri_ref, perm_ref,
                   mq_ref, mk_ref, fq_ref, fkp_ref, vt_ref, sel_ref, conv_ref,
                   km_ref, cum_ref, uc_ref):
    tm = CHUNK
    st = pl.program_id(1)

    @pl.when(st == 0)
    def _():
        km_ref[...] = jnp.zeros_like(km_ref)
        cum_ref[...] = jnp.zeros_like(cum_ref)
        uc_ref[...] = jnp.zeros_like(uc_ref)

    hn = _rms(x_ref[0], g_ref[...]).astype(BF16)

    nqk = 2 * ATT_W
    tail = lax.dot_general(hn, wtail_ref[...], NT_DIMS, preferred_element_type=F32)
    qk = lax.dot_general(hn, wqkv_ref[0:nqk, :], NT_DIMS, preferred_element_type=F32)
    mq = qk[:, 0:ATT_W]
    mk = qk[:, ATT_W:nqk]
    scale = HEAD_DIM ** -0.5 * LOG2E

    rows = lax.broadcasted_iota(jnp.int32, km_ref.shape, 0)
    cols = lax.broadcasted_iota(jnp.int32, km_ref.shape, 1)
    same_head = (rows >> 4) == (cols >> 6)
    km = km_ref[...]
    for r in range(BLOCKS_PER_CHUNK):
        kmean = jnp.mean(mk[r * BLK:(r + 1) * BLK], axis=0, keepdims=True)
        n = st * BLOCKS_PER_CHUNK + r
        km = jnp.where(same_head & ((rows & (MAX_BLOCKS - 1)) == n), kmean, km)
    km_ref[...] = km

    km_hi, km_lo = _split_bf16(km, 2)
    q_hi, q_lo = _split_bf16(mq, 2)
    gate = (lax.dot_general(km_hi, q_hi, NT_DIMS, preferred_element_type=F32)
            + lax.dot_general(km_hi, q_lo, NT_DIMS, preferred_element_type=F32)
            + lax.dot_general(km_lo, q_hi, NT_DIMS, preferred_element_type=F32))

    nidx = lax.broadcasted_iota(jnp.int32, (MAX_BLOCKS, tm), 0)
    own = st * BLOCKS_PER_CHUNK + (lax.broadcasted_iota(jnp.int32, (MAX_BLOCKS, tm), 1) >> BLK_SHIFT)
    valid = nidx < own
    for h in range(N_HEADS):
        gh = jnp.where(valid, gate[h * MAX_BLOCKS:(h + 1) * MAX_BLOCKS], -jnp.inf)
        rank = jnp.zeros((MAX_BLOCKS, tm), jnp.int32)
        for m in range(MAX_BLOCKS):
            gm = gh[m:m + 1, :]
            gt = (gm > gh).astype(jnp.int32)
            ge = (gm >= gh).astype(jnp.int32)
            rank = rank + jnp.where(nidx > m, ge, gt)
        keep = ((rank < TOPK) & valid) | (nidx == own)
        sel_ref[0, h * MAX_BLOCKS:(h + 1) * MAX_BLOCKS, :] = jnp.where(keep, 0.0, NEG)

    fl = tail[:, 0:LANES] + bf_ref[...]
    lf = jnp.minimum(fl, 0.0) - jnp.log1p(jnp.exp(-jnp.abs(fl)))
    lf_terms = jnp.concatenate(_split_bf16(lf, N_SPLIT), axis=1)
    tri = tri_ref[...]
    carry = cum_ref[...]
    groups = []
    for r in range(BLOCKS_PER_CHUNK):
        part = jnp.dot(tri, lf_terms[r * BLK:(r + 1) * BLK], preferred_element_type=F32)
        grp = carry + sum(part[:, t * LANES:(t + 1) * LANES] for t in range(N_SPLIT))
        carry = grp[BLK - 1:BLK, :]
        groups.append(grp)
    cum = jnp.concatenate(groups, axis=0)
    cum_ref[...] = carry
    neg_terms = jnp.concatenate(_split_bf16(cum * -LOG2E, N_SPLIT), axis=1)
    kb = jnp.dot(neg_terms, perm_ref[...], preferred_element_type=F32).astype(BF16)

    rest = lax.dot_general(hn, wqkv_ref[nqk:, :], NT_DIMS, preferred_element_type=F32)
    mv = rest[:, 0 * ATT_W:1 * ATT_W]
    fq = rest[:, 1 * ATT_W:2 * ATT_W]
    fk = rest[:, 2 * ATT_W:3 * ATT_W]
    fv = rest[:, 3 * ATT_W:4 * ATT_W]
    mq_ref[0, 0] = (mq * scale).T.astype(BF16)
    mk_ref[0] = mk.astype(BF16)
    fq_ref[0, 0] = (fq * scale).T.astype(BF16)
    ones = jnp.ones((ONES_ROWS, tm), BF16)
    for kind, v in enumerate((mv, fv)):
        vt = v.T.astype(BF16)
        for h in range(N_HEADS):
            base = (kind * N_HEADS + h) * HEAD_ROWS
            vt_ref[0, 0, base:base + HEAD_DIM, :] = vt[h * HEAD_DIM:(h + 1) * HEAD_DIM]
            vt_ref[0, 0, base + HEAD_DIM:base + HEAD_ROWS, :] = ones
    fkb = fk.astype(BF16)
    for p in range(N_PAIRS):
        fkp_ref[0, :, 2 * p * PAIR_W:(2 * p + 1) * PAIR_W] = fkb[:, p * PAIR_W:(p + 1) * PAIR_W]
        fkp_ref[0, :, (2 * p + 1) * PAIR_W:(2 * p + 2) * PAIR_W] = kb[:, p * PAIR_W:(p + 1) * PAIR_W]

    cv = tail[:, LANES:]
    cvb = cv[:, 0:CONV_CH]
    u = cv[:, CONV_CH:2 * CONV_CH] * cv[:, 2 * CONV_CH:3 * CONV_CH]
    uc = uc_ref[...]
    prev1 = uc[7:8]
    prev2 = uc[6:7]
    ridx = lax.broadcasted_iota(jnp.int32, u.shape, 0)
    u1 = jnp.where(ridx == 0, prev1, pltpu.roll(u, 1, 0))
    u2 = jnp.where(ridx == 0, prev2, jnp.where(ridx == 1, prev1, pltpu.roll(u, 2, 0)))
    cw = cw_ref[...]
    y = cw[0:1] * u2 + cw[1:2] * u1 + cw[2:3] * u
    conv_ref[0] = (cvb * y).astype(BF16)
    uc_ref[...] = u[tm - 8:tm]


def _inproj(x, g, wqkv, wtail, bfp, cw, tri, perm):
    B, S, D = x.shape
    tm = CHUNK
    const = lambda shape: pl.BlockSpec(shape, lambda b, s: (0,) * len(shape))
    row = lambda w: pl.BlockSpec((1, tm, w), lambda b, s: (b, s, 0))
    colmajor = lambda w: pl.BlockSpec((1, 1, w, tm), lambda b, s: (b, s, 0, 0))
    return pl.pallas_call(
        _inproj_kernel,
        grid=(B, S // tm),
        in_specs=[row(D), const((1, D)), const(wqkv.shape), const(wtail.shape),
                  const(bfp.shape), const(cw.shape), const(tri.shape), const(perm.shape)],
        out_specs=[colmajor(ATT_W), row(ATT_W), colmajor(ATT_W), row(2 * ATT_W),
                   pl.BlockSpec((1, 1, 2 * N_HEADS * HEAD_ROWS, tm), lambda b, s: (b, s, 0, 0)),
                   pl.BlockSpec((1, N_HEADS * MAX_BLOCKS, tm), lambda b, s: (b, 0, s)),
                   row(CONV_CH)],
        out_shape=[jax.ShapeDtypeStruct((B, S // tm, ATT_W, tm), BF16),
                   jax.ShapeDtypeStruct((B, S, ATT_W), BF16),
                   jax.ShapeDtypeStruct((B, S // tm, ATT_W, tm), BF16),
                   jax.ShapeDtypeStruct((B, S, 2 * ATT_W), BF16),
                   jax.ShapeDtypeStruct((B, S // tm, 2 * N_HEADS * HEAD_ROWS, tm), BF16),
                   jax.ShapeDtypeStruct((B, N_HEADS * MAX_BLOCKS, S), F32),
                   jax.ShapeDtypeStruct((B, S, CONV_CH), BF16)],
        scratch_shapes=[pltpu.VMEM((N_HEADS * MAX_BLOCKS, ATT_W), F32),
                        pltpu.VMEM((1, LANES), F32),
                        pltpu.VMEM((8, CONV_CH), F32)],
        compiler_params=pltpu.CompilerParams(
            dimension_semantics=("arbitrary", "arbitrary"), vmem_limit_bytes=VMEM_LIMIT),
        name="inproj",
    )(x, g, wqkv, wtail, bfp, cw, tri, perm)


def _attn_kernel(*refs, moba, nchunk):
    if moba:
        rb_ref, qt_ref, k_ref, vt_ref, sel_ref, own_ref, corner_ref, ind_ref, o_ref, sa_ref, sb_ref = refs
        pair = pl.program_id(1)
        xrow = lax.broadcasted_iota(jnp.int32, (XROWS, CHUNK), 0)
        far_rows, near_fix = [], []
        for hh in range(2):
            fb = rb_ref[REL_BUCKETS - 1, 2 * pair + hh] * LOG2E
            terms = _split_bf16(jnp.full((XROWS, CHUNK), fb, F32), N_SPLIT)
            rows = jnp.zeros((XROWS, CHUNK), F32)
            for t in range(N_SPLIT):
                rows = jnp.where(xrow == BLOCKS_PER_CHUNK + t, terms[t].astype(F32), rows)
            far_rows.append(rows)
            near_fix.append(corner_ref[hh] - fb)
        xpad = jnp.zeros((PAIR_W - XROWS, CHUNK), BF16)
    else:
        qt_ref, k_ref, vt_ref, o_ref, sa_ref, sb_ref = refs
    bufs = (sa_ref, sb_ref)
    hidx = [jnp.minimum(pl.program_id(0), 0) + hh for hh in range(2)]

    def weights(qi):
        qt = qt_ref[0, qi]
        frow = lax.broadcasted_iota(jnp.int32, qt.shape, 0)
        zero = jnp.zeros_like(qt)
        ws = []
        for hh in range(2):
            w = jnp.where((frow >= hh * HEAD_DIM) & (frow < (hh + 1) * HEAD_DIM), qt, zero)
            if not moba:
                pick = (frow >= N_SPLIT * hh) & (frow < N_SPLIT * (hh + 1))
                w = jnp.concatenate([w, jnp.where(pick, 1.0, 0.0).astype(BF16)], axis=0)
            ws.append(w)
        return ws

    def produce(qi, ws, c, dst_ref):
        own, near = c == qi, c == qi - 1
        kc = k_ref[0, c * CHUNK:(c + 1) * CHUNK, :]
        if moba:
            kc = jnp.concatenate([kc, ind_ref[...]], axis=1)
        maxima = []
        for hh in range(2):
            w = ws[hh]
            if moba:
                rows = jnp.zeros((XROWS, CHUNK), F32) if own else far_rows[hh]
                for r in range(BLOCKS_PER_CHUNK):
                    n = hh * MAX_BLOCKS + c * BLOCKS_PER_CHUNK + r
                    sel = sel_ref[0, n:n + 1, qi * CHUNK:(qi + 1) * CHUNK]
                    rows = jnp.where(xrow == r, sel, rows)
                w = jnp.concatenate([w, rows.astype(BF16), xpad], axis=0)
            s = jnp.dot(kc, w, preferred_element_type=F32)
            if moba and own:
                s = s + own_ref[hh]
            elif moba and near:
                lo = CHUNK - CORNER
                fixed = jnp.concatenate([s[lo:, :CORNER] + near_fix[hh], s[lo:, CORNER:]], axis=1)
                s = jnp.concatenate([s[:lo], fixed], axis=0)
            elif own:
                kidx = lax.broadcasted_iota(jnp.int32, s.shape, 0)
                qidx = lax.broadcasted_iota(jnp.int32, s.shape, 1)
                s = jnp.where(kidx <= qidx, s, NEG)
            dst_ref[hidx[hh]] = s
            maxima.append(jnp.max(s, axis=0, keepdims=True))
        return tuple(maxima)

    def consume(c, src_ref, maxima, state):
        new = []
        for hh in range(2):
            m_old, acc = state[hh]
            m_new = jnp.maximum(m_old, maxima[hh])
            p = jnp.exp2(src_ref[hidx[hh]] - m_new).astype(BF16)
            vth = vt_ref[0, c, hh * HEAD_ROWS:(hh + 1) * HEAD_ROWS, :]
            pv = jnp.dot(vth, p, preferred_element_type=F32)
            new.append((m_new, jnp.exp2(m_old - m_new) * acc + pv))
        return tuple(new)

    g = 0
    ws = weights(0)
    maxima = produce(0, ws, 0, bufs[0])
    for qi in range(nchunk):
        state = tuple((jnp.full((1, CHUNK), NEG, F32), jnp.zeros((HEAD_ROWS, CHUNK), F32))
                      for _ in range(2))
        ws_next = weights(qi + 1) if qi + 1 < nchunk else None
        for c in range(qi, -1, -1):
            nxt = None
            if c > 0:
                nxt = produce(qi, ws, c - 1, bufs[(g + 1) % 2])
            elif ws_next is not None:
                nxt = produce(qi + 1, ws_next, qi + 1, bufs[(g + 1) % 2])
            state = consume(c, bufs[g % 2], maxima, state)
            maxima = nxt
            g += 1
        ws = ws_next
        ot = jnp.concatenate([acc[:HEAD_DIM] * (1.0 / acc[HEAD_DIM:HEAD_DIM + 1])
                              for (_, acc) in state], axis=0)
        o_ref[0, qi * CHUNK:(qi + 1) * CHUNK, :] = ot.T.astype(BF16)


def _block_indicator():
    ind = np.zeros((CHUNK, PAIR_W), np.float32)
    for r in range(BLOCKS_PER_CHUNK):
        ind[r * BLK:(r + 1) * BLK, r] = 1.0
    ind[:, BLOCKS_PER_CHUNK:BLOCKS_PER_CHUNK + N_SPLIT] = 1.0
    return jnp.asarray(ind, BF16)


def _attention(qt, k, vt, sel=None, tables=None, rel_bias=None):
    moba = sel is not None
    B, S, _ = k.shape
    nchunk = S // CHUNK
    kw = k.shape[-1] // N_PAIRS
    vt_off = 0 if moba else N_PAIRS
    in_specs = [pl.BlockSpec((1, nchunk, PAIR_W, CHUNK), lambda b, p: (b, 0, p, 0)),
                pl.BlockSpec((1, S, kw), lambda b, p: (b, 0, p)),
                pl.BlockSpec((1, nchunk, PAIR_ROWS, CHUNK), lambda b, p: (b, 0, p + vt_off, 0))]
    args = [qt, k, vt]
    if moba:
        in_specs = [pl.BlockSpec(memory_space=pltpu.SMEM)] + in_specs
        in_specs += [pl.BlockSpec((1, 2 * MAX_BLOCKS, S), lambda b, p: (b, p, 0)),
                     pl.BlockSpec((2, CHUNK, CHUNK), lambda b, p: (p, 0, 0)),
                     pl.BlockSpec((2, CORNER, CORNER), lambda b, p: (p, 0, 0)),
                     pl.BlockSpec((CHUNK, PAIR_W), lambda b, p: (0, 0))]
        args = [rel_bias] + args + [sel, *tables, _block_indicator()]
    return pl.pallas_call(
        functools.partial(_attn_kernel, moba=moba, nchunk=nchunk),
        grid=(B, N_PAIRS),
        in_specs=in_specs,
        out_specs=pl.BlockSpec((1, S, PAIR_W), lambda b, p: (b, 0, p)),
        out_shape=jax.ShapeDtypeStruct((B, S, ATT_W), BF16),
        scratch_shapes=[pltpu.VMEM((2, CHUNK, CHUNK), F32), pltpu.VMEM((2, CHUNK, CHUNK), F32)],
        compiler_params=pltpu.CompilerParams(
            dimension_semantics=("arbitrary", "arbitrary"), vmem_limit_bytes=VMEM_LIMIT),
        name="moba_attn" if moba else "fox_attn",
    )(*args)


def _out_ffn_kernel(moba_ref, fox_ref, conv_ref, x_ref, wo_ref, wg_ref, wu_ref, wd_ref,
                    gpm_ref, gpf_ref, gqf_ref, o_ref):
    for lo_row in range(0, FFN_TILE, FFN_ROWS):
        rows = slice(lo_row, lo_row + FFN_ROWS)
        mix_in = jnp.concatenate([moba_ref[rows, :], fox_ref[rows, :], conv_ref[rows, :]], axis=1)
        mixed = jnp.dot(mix_in, wo_ref[...], preferred_element_type=F32)
        x1 = x_ref[rows, :] + _rms(mixed, gpm_ref[...])
        h2 = _rms(x1, gpf_ref[...]).astype(BF16)
        ff = jnp.zeros(x1.shape, F32)
        lo = 0
        for width in FFN_CHUNKS:
            gate = jnp.dot(h2, wg_ref[:, lo:lo + width], preferred_element_type=F32)
            up = jnp.dot(h2, wu_ref[:, lo:lo + width], preferred_element_type=F32)
            act = (gate * (1.0 / (1.0 + jnp.exp(-gate))) * up).astype(BF16)
            ff = ff + jnp.dot(act, wd_ref[lo:lo + width, :], preferred_element_type=F32)
            lo += width
        o_ref[rows, :] = x1 + _rms(ff, gqf_ref[...])


def _out_ffn(moba, fox, conv, x, wo, wg, wu, wd, gpm, gpf, gqf):
    M, D = x.shape
    tm = FFN_TILE
    row = lambda w: pl.BlockSpec((tm, w), lambda m: (m, 0))
    const = lambda shape: pl.BlockSpec(shape, lambda m: (0, 0), pipeline_mode=pl.Buffered(1))
    return pl.pallas_call(
        _out_ffn_kernel,
        grid=(M // tm,),
        in_specs=[row(ATT_W), row(ATT_W), row(CONV_CH), row(D),
                  const(wo.shape), const(wg.shape), const(wu.shape), const(wd.shape),
                  const((1, D)), const((1, D)), const((1, D))],
        out_specs=row(D),
        out_shape=jax.ShapeDtypeStruct((M, D), F32),
        compiler_params=pltpu.CompilerParams(
            dimension_semantics=("arbitrary",), vmem_limit_bytes=VMEM_LIMIT),
        name="out_ffn",
    )(moba, fox, conv, x, wo, wg, wu, wd, gpm, gpf, gqf)


def _cumsum_constants():
    tri = np.tril(np.ones((BLK, BLK), np.float32))
    perm = np.zeros((N_SPLIT * LANES, ATT_W), np.float32)
    for s in range(N_SPLIT):
        for h in range(N_HEADS):
            perm[s * LANES + h, PAIR_W * (h // 2) + N_SPLIT * (h % 2) + s] = 1.0
    return jnp.asarray(tri, BF16), jnp.asarray(perm, BF16)


def kernel(x, w_in, b_f, conv_w, w_out, rel_bias, g_pre_mix, g_post_mix, g_pre_ffn, g_post_ffn,
           w_gate, w_up, w_down):
    B, S, D = x.shape
    depth = w_in.shape[0]
    assert D == D_MODEL and S % CHUNK == 0 and S // BLK <= MAX_BLOCKS
    assert (B * S) % FFN_TILE == 0

    rel_bias = rel_bias.astype(F32)
    tables = _bias_tiles(rel_bias)
    w_in_t = jnp.swapaxes(w_in, 1, 2)
    tri, perm = _cumsum_constants()
    a = ATT_W
    for l in range(depth):
        wl = w_in_t[l]
        wqkv = wl[0:6 * a].astype(BF16)
        wtail = jnp.concatenate([wl[6 * a:6 * a + N_HEADS], jnp.zeros((LANES - N_HEADS, D), wl.dtype),
                                 wl[6 * a + N_HEADS:]], axis=0).astype(BF16)
        bfp = jnp.pad(b_f[l].astype(F32), (0, LANES - N_HEADS)).reshape(1, LANES)
        row = lambda g: g[l].astype(F32).reshape(1, D)

        mq, mk, fq, fkp, vt, sel, conv = _inproj(
            x, row(g_pre_mix), wqkv, wtail, bfp, conv_w[l].astype(F32), tri, perm)
        moba = _attention(mq, mk, vt, sel, tables, rel_bias)
        fox = _attention(fq, fkp, vt)
        x = _out_ffn(moba.reshape(B * S, a), fox.reshape(B * S, a), conv.reshape(B * S, CONV_CH),
                     x.reshape(B * S, D), w_out[l].astype(BF16), w_gate[l].astype(BF16),
                     w_up[l].astype(BF16), w_down[l].astype(BF16),
                     row(g_post_mix), row(g_pre_ffn), row(g_post_ffn)).reshape(B, S, D)
    return x
```

```python
import functools
import math

import jax
import jax.numpy as jnp
import numpy as np
from jax import lax
from jax.experimental import pallas as pl
from jax.experimental.pallas import tpu as pltpu

D_MODEL = 1024
HEAD_DIM = 64
N_HEADS = 6
ATT_W = N_HEADS * HEAD_DIM
PAIR_W = 2 * HEAD_DIM
N_PAIRS = N_HEADS // 2
CONV_CH = 256
CONV_WIDTH = 3
BLK = 256
BLK_SHIFT = BLK.bit_length() - 1
CHUNK = 512
BLOCKS_PER_CHUNK = CHUNK // BLK
MAX_BLOCKS = 16
TOPK = 3
REL_BUCKETS = 32
REL_MAX_DIST = 128
D_FF = 2816
RMS_EPS = 1e-6
NEG = -1e30
LANES = 128
N_SPLIT = 3
LOG2E = math.log2(math.e)
CORNER = REL_MAX_DIST
ONES_ROWS = 16
HEAD_ROWS = HEAD_DIM + ONES_ROWS
PAIR_ROWS = 2 * HEAD_ROWS
XROWS = 16

F32 = jnp.float32
BF16 = jnp.bfloat16
NT_DIMS = (((1,), (1,)), ((), ()))

FFN_TILE = 512
FFN_CHUNKS = (1536, 1280)
assert sum(FFN_CHUNKS) == D_FF
VMEM_LIMIT = 50 * 1024 * 1024


def _split_bf16(v, n):
    terms = []
    for _ in range(n):
        t = v.astype(BF16)
        terms.append(t)
        v = v - t.astype(F32)
    return terms


def _rms(v, g):
    return v * lax.rsqrt(jnp.mean(v * v, axis=-1, keepdims=True) + RMS_EPS) * g


def _bucket_tiles():
    max_exact = REL_BUCKETS // 2

    def bucket(dist):
        n = np.maximum(dist, 0)
        nf = np.maximum(n, 1).astype(np.float32)
        large = max_exact + (np.log(nf / np.float32(max_exact))
                             / np.float32(math.log(REL_MAX_DIST / max_exact))
                             * np.float32(REL_BUCKETS - max_exact)).astype(np.int32)
        large = np.minimum(large, REL_BUCKETS - 1)
        return np.where(n < max_exact, n, large).astype(np.int32)

    key = np.arange(CHUNK, dtype=np.int32)[:, None]
    qry = np.arange(CHUNK, dtype=np.int32)[None, :]
    d0 = qry - key
    own = np.where(d0 >= 0, bucket(d0), -1).astype(np.int32)
    prev = bucket(d0 + CHUNK)
    corner = prev[CHUNK - CORNER:, :CORNER]
    outside = prev.copy()
    outside[CHUNK - CORNER:, :CORNER] = REL_BUCKETS - 1
    assert (outside == REL_BUCKETS - 1).all()
    return own, np.ascontiguousarray(corner)


def _bias_tile_kernel(rb_ref, own_bm_ref, corner_bm_ref, own_ref, corner_ref):
    h = pl.program_id(0)
    for bm_ref, out_ref in ((own_bm_ref, own_ref), (corner_bm_ref, corner_ref)):
        bm = bm_ref[...]
        t = jnp.full(bm.shape, NEG, F32)
        for b in range(REL_BUCKETS):
            t = jnp.where(bm == b, rb_ref[b, h] * LOG2E, t)
        out_ref[0] = t


def _bias_tiles(rel_bias):
    own_bm, corner_bm = _bucket_tiles()
    return pl.pallas_call(
        _bias_tile_kernel,
        grid=(N_HEADS,),
        in_specs=[pl.BlockSpec(memory_space=pltpu.SMEM),
                  pl.BlockSpec((CHUNK, CHUNK), lambda h: (0, 0)),
                  pl.BlockSpec((CORNER, CORNER), lambda h: (0, 0))],
        out_specs=[pl.BlockSpec((1, CHUNK, CHUNK), lambda h: (h, 0, 0)),
                   pl.BlockSpec((1, CORNER, CORNER), lambda h: (h, 0, 0))],
        out_shape=[jax.ShapeDtypeStruct((N_HEADS, CHUNK, CHUNK), F32),
                   jax.ShapeDtypeStruct((N_HEADS, CORNER, CORNER), F32)],
        name="bias_tiles",
    )(rel_bias, jnp.asarray(own_bm), jnp.asarray(corner_bm))


def _inproj_kernel(x_ref, g_ref, wqkv_ref, wtail_ref, bf_ref, cw_ref, tri_ref, perm_ref,
                   mq_ref, mk_ref, fq_ref, fkp_ref, vt_ref, sel_ref, conv_ref,
                   km_ref, cum_ref, uc_ref):
    tm = CHUNK
    st = pl.program_id(1)

    @pl.when(st == 0)
    def _():
        km_ref[...] = jnp.zeros_like(km_ref)
        cum_ref[...] = jnp.zeros_like(cum_ref)
        uc_ref[...] = jnp.zeros_like(uc_ref)

    hn = _rms(x_ref[0], g_ref[...]).astype(BF16)

    qkv = lax.dot_general(hn, wqkv_ref[...], NT_DIMS, preferred_element_type=F32)
    tail = lax.dot_general(hn, wtail_ref[...], NT_DIMS, preferred_element_type=F32)
    mq = qkv[:, 0 * ATT_W:1 * ATT_W]
    mk = qkv[:, 1 * ATT_W:2 * ATT_W]
    mv = qkv[:, 2 * ATT_W:3 * ATT_W]
    fq = qkv[:, 3 * ATT_W:4 * ATT_W]
    fk = qkv[:, 4 * ATT_W:5 * ATT_W]
    fv = qkv[:, 5 * ATT_W:6 * ATT_W]
    scale = HEAD_DIM ** -0.5 * LOG2E
    mq_ref[0, 0] = (mq * scale).T.astype(BF16)
    mk_ref[0] = mk.astype(BF16)
    fq_ref[0, 0] = (fq * scale).T.astype(BF16)
    ones = jnp.ones((ONES_ROWS, tm), BF16)
    for kind, v in enumerate((mv, fv)):
        vt = v.T.astype(BF16)
        for h in range(N_HEADS):
            base = (kind * N_HEADS + h) * HEAD_ROWS
            vt_ref[0, 0, base:base + HEAD_DIM, :] = vt[h * HEAD_DIM:(h + 1) * HEAD_DIM]
            vt_ref[0, 0, base + HEAD_DIM:base + HEAD_ROWS, :] = ones

    rows = lax.broadcasted_iota(jnp.int32, km_ref.shape, 0)
    cols = lax.broadcasted_iota(jnp.int32, km_ref.shape, 1)
    same_head = (rows >> 4) == (cols >> 6)
    km = km_ref[...]
    for r in range(BLOCKS_PER_CHUNK):
        kmean = jnp.mean(mk[r * BLK:(r + 1) * BLK], axis=0, keepdims=True)
        n = st * BLOCKS_PER_CHUNK + r
        km = jnp.where(same_head & ((rows & (MAX_BLOCKS - 1)) == n), kmean, km)
    km_ref[...] = km

    km_hi, km_lo = _split_bf16(km, 2)
    q_hi, q_lo = _split_bf16(mq, 2)
    gate = (lax.dot_general(km_hi, q_hi, NT_DIMS, preferred_element_type=F32)
            + lax.dot_general(km_hi, q_lo, NT_DIMS, preferred_element_type=F32)
            + lax.dot_general(km_lo, q_hi, NT_DIMS, preferred_element_type=F32))

    nidx = lax.broadcasted_iota(jnp.int32, (MAX_BLOCKS, tm), 0)
    own = st * BLOCKS_PER_CHUNK + (lax.broadcasted_iota(jnp.int32, (MAX_BLOCKS, tm), 1) >> BLK_SHIFT)
    valid = nidx < own
    for h in range(N_HEADS):
        gh = jnp.where(valid, gate[h * MAX_BLOCKS:(h + 1) * MAX_BLOCKS], -jnp.inf)
        rank = jnp.zeros((MAX_BLOCKS, tm), jnp.int32)
        for m in range(MAX_BLOCKS):
            gm = gh[m:m + 1, :]
            gt = (gm > gh).astype(jnp.int32)
            ge = (gm >= gh).astype(jnp.int32)
            rank = rank + jnp.where(nidx > m, ge, gt)
        keep = ((rank < TOPK) & valid) | (nidx == own)
        sel_ref[0, h * MAX_BLOCKS:(h + 1) * MAX_BLOCKS, :] = jnp.where(keep, 0.0, NEG)

    fl = tail[:, 0:LANES] + bf_ref[...]
    lf = jnp.minimum(fl, 0.0) - jnp.log1p(jnp.exp(-jnp.abs(fl)))
    tri = tri_ref[...]
    cum = cum_ref[...]
    for t in _split_bf16(lf, N_SPLIT):
        cum = cum + jnp.dot(tri, t, preferred_element_type=F32)
    cum_ref[...] = cum[tm - 1:tm, :]
    kb = jnp.zeros((tm, ATT_W), F32)
    for s, t in enumerate(_split_bf16(cum * -LOG2E, N_SPLIT)):
        kb = kb + jnp.dot(t, perm_ref[s], preferred_element_type=F32)
    kb = kb.astype(BF16)
    fkb = fk.astype(BF16)
    for p in range(N_PAIRS):
        fkp_ref[0, :, 2 * p * PAIR_W:(2 * p + 1) * PAIR_W] = fkb[:, p * PAIR_W:(p + 1) * PAIR_W]
        fkp_ref[0, :, (2 * p + 1) * PAIR_W:(2 * p + 2) * PAIR_W] = kb[:, p * PAIR_W:(p + 1) * PAIR_W]

    cv = tail[:, LANES:]
    cvb = cv[:, 0:CONV_CH]
    u = cv[:, CONV_CH:2 * CONV_CH] * cv[:, 2 * CONV_CH:3 * CONV_CH]
    uc = uc_ref[...]
    prev1 = uc[7:8]
    prev2 = uc[6:7]
    ridx = lax.broadcasted_iota(jnp.int32, u.shape, 0)
    u1 = jnp.where(ridx == 0, prev1, pltpu.roll(u, 1, 0))
    u2 = jnp.where(ridx == 0, prev2, jnp.where(ridx == 1, prev1, pltpu.roll(u, 2, 0)))
    cw = cw_ref[...]
    y = cw[0:1] * u2 + cw[1:2] * u1 + cw[2:3] * u
    conv_ref[0] = (cvb * y).astype(BF16)
    uc_ref[...] = u[tm - 8:tm]


def _inproj(x, g, wqkv, wtail, bfp, cw, tri, perm):
    B, S, D = x.shape
    tm = CHUNK
    const = lambda shape: pl.BlockSpec(shape, lambda b, s: (0,) * len(shape))
    row = lambda w: pl.BlockSpec((1, tm, w), lambda b, s: (b, s, 0))
    colmajor = lambda w: pl.BlockSpec((1, 1, w, tm), lambda b, s: (b, s, 0, 0))
    return pl.pallas_call(
        _inproj_kernel,
        grid=(B, S // tm),
        in_specs=[row(D), const((1, D)), const(wqkv.shape), const(wtail.shape),
                  const(bfp.shape), const(cw.shape), const(tri.shape), const(perm.shape)],
        out_specs=[colmajor(ATT_W), row(ATT_W), colmajor(ATT_W), row(2 * ATT_W),
                   pl.BlockSpec((1, 1, 2 * N_HEADS * HEAD_ROWS, tm), lambda b, s: (b, s, 0, 0)),
                   pl.BlockSpec((1, N_HEADS * MAX_BLOCKS, tm), lambda b, s: (b, 0, s)),
                   row(CONV_CH)],
        out_shape=[jax.ShapeDtypeStruct((B, S // tm, ATT_W, tm), BF16),
                   jax.ShapeDtypeStruct((B, S, ATT_W), BF16),
                   jax.ShapeDtypeStruct((B, S // tm, ATT_W, tm), BF16),
                   jax.ShapeDtypeStruct((B, S, 2 * ATT_W), BF16),
                   jax.ShapeDtypeStruct((B, S // tm, 2 * N_HEADS * HEAD_ROWS, tm), BF16),
                   jax.ShapeDtypeStruct((B, N_HEADS * MAX_BLOCKS, S), F32),
                   jax.ShapeDtypeStruct((B, S, CONV_CH), BF16)],
        scratch_shapes=[pltpu.VMEM((N_HEADS * MAX_BLOCKS, ATT_W), F32),
                        pltpu.VMEM((1, LANES), F32),
                        pltpu.VMEM((8, CONV_CH), F32)],
        compiler_params=pltpu.CompilerParams(
            dimension_semantics=("arbitrary", "arbitrary"), vmem_limit_bytes=VMEM_LIMIT),
        name="inproj",
    )(x, g, wqkv, wtail, bfp, cw, tri, perm)


def _attn_kernel(*refs, moba, nchunk):
    if moba:
        rb_ref, qt_ref, k_ref, vt_ref, sel_ref, own_ref, corner_ref, ind_ref, o_ref, sa_ref, sb_ref = refs
        pair = pl.program_id(1)
        xrow = lax.broadcasted_iota(jnp.int32, (XROWS, CHUNK), 0)
        far_rows, near_fix = [], []
        for hh in range(2):
            fb = rb_ref[REL_BUCKETS - 1, 2 * pair + hh] * LOG2E
            terms = _split_bf16(jnp.full((XROWS, CHUNK), fb, F32), N_SPLIT)
            rows = jnp.zeros((XROWS, CHUNK), F32)
            for t in range(N_SPLIT):
                rows = jnp.where(xrow == BLOCKS_PER_CHUNK + t, terms[t].astype(F32), rows)
            far_rows.append(rows)
            near_fix.append(corner_ref[hh] - fb)
        xpad = jnp.zeros((PAIR_W - XROWS, CHUNK), BF16)
    else:
        qt_ref, k_ref, vt_ref, o_ref, sa_ref, sb_ref = refs
    bufs = (sa_ref, sb_ref)
    hidx = [jnp.minimum(pl.program_id(0), 0) + hh for hh in range(2)]

    def weights(qi):
        qt = qt_ref[0, qi]
        frow = lax.broadcasted_iota(jnp.int32, qt.shape, 0)
        zero = jnp.zeros_like(qt)
        ws = []
        for hh in range(2):
            w = jnp.where((frow >= hh * HEAD_DIM) & (frow < (hh + 1) * HEAD_DIM), qt, zero)
            if not moba:
                pick = (frow >= N_SPLIT * hh) & (frow < N_SPLIT * (hh + 1))
                w = jnp.concatenate([w, jnp.where(pick, 1.0, 0.0).astype(BF16)], axis=0)
            ws.append(w)
        return ws

    def produce(qi, ws, c, dst_ref):
        own, near = c == qi, c == qi - 1
        kc = k_ref[0, c * CHUNK:(c + 1) * CHUNK, :]
        if moba:
            kc = jnp.concatenate([kc, ind_ref[...]], axis=1)
        maxima = []
        for hh in range(2):
            w = ws[hh]
            if moba:
                rows = jnp.zeros((XROWS, CHUNK), F32) if own else far_rows[hh]
                for r in range(BLOCKS_PER_CHUNK):
                    n = hh * MAX_BLOCKS + c * BLOCKS_PER_CHUNK + r
                    sel = sel_ref[0, n:n + 1, qi * CHUNK:(qi + 1) * CHUNK]
                    rows = jnp.where(xrow == r, sel, rows)
                w = jnp.concatenate([w, rows.astype(BF16), xpad], axis=0)
            s = jnp.dot(kc, w, preferred_element_type=F32)
            if moba and own:
                s = s + own_ref[hh]
            elif moba and near:
                lo = CHUNK - CORNER
                fixed = jnp.concatenate([s[lo:, :CORNER] + near_fix[hh], s[lo:, CORNER:]], axis=1)
                s = jnp.concatenate([s[:lo], fixed], axis=0)
            elif own:
                kidx = lax.broadcasted_iota(jnp.int32, s.shape, 0)
                qidx = lax.broadcasted_iota(jnp.int32, s.shape, 1)
                s = jnp.where(kidx <= qidx, s, NEG)
            dst_ref[hidx[hh]] = s
            maxima.append(jnp.max(s, axis=0, keepdims=True))
        return tuple(maxima)

    def consume(c, src_ref, maxima, state):
        new = []
        for hh in range(2):
            m_old, acc = state[hh]
            m_new = jnp.maximum(m_old, maxima[hh])
            p = jnp.exp2(src_ref[hidx[hh]] - m_new).astype(BF16)
            vth = vt_ref[0, c, hh * HEAD_ROWS:(hh + 1) * HEAD_ROWS, :]
            pv = jnp.dot(vth, p, preferred_element_type=F32)
            new.append((m_new, jnp.exp2(m_old - m_new) * acc + pv))
        return tuple(new)

    g = 0
    ws = weights(0)
    maxima = produce(0, ws, 0, bufs[0])
    for qi in range(nchunk):
        state = tuple((jnp.full((1, CHUNK), NEG, F32), jnp.zeros((HEAD_ROWS, CHUNK), F32))
                      for _ in range(2))
        ws_next = weights(qi + 1) if qi + 1 < nchunk else None
        for c in range(qi, -1, -1):
            nxt = None
            if c > 0:
                nxt = produce(qi, ws, c - 1, bufs[(g + 1) % 2])
            elif ws_next is not None:
                nxt = produce(qi + 1, ws_next, qi + 1, bufs[(g + 1) % 2])
            state = consume(c, bufs[g % 2], maxima, state)
            maxima = nxt
            g += 1
        ws = ws_next
        ot = jnp.concatenate([acc[:HEAD_DIM] * (1.0 / acc[HEAD_DIM:HEAD_DIM + 1])
                              for (_, acc) in state], axis=0)
        o_ref[0, qi * CHUNK:(qi + 1) * CHUNK, :] = ot.T.astype(BF16)


def _block_indicator():
    ind = np.zeros((CHUNK, PAIR_W), np.float32)
    for r in range(BLOCKS_PER_CHUNK):
        ind[r * BLK:(r + 1) * BLK, r] = 1.0
    ind[:, BLOCKS_PER_CHUNK:BLOCKS_PER_CHUNK + N_SPLIT] = 1.0
    return jnp.asarray(ind, BF16)


def _attention(qt, k, vt, sel=None, tables=None, rel_bias=None):
    moba = sel is not None
    B, S, _ = k.shape
    nchunk = S // CHUNK
    kw = k.shape[-1] // N_PAIRS
    vt_off = 0 if moba else N_PAIRS
    in_specs = [pl.BlockSpec((1, nchunk, PAIR_W, CHUNK), lambda b, p: (b, 0, p, 0)),
                pl.BlockSpec((1, S, kw), lambda b, p: (b, 0, p)),
                pl.BlockSpec((1, nchunk, PAIR_ROWS, CHUNK), lambda b, p: (b, 0, p + vt_off, 0))]
    args = [qt, k, vt]
    if moba:
        in_specs = [pl.BlockSpec(memory_space=pltpu.SMEM)] + in_specs
        in_specs += [pl.BlockSpec((1, 2 * MAX_BLOCKS, S), lambda b, p: (b, p, 0)),
                     pl.BlockSpec((2, CHUNK, CHUNK), lambda b, p: (p, 0, 0)),
                     pl.BlockSpec((2, CORNER, CORNER), lambda b, p: (p, 0, 0)),
                     pl.BlockSpec((CHUNK, PAIR_W), lambda b, p: (0, 0))]
        args = [rel_bias] + args + [sel, *tables, _block_indicator()]
    return pl.pallas_call(
        functools.partial(_attn_kernel, moba=moba, nchunk=nchunk),
        grid=(B, N_PAIRS),
        in_specs=in_specs,
        out_specs=pl.BlockSpec((1, S, PAIR_W), lambda b, p: (b, 0, p)),
        out_shape=jax.ShapeDtypeStruct((B, S, ATT_W), BF16),
        scratch_shapes=[pltpu.VMEM((2, CHUNK, CHUNK), F32), pltpu.VMEM((2, CHUNK, CHUNK), F32)],
        compiler_params=pltpu.CompilerParams(
            dimension_semantics=("arbitrary", "arbitrary"), vmem_limit_bytes=VMEM_LIMIT),
        name="moba_attn" if moba else "fox_attn",
    )(*args)


def _out_ffn_kernel(moba_ref, fox_ref, conv_ref, x_ref, wo_ref, wg_ref, wu_ref, wd_ref,
                    gpm_ref, gpf_ref, gqf_ref, o_ref):
    mix_in = jnp.concatenate([moba_ref[...], fox_ref[...], conv_ref[...]], axis=1)
    mixed = jnp.dot(mix_in, wo_ref[...], preferred_element_type=F32)
    x1 = x_ref[...] + _rms(mixed, gpm_ref[...])
    h2 = _rms(x1, gpf_ref[...]).astype(BF16)
    ff = jnp.zeros(x1.shape, F32)
    lo = 0
    for width in FFN_CHUNKS:
        gate = jnp.dot(h2, wg_ref[:, lo:lo + width], preferred_element_type=F32)
        up = jnp.dot(h2, wu_ref[:, lo:lo + width], preferred_element_type=F32)
        act = (gate * (1.0 / (1.0 + jnp.exp(-gate))) * up).astype(BF16)
        ff = ff + jnp.dot(act, wd_ref[lo:lo + width, :], preferred_element_type=F32)
        lo += width
    o_ref[...] = x1 + _rms(ff, gqf_ref[...])


def _out_ffn(moba, fox, conv, x, wo, wg, wu, wd, gpm, gpf, gqf):
    M, D = x.shape
    tm = FFN_TILE
    row = lambda w: pl.BlockSpec((tm, w), lambda m: (m, 0))
    const = lambda shape: pl.BlockSpec(shape, lambda m: (0, 0), pipeline_mode=pl.Buffered(1))
    return pl.pallas_call(
        _out_ffn_kernel,
        grid=(M // tm,),
        in_specs=[row(ATT_W), row(ATT_W), row(CONV_CH), row(D),
                  const(wo.shape), const(wg.shape), const(wu.shape), const(wd.shape),
                  const((1, D)), const((1, D)), const((1, D))],
        out_specs=row(D),
        out_shape=jax.ShapeDtypeStruct((M, D), F32),
        compiler_params=pltpu.CompilerParams(
            dimension_semantics=("arbitrary",), vmem_limit_bytes=VMEM_LIMIT),
        name="out_ffn",
    )(moba, fox, conv, x, wo, wg, wu, wd, gpm, gpf, gqf)


def _cumsum_constants(tm):
    tri = np.tril(np.ones((tm, tm), np.float32))
    perm = np.zeros((N_SPLIT, LANES, ATT_W), np.float32)
    for s in range(N_SPLIT):
        for h in range(N_HEADS):
            perm[s, h, PAIR_W * (h // 2) + N_SPLIT * (h % 2) + s] = 1.0
    return jnp.asarray(tri, BF16), jnp.asarray(perm, BF16)


def kernel(x, w_in, b_f, conv_w, w_out, rel_bias, g_pre_mix, g_post_mix, g_pre_ffn, g_post_ffn,
           w_gate, w_up, w_down):
    B, S, D = x.shape
    depth = w_in.shape[0]
    assert D == D_MODEL and S % CHUNK == 0 and S // BLK <= MAX_BLOCKS
    assert (B * S) % FFN_TILE == 0

    rel_bias = rel_bias.astype(F32)
    tables = _bias_tiles(rel_bias)
    w_in_t = jnp.swapaxes(w_in, 1, 2)
    tri, perm = _cumsum_constants(CHUNK)
    a = ATT_W
    for l in range(depth):
        wl = w_in_t[l]
        wqkv = wl[0:6 * a].astype(BF16)
        wtail = jnp.concatenate([wl[6 * a:6 * a + N_HEADS], jnp.zeros((LANES - N_HEADS, D), wl.dtype),
                                 wl[6 * a + N_HEADS:]], axis=0).astype(BF16)
        bfp = jnp.pad(b_f[l].astype(F32), (0, LANES - N_HEADS)).reshape(1, LANES)
        row = lambda g: g[l].astype(F32).reshape(1, D)

        mq, mk, fq, fkp, vt, sel, conv = _inproj(
            x, row(g_pre_mix), wqkv, wtail, bfp, conv_w[l].astype(F32), tri, perm)
        moba = _attention(mq, mk, vt, sel, tables, rel_bias)
        fox = _attention(fq, fkp, vt)
        x = _out_ffn(moba.reshape(B * S, a), fox.reshape(B * S, a), conv.reshape(B * S, CONV_CH),
                     x.reshape(B * S, D), w_out[l].astype(BF16), w_gate[l].astype(BF16),
                     w_up[l].astype(BF16), w_down[l].astype(BF16),
                     row(g_post_mix), row(g_pre_ffn), row(g_post_ffn)).reshape(B, S, D)
    return x
```

```python
import functools
import math

import jax
import jax.numpy as jnp
import numpy as np
from jax import lax
from jax.experimental import pallas as pl
from jax.experimental.pallas import tpu as pltpu

D_MODEL = 1024
HEAD_DIM = 64
N_HEADS = 6
ATT_W = N_HEADS * HEAD_DIM
PAIR_W = 2 * HEAD_DIM
N_PAIRS = N_HEADS // 2
CONV_CH = 256
CONV_WIDTH = 3
BLK = 256
BLK_SHIFT = BLK.bit_length() - 1
CHUNK = 512
BLOCKS_PER_CHUNK = CHUNK // BLK
MAX_BLOCKS = 16
TOPK = 3
REL_BUCKETS = 32
REL_MAX_DIST = 128
D_FF = 2816
RMS_EPS = 1e-6
NEG = -1e30
LANES = 128
N_SPLIT = 3
LOG2E = math.log2(math.e)
CORNER = REL_MAX_DIST
ONES_ROWS = 16
HEAD_ROWS = HEAD_DIM + ONES_ROWS
PAIR_ROWS = 2 * HEAD_ROWS
XROWS = 16

F32 = jnp.float32
BF16 = jnp.bfloat16
NT_DIMS = (((1,), (1,)), ((), ()))

FFN_TILE = 512
FFN_CHUNKS = (1536, 1280)
assert sum(FFN_CHUNKS) == D_FF
VMEM_LIMIT = 50 * 1024 * 1024


def _split_bf16(v, n):
    terms = []
    for _ in range(n):
        t = v.astype(BF16)
        terms.append(t)
        v = v - t.astype(F32)
    return terms


def _rms(v, g):
    return v * lax.rsqrt(jnp.mean(v * v, axis=-1, keepdims=True) + RMS_EPS) * g


def _bucket_tiles():
    max_exact = REL_BUCKETS // 2

    def bucket(dist):
        n = np.maximum(dist, 0)
        nf = np.maximum(n, 1).astype(np.float32)
        large = max_exact + (np.log(nf / np.float32(max_exact))
                             / np.float32(math.log(REL_MAX_DIST / max_exact))
                             * np.float32(REL_BUCKETS - max_exact)).astype(np.int32)
        large = np.minimum(large, REL_BUCKETS - 1)
        return np.where(n < max_exact, n, large).astype(np.int32)

    key = np.arange(CHUNK, dtype=np.int32)[:, None]
    qry = np.arange(CHUNK, dtype=np.int32)[None, :]
    d0 = qry - key
    own = np.where(d0 >= 0, bucket(d0), -1).astype(np.int32)
    prev = bucket(d0 + CHUNK)
    corner = prev[CHUNK - CORNER:, :CORNER]
    outside = prev.copy()
    outside[CHUNK - CORNER:, :CORNER] = REL_BUCKETS - 1
    assert (outside == REL_BUCKETS - 1).all()
    return own, np.ascontiguousarray(corner)


def _bias_tile_kernel(rb_ref, own_bm_ref, corner_bm_ref, own_ref, corner_ref):
    h = pl.program_id(0)
    for bm_ref, out_ref in ((own_bm_ref, own_ref), (corner_bm_ref, corner_ref)):
        bm = bm_ref[...]
        t = jnp.full(bm.shape, NEG, F32)
        for b in range(REL_BUCKETS):
            t = jnp.where(bm == b, rb_ref[b, h] * LOG2E, t)
        out_ref[0] = t


def _bias_tiles(rel_bias):
    own_bm, corner_bm = _bucket_tiles()
    return pl.pallas_call(
        _bias_tile_kernel,
        grid=(N_HEADS,),
        in_specs=[pl.BlockSpec(memory_space=pltpu.SMEM),
                  pl.BlockSpec((CHUNK, CHUNK), lambda h: (0, 0)),
                  pl.BlockSpec((CORNER, CORNER), lambda h: (0, 0))],
        out_specs=[pl.BlockSpec((1, CHUNK, CHUNK), lambda h: (h, 0, 0)),
                   pl.BlockSpec((1, CORNER, CORNER), lambda h: (h, 0, 0))],
        out_shape=[jax.ShapeDtypeStruct((N_HEADS, CHUNK, CHUNK), F32),
                   jax.ShapeDtypeStruct((N_HEADS, CORNER, CORNER), F32)],
        name="bias_tiles",
    )(rel_bias, jnp.asarray(own_bm), jnp.asarray(corner_bm))


def _inproj_kernel(x_ref, g_ref, wqkv_ref, wtail_ref, bf_ref, cw_ref, tri_ref, perm_ref,
                   mq_ref, mk_ref, fq_ref, fkp_ref, vt_ref, sel_ref, conv_ref,
                   km_ref, cum_ref, uc_ref):
    tm = CHUNK
    st = pl.program_id(1)

    @pl.when(st == 0)
    def _():
        km_ref[...] = jnp.zeros_like(km_ref)
        cum_ref[...] = jnp.zeros_like(cum_ref)
        uc_ref[...] = jnp.zeros_like(uc_ref)

    hn = _rms(x_ref[0], g_ref[...]).astype(BF16)

    qkv = lax.dot_general(hn, wqkv_ref[...], NT_DIMS, preferred_element_type=F32)
    tail = lax.dot_general(hn, wtail_ref[...], NT_DIMS, preferred_element_type=F32)
    mq = qkv[:, 0 * ATT_W:1 * ATT_W]
    mk = qkv[:, 1 * ATT_W:2 * ATT_W]
    mv = qkv[:, 2 * ATT_W:3 * ATT_W]
    fq = qkv[:, 3 * ATT_W:4 * ATT_W]
    fk = qkv[:, 4 * ATT_W:5 * ATT_W]
    fv = qkv[:, 5 * ATT_W:6 * ATT_W]
    scale = HEAD_DIM ** -0.5 * LOG2E
    mq_ref[0, 0] = (mq * scale).T.astype(BF16)
    mk_ref[0] = mk.astype(BF16)
    fq_ref[0, 0] = (fq * scale).T.astype(BF16)
    ones = jnp.ones((ONES_ROWS, tm), BF16)
    for kind, v in enumerate((mv, fv)):
        vt = v.T.astype(BF16)
        for h in range(N_HEADS):
            base = (kind * N_HEADS + h) * HEAD_ROWS
            vt_ref[0, 0, base:base + HEAD_DIM, :] = vt[h * HEAD_DIM:(h + 1) * HEAD_DIM]
            vt_ref[0, 0, base + HEAD_DIM:base + HEAD_ROWS, :] = ones

    rows = lax.broadcasted_iota(jnp.int32, km_ref.shape, 0)
    cols = lax.broadcasted_iota(jnp.int32, km_ref.shape, 1)
    same_head = (rows >> 4) == (cols >> 6)
    km = km_ref[...]
    for r in range(BLOCKS_PER_CHUNK):
        kmean = jnp.mean(mk[r * BLK:(r + 1) * BLK], axis=0, keepdims=True)
        n = st * BLOCKS_PER_CHUNK + r
        km = jnp.where(same_head & ((rows & (MAX_BLOCKS - 1)) == n), kmean, km)
    km_ref[...] = km

    km_hi, km_lo = _split_bf16(km, 2)
    q_hi, q_lo = _split_bf16(mq, 2)
    gate = (lax.dot_general(km_hi, q_hi, NT_DIMS, preferred_element_type=F32)
            + lax.dot_general(km_hi, q_lo, NT_DIMS, preferred_element_type=F32)
            + lax.dot_general(km_lo, q_hi, NT_DIMS, preferred_element_type=F32))

    nidx = lax.broadcasted_iota(jnp.int32, (MAX_BLOCKS, tm), 0)
    own = st * BLOCKS_PER_CHUNK + (lax.broadcasted_iota(jnp.int32, (MAX_BLOCKS, tm), 1) >> BLK_SHIFT)
    valid = nidx < own
    for h in range(N_HEADS):
        gh = jnp.where(valid, gate[h * MAX_BLOCKS:(h + 1) * MAX_BLOCKS], -jnp.inf)
        rank = jnp.zeros((MAX_BLOCKS, tm), jnp.int32)
        for m in range(MAX_BLOCKS):
            gm = gh[m:m + 1, :]
            gt = (gm > gh).astype(jnp.int32)
            ge = (gm >= gh).astype(jnp.int32)
            rank = rank + jnp.where(nidx > m, ge, gt)
        keep = ((rank < TOPK) & valid) | (nidx == own)
        sel_ref[0, h * MAX_BLOCKS:(h + 1) * MAX_BLOCKS, :] = jnp.where(keep, 0.0, NEG)

    fl = tail[:, 0:LANES] + bf_ref[...]
    lf = jnp.minimum(fl, 0.0) - jnp.log1p(jnp.exp(-jnp.abs(fl)))
    tri = tri_ref[...]
    cum = cum_ref[...]
    for t in _split_bf16(lf, N_SPLIT):
        cum = cum + jnp.dot(tri, t, preferred_element_type=F32)
    cum_ref[...] = cum[tm - 1:tm, :]
    kb = jnp.zeros((tm, ATT_W), F32)
    for s, t in enumerate(_split_bf16(cum * -LOG2E, N_SPLIT)):
        kb = kb + jnp.dot(t, perm_ref[s], preferred_element_type=F32)
    kb = kb.astype(BF16)
    fkb = fk.astype(BF16)
    for p in range(N_PAIRS):
        fkp_ref[0, :, 2 * p * PAIR_W:(2 * p + 1) * PAIR_W] = fkb[:, p * PAIR_W:(p + 1) * PAIR_W]
        fkp_ref[0, :, (2 * p + 1) * PAIR_W:(2 * p + 2) * PAIR_W] = kb[:, p * PAIR_W:(p + 1) * PAIR_W]

    cv = tail[:, LANES:]
    cvb = cv[:, 0:CONV_CH]
    u = cv[:, CONV_CH:2 * CONV_CH] * cv[:, 2 * CONV_CH:3 * CONV_CH]
    uc = uc_ref[...]
    prev1 = uc[7:8]
    prev2 = uc[6:7]
    ridx = lax.broadcasted_iota(jnp.int32, u.shape, 0)
    u1 = jnp.where(ridx == 0, prev1, pltpu.roll(u, 1, 0))
    u2 = jnp.where(ridx == 0, prev2, jnp.where(ridx == 1, prev1, pltpu.roll(u, 2, 0)))
    cw = cw_ref[...]
    y = cw[0:1] * u2 + cw[1:2] * u1 + cw[2:3] * u
    conv_ref[0] = (cvb * y).astype(BF16)
    uc_ref[...] = u[tm - 8:tm]


def _inproj(x, g, wqkv, wtail, bfp, cw, tri, perm):
    B, S, D = x.shape
    tm = CHUNK
    const = lambda shape: pl.BlockSpec(shape, lambda b, s: (0,) * len(shape))
    row = lambda w: pl.BlockSpec((1, tm, w), lambda b, s: (b, s, 0))
    colmajor = lambda w: pl.BlockSpec((1, 1, w, tm), lambda b, s: (b, s, 0, 0))
    return pl.pallas_call(
        _inproj_kernel,
        grid=(B, S // tm),
        in_specs=[row(D), const((1, D)), const(wqkv.shape), const(wtail.shape),
                  const(bfp.shape), const(cw.shape), const(tri.shape), const(perm.shape)],
        out_specs=[colmajor(ATT_W), row(ATT_W), colmajor(ATT_W), row(2 * ATT_W),
                   pl.BlockSpec((1, 1, 2 * N_HEADS * HEAD_ROWS, tm), lambda b, s: (b, s, 0, 0)),
                   pl.BlockSpec((1, N_HEADS * MAX_BLOCKS, tm), lambda b, s: (b, 0, s)),
                   row(CONV_CH)],
        out_shape=[jax.ShapeDtypeStruct((B, S // tm, ATT_W, tm), BF16),
                   jax.ShapeDtypeStruct((B, S, ATT_W), BF16),
                   jax.ShapeDtypeStruct((B, S // tm, ATT_W, tm), BF16),
                   jax.ShapeDtypeStruct((B, S, 2 * ATT_W), BF16),
                   jax.ShapeDtypeStruct((B, S // tm, 2 * N_HEADS * HEAD_ROWS, tm), BF16),
                   jax.ShapeDtypeStruct((B, N_HEADS * MAX_BLOCKS, S), F32),
                   jax.ShapeDtypeStruct((B, S, CONV_CH), BF16)],
        scratch_shapes=[pltpu.VMEM((N_HEADS * MAX_BLOCKS, ATT_W), F32),
                        pltpu.VMEM((1, LANES), F32),
                        pltpu.VMEM((8, CONV_CH), F32)],
        compiler_params=pltpu.CompilerParams(
            dimension_semantics=("arbitrary", "arbitrary"), vmem_limit_bytes=VMEM_LIMIT),
        name="inproj",
    )(x, g, wqkv, wtail, bfp, cw, tri, perm)


def _attn_kernel(*refs, moba, nchunk):
    if moba:
        rb_ref, qt_ref, k_ref, vt_ref, sel_ref, own_ref, corner_ref, ind_ref, o_ref, sa_ref, sb_ref = refs
        pair = pl.program_id(1)
        xrow = lax.broadcasted_iota(jnp.int32, (XROWS, CHUNK), 0)
        far_rows, near_fix = [], []
        for hh in range(2):
            fb = rb_ref[REL_BUCKETS - 1, 2 * pair + hh] * LOG2E
            terms = _split_bf16(jnp.full((XROWS, CHUNK), fb, F32), N_SPLIT)
            rows = jnp.zeros((XROWS, CHUNK), F32)
            for t in range(N_SPLIT):
                rows = jnp.where(xrow == BLOCKS_PER_CHUNK + t, terms[t].astype(F32), rows)
            far_rows.append(rows)
            near_fix.append(corner_ref[hh] - fb)
        xpad = jnp.zeros((PAIR_W - XROWS, CHUNK), BF16)
    else:
        qt_ref, k_ref, vt_ref, o_ref, sa_ref, sb_ref = refs
    bufs = (sa_ref, sb_ref)
    hidx = [jnp.minimum(pl.program_id(0), 0) + hh for hh in range(2)]

    def weights(qi):
        qt = qt_ref[0, qi]
        frow = lax.broadcasted_iota(jnp.int32, qt.shape, 0)
        zero = jnp.zeros_like(qt)
        ws = []
        for hh in range(2):
            w = jnp.where((frow >= hh * HEAD_DIM) & (frow < (hh + 1) * HEAD_DIM), qt, zero)
            if not moba:
                pick = (frow >= N_SPLIT * hh) & (frow < N_SPLIT * (hh + 1))
                w = jnp.concatenate([w, jnp.where(pick, 1.0, 0.0).astype(BF16)], axis=0)
            ws.append(w)
        return ws

    def produce(qi, ws, c, dst_ref):
        own, near = c == qi, c == qi - 1
        kc = k_ref[0, c * CHUNK:(c + 1) * CHUNK, :]
        if moba:
            kc = jnp.concatenate([kc, ind_ref[...]], axis=1)
        maxima = []
        for hh in range(2):
            w = ws[hh]
            if moba:
                rows = jnp.zeros((XROWS, CHUNK), F32) if own else far_rows[hh]
                for r in range(BLOCKS_PER_CHUNK):
                    n = hh * MAX_BLOCKS + c * BLOCKS_PER_CHUNK + r
                    sel = sel_ref[0, n:n + 1, qi * CHUNK:(qi + 1) * CHUNK]
                    rows = jnp.where(xrow == r, sel, rows)
                w = jnp.concatenate([w, rows.astype(BF16), xpad], axis=0)
            s = jnp.dot(kc, w, preferred_element_type=F32)
            if moba and own:
                s = s + own_ref[hh]
            elif moba and near:
                lo = CHUNK - CORNER
                fixed = jnp.concatenate([s[lo:, :CORNER] + near_fix[hh], s[lo:, CORNER:]], axis=1)
                s = jnp.concatenate([s[:lo], fixed], axis=0)
            elif own:
                kidx = lax.broadcasted_iota(jnp.int32, s.shape, 0)
                qidx = lax.broadcasted_iota(jnp.int32, s.shape, 1)
                s = jnp.where(kidx <= qidx, s, NEG)
            dst_ref[hidx[hh]] = s
            maxima.append(jnp.max(s, axis=0, keepdims=True))
        return tuple(maxima)

    def consume(c, src_ref, maxima, state):
        new = []
        for hh in range(2):
            m_old, acc = state[hh]
            m_new = jnp.maximum(m_old, maxima[hh])
            p = jnp.exp2((src_ref[hidx[hh]] - m_new).astype(BF16))
            vth = vt_ref[0, c, hh * HEAD_ROWS:(hh + 1) * HEAD_ROWS, :]
            pv = jnp.dot(vth, p, preferred_element_type=F32)
            new.append((m_new, jnp.exp2(m_old - m_new) * acc + pv))
        return tuple(new)

    g = 0
    ws = weights(0)
    maxima = produce(0, ws, 0, bufs[0])
    for qi in range(nchunk):
        state = tuple((jnp.full((1, CHUNK), NEG, F32), jnp.zeros((HEAD_ROWS, CHUNK), F32))
                      for _ in range(2))
        ws_next = weights(qi + 1) if qi + 1 < nchunk else None
        for c in range(qi, -1, -1):
            nxt = None
            if c > 0:
                nxt = produce(qi, ws, c - 1, bufs[(g + 1) % 2])
            elif ws_next is not None:
                nxt = produce(qi + 1, ws_next, qi + 1, bufs[(g + 1) % 2])
            state = consume(c, bufs[g % 2], maxima, state)
            maxima = nxt
            g += 1
        ws = ws_next
        ot = jnp.concatenate([acc[:HEAD_DIM] * (1.0 / acc[HEAD_DIM:HEAD_DIM + 1])
                              for (_, acc) in state], axis=0)
        o_ref[0, qi * CHUNK:(qi + 1) * CHUNK, :] = ot.T.astype(BF16)


def _block_indicator():
    ind = np.zeros((CHUNK, PAIR_W), np.float32)
    for r in range(BLOCKS_PER_CHUNK):
        ind[r * BLK:(r + 1) * BLK, r] = 1.0
    ind[:, BLOCKS_PER_CHUNK:BLOCKS_PER_CHUNK + N_SPLIT] = 1.0
    return jnp.asarray(ind, BF16)


def _attention(qt, k, vt, sel=None, tables=None, rel_bias=None):
    moba = sel is not None
    B, S, _ = k.shape
    nchunk = S // CHUNK
    kw = k.shape[-1] // N_PAIRS
    vt_off = 0 if moba else N_PAIRS
    in_specs = [pl.BlockSpec((1, nchunk, PAIR_W, CHUNK), lambda b, p: (b, 0, p, 0)),
                pl.BlockSpec((1, S, kw), lambda b, p: (b, 0, p)),
                pl.BlockSpec((1, nchunk, PAIR_ROWS, CHUNK), lambda b, p: (b, 0, p + vt_off, 0))]
    args = [qt, k, vt]
    if moba:
        in_specs = [pl.BlockSpec(memory_space=pltpu.SMEM)] + in_specs
        in_specs += [pl.BlockSpec((1, 2 * MAX_BLOCKS, S), lambda b, p: (b, p, 0)),
                     pl.BlockSpec((2, CHUNK, CHUNK), lambda b, p: (p, 0, 0)),
                     pl.BlockSpec((2, CORNER, CORNER), lambda b, p: (p, 0, 0)),
                     pl.BlockSpec((CHUNK, PAIR_W), lambda b, p: (0, 0))]
        args = [rel_bias] + args + [sel, *tables, _block_indicator()]
    return pl.pallas_call(
        functools.partial(_attn_kernel, moba=moba, nchunk=nchunk),
        grid=(B, N_PAIRS),
        in_specs=in_specs,
        out_specs=pl.BlockSpec((1, S, PAIR_W), lambda b, p: (b, 0, p)),
        out_shape=jax.ShapeDtypeStruct((B, S, ATT_W), BF16),
        scratch_shapes=[pltpu.VMEM((2, CHUNK, CHUNK), F32), pltpu.VMEM((2, CHUNK, CHUNK), F32)],
        compiler_params=pltpu.CompilerParams(
            dimension_semantics=("arbitrary", "arbitrary"), vmem_limit_bytes=VMEM_LIMIT),
        name="moba_attn" if moba else "fox_attn",
    )(*args)


def _out_ffn_kernel(moba_ref, fox_ref, conv_ref, x_ref, wo_ref, wg_ref, wu_ref, wd_ref,
                    gpm_ref, gpf_ref, gqf_ref, o_ref):
    mix_in = jnp.concatenate([moba_ref[...], fox_ref[...], conv_ref[...]], axis=1)
    mixed = jnp.dot(mix_in, wo_ref[...], preferred_element_type=F32)
    x1 = x_ref[...] + _rms(mixed, gpm_ref[...])
    h2 = _rms(x1, gpf_ref[...]).astype(BF16)
    ff = jnp.zeros(x1.shape, F32)
    lo = 0
    for width in FFN_CHUNKS:
        gate = jnp.dot(h2, wg_ref[:, lo:lo + width], preferred_element_type=F32)
        up = jnp.dot(h2, wu_ref[:, lo:lo + width], preferred_element_type=F32)
        act = (gate * (1.0 / (1.0 + jnp.exp(-gate))) * up).astype(BF16)
        ff = ff + jnp.dot(act, wd_ref[lo:lo + width, :], preferred_element_type=F32)
        lo += width
    o_ref[...] = x1 + _rms(ff, gqf_ref[...])


def _out_ffn(moba, fox, conv, x, wo, wg, wu, wd, gpm, gpf, gqf):
    M, D = x.shape
    tm = FFN_TILE
    row = lambda w: pl.BlockSpec((tm, w), lambda m: (m, 0))
    const = lambda shape: pl.BlockSpec(shape, lambda m: (0, 0), pipeline_mode=pl.Buffered(1))
    return pl.pallas_call(
        _out_ffn_kernel,
        grid=(M // tm,),
        in_specs=[row(ATT_W), row(ATT_W), row(CONV_CH), row(D),
                  const(wo.shape), const(wg.shape), const(wu.shape), const(wd.shape),
                  const((1, D)), const((1, D)), const((1, D))],
        out_specs=row(D),
        out_shape=jax.ShapeDtypeStruct((M, D), F32),
        compiler_params=pltpu.CompilerParams(
            dimension_semantics=("arbitrary",), vmem_limit_bytes=VMEM_LIMIT),
        name="out_ffn",
    )(moba, fox, conv, x, wo, wg, wu, wd, gpm, gpf, gqf)


def _cumsum_constants(tm):
    tri = np.tril(np.ones((tm, tm), np.float32))
    perm = np.zeros((N_SPLIT, LANES, ATT_W), np.float32)
    for s in range(N_SPLIT):
        for h in range(N_HEADS):
            perm[s, h, PAIR_W * (h // 2) + N_SPLIT * (h % 2) + s] = 1.0
    return jnp.asarray(tri, BF16), jnp.asarray(perm, BF16)


def kernel(x, w_in, b_f, conv_w, w_out, rel_bias, g_pre_mix, g_post_mix, g_pre_ffn, g_post_ffn,
           w_gate, w_up, w_down):
    B, S, D = x.shape
    depth = w_in.shape[0]
    assert D == D_MODEL and S % CHUNK == 0 and S // BLK <= MAX_BLOCKS
    assert (B * S) % FFN_TILE == 0

    rel_bias = rel_bias.astype(F32)
    tables = _bias_tiles(rel_bias)
    w_in_t = jnp.swapaxes(w_in, 1, 2)
    tri, perm = _cumsum_constants(CHUNK)
    a = ATT_W
    for l in range(depth):
        wl = w_in_t[l]
        wqkv = wl[0:6 * a].astype(BF16)
        wtail = jnp.concatenate([wl[6 * a:6 * a + N_HEADS], jnp.zeros((LANES - N_HEADS, D), wl.dtype),
                                 wl[6 * a + N_HEADS:]], axis=0).astype(BF16)
        bfp = jnp.pad(b_f[l].astype(F32), (0, LANES - N_HEADS)).reshape(1, LANES)
        row = lambda g: g[l].astype(F32).reshape(1, D)

        mq, mk, fq, fkp, vt, sel, conv = _inproj(
            x, row(g_pre_mix), wqkv, wtail, bfp, conv_w[l].astype(F32), tri, perm)
        moba = _attention(mq, mk, vt, sel, tables, rel_bias)
        fox = _attention(fq, fkp, vt)
        x = _out_ffn(moba.reshape(B * S, a), fox.reshape(B * S, a), conv.reshape(B * S, CONV_CH),
                     x.reshape(B * S, D), w_out[l].astype(BF16), w_gate[l].astype(BF16),
                     w_up[l].astype(BF16), w_down[l].astype(BF16),
                     row(g_post_mix), row(g_pre_ffn), row(g_post_ffn)).reshape(B, S, D)
    return x
```

```python
import functools
import math

import jax
import jax.numpy as jnp
import numpy as np
from jax import lax
from jax.experimental import pallas as pl
from jax.experimental.pallas import tpu as pltpu

D_MODEL = 1024
HEAD_DIM = 64
N_HEADS = 6
ATT_W = N_HEADS * HEAD_DIM
PAIR_W = 2 * HEAD_DIM
N_PAIRS = N_HEADS // 2
CONV_CH = 256
CONV_WIDTH = 3
BLK = 256
BLK_SHIFT = BLK.bit_length() - 1
CHUNK = 512
BLOCKS_PER_CHUNK = CHUNK // BLK
MAX_BLOCKS = 16
TOPK = 3
REL_BUCKETS = 32
REL_MAX_DIST = 128
D_FF = 2816
RMS_EPS = 1e-6
NEG = -1e30
LANES = 128
N_SPLIT = 3
LOG2E = math.log2(math.e)
CORNER = REL_MAX_DIST
ONES_ROWS = 16
HEAD_ROWS = HEAD_DIM + ONES_ROWS
PAIR_ROWS = 2 * HEAD_ROWS
XROWS = 16
NEAR_CHUNKS = 3
UNDERFLOW_BITS = 152.0

F32 = jnp.float32
BF16 = jnp.bfloat16
NT_DIMS = (((1,), (1,)), ((), ()))

FFN_TILE = 512
FFN_CHUNKS = (1536, 1280)
assert sum(FFN_CHUNKS) == D_FF
VMEM_LIMIT = 50 * 1024 * 1024


def _split_bf16(v, n):
    terms = []
    for _ in range(n):
        t = v.astype(BF16)
        terms.append(t)
        v = v - t.astype(F32)
    return terms


def _rms(v, g):
    return v * lax.rsqrt(jnp.mean(v * v, axis=-1, keepdims=True) + RMS_EPS) * g


def _bucket_tiles():
    max_exact = REL_BUCKETS // 2

    def bucket(dist):
        n = np.maximum(dist, 0)
        nf = np.maximum(n, 1).astype(np.float32)
        large = max_exact + (np.log(nf / np.float32(max_exact))
                             / np.float32(math.log(REL_MAX_DIST / max_exact))
                             * np.float32(REL_BUCKETS - max_exact)).astype(np.int32)
        large = np.minimum(large, REL_BUCKETS - 1)
        return np.where(n < max_exact, n, large).astype(np.int32)

    key = np.arange(CHUNK, dtype=np.int32)[:, None]
    qry = np.arange(CHUNK, dtype=np.int32)[None, :]
    d0 = qry - key
    own = np.where(d0 >= 0, bucket(d0), -1).astype(np.int32)
    prev = bucket(d0 + CHUNK)
    corner = prev[CHUNK - CORNER:, :CORNER]
    outside = prev.copy()
    outside[CHUNK - CORNER:, :CORNER] = REL_BUCKETS - 1
    assert (outside == REL_BUCKETS - 1).all()
    return own, np.ascontiguousarray(corner)


def _bias_tile_kernel(rb_ref, own_bm_ref, corner_bm_ref, own_ref, corner_ref):
    h = pl.program_id(0)
    for bm_ref, out_ref in ((own_bm_ref, own_ref), (corner_bm_ref, corner_ref)):
        bm = bm_ref[...]
        t = jnp.full(bm.shape, NEG, F32)
        for b in range(REL_BUCKETS):
            t = jnp.where(bm == b, rb_ref[b, h] * LOG2E, t)
        out_ref[0] = t


def _bias_tiles(rel_bias):
    own_bm, corner_bm = _bucket_tiles()
    return pl.pallas_call(
        _bias_tile_kernel,
        grid=(N_HEADS,),
        in_specs=[pl.BlockSpec(memory_space=pltpu.SMEM),
                  pl.BlockSpec((CHUNK, CHUNK), lambda h: (0, 0)),
                  pl.BlockSpec((CORNER, CORNER), lambda h: (0, 0))],
        out_specs=[pl.BlockSpec((1, CHUNK, CHUNK), lambda h: (h, 0, 0)),
                   pl.BlockSpec((1, CORNER, CORNER), lambda h: (h, 0, 0))],
        out_shape=[jax.ShapeDtypeStruct((N_HEADS, CHUNK, CHUNK), F32),
                   jax.ShapeDtypeStruct((N_HEADS, CORNER, CORNER), F32)],
        name="bias_tiles",
    )(rel_bias, jnp.asarray(own_bm), jnp.asarray(corner_bm))


def _inproj_kernel(x_ref, g_ref, wqkv_ref, wtail_ref, bf_ref, cw_ref, tri_ref, perm_ref,
                   mq_ref, mk_ref, fq_ref, fkp_ref, vt_ref, sel_ref, conv_ref,
                   km_ref, cum_ref, uc_ref):
    tm = CHUNK
    st = pl.program_id(1)

    @pl.when(st == 0)
    def _():
        km_ref[...] = jnp.zeros_like(km_ref)
        cum_ref[...] = jnp.zeros_like(cum_ref)
        uc_ref[...] = jnp.zeros_like(uc_ref)

    hn = _rms(x_ref[0], g_ref[...]).astype(BF16)

    qkv = lax.dot_general(hn, wqkv_ref[...], NT_DIMS, preferred_element_type=F32)
    tail = lax.dot_general(hn, wtail_ref[...], NT_DIMS, preferred_element_type=F32)
    mq = qkv[:, 0 * ATT_W:1 * ATT_W]
    mk = qkv[:, 1 * ATT_W:2 * ATT_W]
    mv = qkv[:, 2 * ATT_W:3 * ATT_W]
    fq = qkv[:, 3 * ATT_W:4 * ATT_W]
    fk = qkv[:, 4 * ATT_W:5 * ATT_W]
    fv = qkv[:, 5 * ATT_W:6 * ATT_W]
    scale = HEAD_DIM ** -0.5 * LOG2E
    mq_ref[0, 0] = (mq * scale).T.astype(BF16)
    mk_ref[0] = mk.astype(BF16)
    fq_ref[0, 0] = (fq * scale).T.astype(BF16)
    ones = jnp.ones((ONES_ROWS, tm), BF16)
    for kind, v in enumerate((mv, fv)):
        vt = v.T.astype(BF16)
        for h in range(N_HEADS):
            base = (kind * N_HEADS + h) * HEAD_ROWS
            vt_ref[0, 0, base:base + HEAD_DIM, :] = vt[h * HEAD_DIM:(h + 1) * HEAD_DIM]
            vt_ref[0, 0, base + HEAD_DIM:base + HEAD_ROWS, :] = ones

    rows = lax.broadcasted_iota(jnp.int32, km_ref.shape, 0)
    cols = lax.broadcasted_iota(jnp.int32, km_ref.shape, 1)
    same_head = (rows >> 4) == (cols >> 6)
    km = km_ref[...]
    for r in range(BLOCKS_PER_CHUNK):
        kmean = jnp.mean(mk[r * BLK:(r + 1) * BLK], axis=0, keepdims=True)
        n = st * BLOCKS_PER_CHUNK + r
        km = jnp.where(same_head & ((rows & (MAX_BLOCKS - 1)) == n), kmean, km)
    km_ref[...] = km

    km_hi, km_lo = _split_bf16(km, 2)
    q_hi, q_lo = _split_bf16(mq, 2)
    gate = (lax.dot_general(km_hi, q_hi, NT_DIMS, preferred_element_type=F32)
            + lax.dot_general(km_hi, q_lo, NT_DIMS, preferred_element_type=F32)
            + lax.dot_general(km_lo, q_hi, NT_DIMS, preferred_element_type=F32))

    nidx = lax.broadcasted_iota(jnp.int32, (MAX_BLOCKS, tm), 0)
    own = st * BLOCKS_PER_CHUNK + (lax.broadcasted_iota(jnp.int32, (MAX_BLOCKS, tm), 1) >> BLK_SHIFT)
    valid = nidx < own
    for h in range(N_HEADS):
        gh = jnp.where(valid, gate[h * MAX_BLOCKS:(h + 1) * MAX_BLOCKS], -jnp.inf)
        rank = jnp.zeros((MAX_BLOCKS, tm), jnp.int32)
        for m in range(MAX_BLOCKS):
            gm = gh[m:m + 1, :]
            gt = (gm > gh).astype(jnp.int32)
            ge = (gm >= gh).astype(jnp.int32)
            rank = rank + jnp.where(nidx > m, ge, gt)
        keep = ((rank < TOPK) & valid) | (nidx == own)
        sel_ref[0, h * MAX_BLOCKS:(h + 1) * MAX_BLOCKS, :] = jnp.where(keep, 0.0, NEG)

    fl = tail[:, 0:LANES] + bf_ref[...]
    lf = jnp.minimum(fl, 0.0) - jnp.log1p(jnp.exp(-jnp.abs(fl)))
    tri = tri_ref[...]
    cum = cum_ref[...]
    for t in _split_bf16(lf, N_SPLIT):
        cum = cum + jnp.dot(tri, t, preferred_element_type=F32)
    cum_ref[...] = cum[tm - 1:tm, :]
    kb = jnp.zeros((tm, ATT_W), F32)
    for s, t in enumerate(_split_bf16(cum * -LOG2E, N_SPLIT)):
        kb = kb + jnp.dot(t, perm_ref[s], preferred_element_type=F32)
    kb = kb.astype(BF16)
    fkb = fk.astype(BF16)
    for p in range(N_PAIRS):
        fkp_ref[0, :, 2 * p * PAIR_W:(2 * p + 1) * PAIR_W] = fkb[:, p * PAIR_W:(p + 1) * PAIR_W]
        fkp_ref[0, :, (2 * p + 1) * PAIR_W:(2 * p + 2) * PAIR_W] = kb[:, p * PAIR_W:(p + 1) * PAIR_W]

    cv = tail[:, LANES:]
    cvb = cv[:, 0:CONV_CH]
    u = cv[:, CONV_CH:2 * CONV_CH] * cv[:, 2 * CONV_CH:3 * CONV_CH]
    uc = uc_ref[...]
    prev1 = uc[7:8]
    prev2 = uc[6:7]
    ridx = lax.broadcasted_iota(jnp.int32, u.shape, 0)
    u1 = jnp.where(ridx == 0, prev1, pltpu.roll(u, 1, 0))
    u2 = jnp.where(ridx == 0, prev2, jnp.where(ridx == 1, prev1, pltpu.roll(u, 2, 0)))
    cw = cw_ref[...]
    y = cw[0:1] * u2 + cw[1:2] * u1 + cw[2:3] * u
    conv_ref[0] = (cvb * y).astype(BF16)
    uc_ref[...] = u[tm - 8:tm]


def _inproj(x, g, wqkv, wtail, bfp, cw, tri, perm):
    B, S, D = x.shape
    tm = CHUNK
    const = lambda shape: pl.BlockSpec(shape, lambda b, s: (0,) * len(shape))
    row = lambda w: pl.BlockSpec((1, tm, w), lambda b, s: (b, s, 0))
    colmajor = lambda w: pl.BlockSpec((1, 1, w, tm), lambda b, s: (b, s, 0, 0))
    return pl.pallas_call(
        _inproj_kernel,
        grid=(B, S // tm),
        in_specs=[row(D), const((1, D)), const(wqkv.shape), const(wtail.shape),
                  const(bfp.shape), const(cw.shape), const(tri.shape), const(perm.shape)],
        out_specs=[colmajor(ATT_W), row(ATT_W), colmajor(ATT_W), row(2 * ATT_W),
                   pl.BlockSpec((1, 1, 2 * N_HEADS * HEAD_ROWS, tm), lambda b, s: (b, s, 0, 0)),
                   pl.BlockSpec((1, N_HEADS * MAX_BLOCKS, tm), lambda b, s: (b, 0, s)),
                   row(CONV_CH)],
        out_shape=[jax.ShapeDtypeStruct((B, S // tm, ATT_W, tm), BF16),
                   jax.ShapeDtypeStruct((B, S, ATT_W), BF16),
                   jax.ShapeDtypeStruct((B, S // tm, ATT_W, tm), BF16),
                   jax.ShapeDtypeStruct((B, S, 2 * ATT_W), BF16),
                   jax.ShapeDtypeStruct((B, S // tm, 2 * N_HEADS * HEAD_ROWS, tm), BF16),
                   jax.ShapeDtypeStruct((B, N_HEADS * MAX_BLOCKS, S), F32),
                   jax.ShapeDtypeStruct((B, S, CONV_CH), BF16)],
        scratch_shapes=[pltpu.VMEM((N_HEADS * MAX_BLOCKS, ATT_W), F32),
                        pltpu.VMEM((1, LANES), F32),
                        pltpu.VMEM((8, CONV_CH), F32)],
        compiler_params=pltpu.CompilerParams(
            dimension_semantics=("arbitrary", "arbitrary"), vmem_limit_bytes=VMEM_LIMIT),
        name="inproj",
    )(x, g, wqkv, wtail, bfp, cw, tri, perm)


def _attn_kernel(*refs, moba, nchunk):
    if moba:
        rb_ref, qt_ref, k_ref, vt_ref, sel_ref, own_ref, corner_ref, ind_ref, o_ref, *bufs = refs
        pair = pl.program_id(1)
        xrow = lax.broadcasted_iota(jnp.int32, (XROWS, CHUNK), 0)
        far_rows, near_fix = [], []
        for hh in range(2):
            fb = rb_ref[REL_BUCKETS - 1, 2 * pair + hh] * LOG2E
            terms = _split_bf16(jnp.full((XROWS, CHUNK), fb, F32), N_SPLIT)
            rows = jnp.zeros((XROWS, CHUNK), F32)
            for t in range(N_SPLIT):
                rows = jnp.where(xrow == BLOCKS_PER_CHUNK + t, terms[t].astype(F32), rows)
            far_rows.append(rows)
            near_fix.append(corner_ref[hh] - fb)
        xpad = jnp.zeros((PAIR_W - XROWS, CHUNK), BF16)
    else:
        qt_ref, k_ref, vt_ref, o_ref, *bufs = refs
    hidx = [jnp.minimum(pl.program_id(0), 0) + hh for hh in range(2)]

    def weights(qi):
        qt = qt_ref[0, qi]
        frow = lax.broadcasted_iota(jnp.int32, qt.shape, 0)
        zero = jnp.zeros_like(qt)
        ws = []
        for hh in range(2):
            w = jnp.where((frow >= hh * HEAD_DIM) & (frow < (hh + 1) * HEAD_DIM), qt, zero)
            if not moba:
                pick = (frow >= N_SPLIT * hh) & (frow < N_SPLIT * (hh + 1))
                w = jnp.concatenate([w, jnp.where(pick, 1.0, 0.0).astype(BF16)], axis=0)
            ws.append(w)
        return ws

    def query_norms(qi):
        q2 = qt_ref[0, qi].astype(F32) ** 2
        return [jnp.sqrt(jnp.sum(q2[hh * HEAD_DIM:(hh + 1) * HEAD_DIM], axis=0, keepdims=True))
                for hh in range(2)]

    def produce(qi, ws, c, dst_ref):
        own, near = c == qi, c == qi - 1
        kc = k_ref[0, c * CHUNK:(c + 1) * CHUNK, :]
        if moba:
            kc = jnp.concatenate([kc, ind_ref[...]], axis=1)
        maxima = []
        for hh in range(2):
            w = ws[hh]
            if moba:
                rows = jnp.zeros((XROWS, CHUNK), F32) if own else far_rows[hh]
                for r in range(BLOCKS_PER_CHUNK):
                    n = hh * MAX_BLOCKS + c * BLOCKS_PER_CHUNK + r
                    sel = sel_ref[0, n:n + 1, qi * CHUNK:(qi + 1) * CHUNK]
                    rows = jnp.where(xrow == r, sel, rows)
                w = jnp.concatenate([w, rows.astype(BF16), xpad], axis=0)
            s = jnp.dot(kc, w, preferred_element_type=F32)
            if moba and own:
                s = s + own_ref[hh]
            elif moba and near:
                lo = CHUNK - CORNER
                fixed = jnp.concatenate([s[lo:, :CORNER] + near_fix[hh], s[lo:, CORNER:]], axis=1)
                s = jnp.concatenate([s[:lo], fixed], axis=0)
            elif own:
                kidx = lax.broadcasted_iota(jnp.int32, s.shape, 0)
                qidx = lax.broadcasted_iota(jnp.int32, s.shape, 1)
                s = jnp.where(kidx <= qidx, s, NEG)
            dst_ref[hidx[hh]] = s
            maxima.append(jnp.max(s, axis=0, keepdims=True))
        return tuple(maxima)

    def consume(c, src_ref, maxima, state):
        new = []
        for hh in range(2):
            m_old, acc = state[hh]
            m_new = jnp.maximum(m_old, maxima[hh])
            p = jnp.exp2(src_ref[hidx[hh]] - m_new).astype(BF16)
            vth = vt_ref[0, c, hh * HEAD_ROWS:(hh + 1) * HEAD_ROWS, :]
            pv = jnp.dot(vth, p, preferred_element_type=F32)
            new.append((m_new, jnp.exp2(m_old - m_new) * acc + pv))
        return tuple(new)

    def fresh_state():
        return tuple((jnp.full((1, CHUNK), NEG, F32), jnp.zeros((HEAD_ROWS, CHUNK), F32))
                     for _ in range(2))

    def finish(qi, state):
        ot = jnp.concatenate([acc[:HEAD_DIM] * (1.0 / acc[HEAD_DIM:HEAD_DIM + 1])
                              for (_, acc) in state], axis=0)
        o_ref[0, qi * CHUNK:(qi + 1) * CHUNK, :] = ot.T.astype(BF16)

    if moba:
        g = 0
        ws = weights(0)
        maxima = produce(0, ws, 0, bufs[0])
        for qi in range(nchunk):
            state = fresh_state()
            ws_next = weights(qi + 1) if qi + 1 < nchunk else None
            for c in range(qi, -1, -1):
                nxt = None
                if c > 0:
                    nxt = produce(qi, ws, c - 1, bufs[(g + 1) % 2])
                elif ws_next is not None:
                    nxt = produce(qi + 1, ws_next, qi + 1, bufs[(g + 1) % 2])
                state = consume(c, bufs[g % 2], maxima, state)
                maxima = nxt
                g += 1
            ws = ws_next
            finish(qi, state)
        return

    own_bufs, far_bufs = bufs[0:2], bufs[2:4]
    kall = k_ref[0][:, 0:PAIR_W].astype(F32) ** 2
    lane = lax.broadcasted_iota(jnp.int32, (1, PAIR_W), 1)
    key_norm = [jnp.sqrt(jnp.max(jnp.sum(jnp.where((lane >= hh * HEAD_DIM) & (lane < (hh + 1) * HEAD_DIM),
                                                      kall, 0.0), axis=1, keepdims=True)))
                for hh in range(2)]

    def all_older_underflow(qi, state, first_far):
        last_key = (first_far + 1) * CHUNK
        decay = k_ref[0, last_key - 16:last_key, PAIR_W:2 * PAIR_W].astype(F32)
        qn = query_norms(qi)
        worst = None
        for hh in range(2):
            mine = (lane >= N_SPLIT * hh) & (lane < N_SPLIT * (hh + 1))
            newest = jnp.max(jnp.sum(jnp.where(mine, decay, 0.0), axis=1, keepdims=True))
            gap = jnp.max(qn[hh] * key_norm[hh] + newest - state[hh][0])
            worst = gap if worst is None else jnp.maximum(worst, gap)
        return worst < -UNDERFLOW_BITS

    ws = weights(0)
    maxima = produce(0, ws, 0, own_bufs[0])
    for qi in range(nchunk):
        state = fresh_state()
        ws_next = weights(qi + 1) if qi + 1 < nchunk else None
        n_near = min(qi + 1, NEAR_CHUNKS)
        src = own_bufs[qi % 2]
        next_own = None
        for j in range(n_near):
            c = qi - j
            if j < n_near - 1:
                dst = far_bufs[j % 2]
                nxt = produce(qi, ws, c - 1, dst)
            elif ws_next is not None:
                next_own = produce(qi + 1, ws_next, qi + 1, own_bufs[(qi + 1) % 2])
            state = consume(c, src, maxima, state)
            if j < n_near - 1:
                src, maxima = dst, nxt
        first_far = qi - NEAR_CHUNKS
        if first_far >= 0:
            def far_stages(st, qi=qi, ws=ws, first_far=first_far):
                mx = produce(qi, ws, first_far, far_bufs[0])
                for i, c in enumerate(range(first_far, -1, -1)):
                    nxt = produce(qi, ws, c - 1, far_bufs[(i + 1) % 2]) if c > 0 else None
                    st = consume(c, far_bufs[i % 2], mx, st)
                    mx = nxt
                return st
            state = lax.cond(all_older_underflow(qi, state, first_far), lambda st: st, far_stages, state)
        finish(qi, state)
        ws, maxima = ws_next, next_own


def _block_indicator():
    ind = np.zeros((CHUNK, PAIR_W), np.float32)
    for r in range(BLOCKS_PER_CHUNK):
        ind[r * BLK:(r + 1) * BLK, r] = 1.0
    ind[:, BLOCKS_PER_CHUNK:BLOCKS_PER_CHUNK + N_SPLIT] = 1.0
    return jnp.asarray(ind, BF16)


def _attention(qt, k, vt, sel=None, tables=None, rel_bias=None):
    moba = sel is not None
    B, S, _ = k.shape
    nchunk = S // CHUNK
    kw = k.shape[-1] // N_PAIRS
    vt_off = 0 if moba else N_PAIRS
    in_specs = [pl.BlockSpec((1, nchunk, PAIR_W, CHUNK), lambda b, p: (b, 0, p, 0)),
                pl.BlockSpec((1, S, kw), lambda b, p: (b, 0, p)),
                pl.BlockSpec((1, nchunk, PAIR_ROWS, CHUNK), lambda b, p: (b, 0, p + vt_off, 0))]
    args = [qt, k, vt]
    if moba:
        in_specs = [pl.BlockSpec(memory_space=pltpu.SMEM)] + in_specs
        in_specs += [pl.BlockSpec((1, 2 * MAX_BLOCKS, S), lambda b, p: (b, p, 0)),
                     pl.BlockSpec((2, CHUNK, CHUNK), lambda b, p: (p, 0, 0)),
                     pl.BlockSpec((2, CORNER, CORNER), lambda b, p: (p, 0, 0)),
                     pl.BlockSpec((CHUNK, PAIR_W), lambda b, p: (0, 0))]
        args = [rel_bias] + args + [sel, *tables, _block_indicator()]
    return pl.pallas_call(
        functools.partial(_attn_kernel, moba=moba, nchunk=nchunk),
        grid=(B, N_PAIRS),
        in_specs=in_specs,
        out_specs=pl.BlockSpec((1, S, PAIR_W), lambda b, p: (b, 0, p)),
        out_shape=jax.ShapeDtypeStruct((B, S, ATT_W), BF16),
        scratch_shapes=[pltpu.VMEM((2, CHUNK, CHUNK), F32)] * (2 if moba else 4),
        compiler_params=pltpu.CompilerParams(
            dimension_semantics=("arbitrary", "arbitrary"), vmem_limit_bytes=VMEM_LIMIT),
        name="moba_attn" if moba else "fox_attn",
    )(*args)


def _out_ffn_kernel(moba_ref, fox_ref, conv_ref, x_ref, wo_ref, wg_ref, wu_ref, wd_ref,
                    gpm_ref, gpf_ref, gqf_ref, o_ref):
    mix_in = jnp.concatenate([moba_ref[...], fox_ref[...], conv_ref[...]], axis=1)
    mixed = jnp.dot(mix_in, wo_ref[...], preferred_element_type=F32)
    x1 = x_ref[...] + _rms(mixed, gpm_ref[...])
    h2 = _rms(x1, gpf_ref[...]).astype(BF16)
    ff = jnp.zeros(x1.shape, F32)
    lo = 0
    for width in FFN_CHUNKS:
        gate = jnp.dot(h2, wg_ref[:, lo:lo + width], preferred_element_type=F32)
        up = jnp.dot(h2, wu_ref[:, lo:lo + width], preferred_element_type=F32)
        act = (gate * (1.0 / (1.0 + jnp.exp(-gate))) * up).astype(BF16)
        ff = ff + jnp.dot(act, wd_ref[lo:lo + width, :], preferred_element_type=F32)
        lo += width
    o_ref[...] = x1 + _rms(ff, gqf_ref[...])


def _out_ffn(moba, fox, conv, x, wo, wg, wu, wd, gpm, gpf, gqf):
    M, D = x.shape
    tm = FFN_TILE
    row = lambda w: pl.BlockSpec((tm, w), lambda m: (m, 0))
    const = lambda shape: pl.BlockSpec(shape, lambda m: (0, 0), pipeline_mode=pl.Buffered(1))
    return pl.pallas_call(
        _out_ffn_kernel,
        grid=(M // tm,),
        in_specs=[row(ATT_W), row(ATT_W), row(CONV_CH), row(D),
                  const(wo.shape), const(wg.shape), const(wu.shape), const(wd.shape),
                  const((1, D)), const((1, D)), const((1, D))],
        out_specs=row(D),
        out_shape=jax.ShapeDtypeStruct((M, D), F32),
        compiler_params=pltpu.CompilerParams(
            dimension_semantics=("arbitrary",), vmem_limit_bytes=VMEM_LIMIT),
        name="out_ffn",
    )(moba, fox, conv, x, wo, wg, wu, wd, gpm, gpf, gqf)


def _cumsum_constants(tm):
    tri = np.tril(np.ones((tm, tm), np.float32))
    perm = np.zeros((N_SPLIT, LANES, ATT_W), np.float32)
    for s in range(N_SPLIT):
        for h in range(N_HEADS):
            perm[s, h, PAIR_W * (h // 2) + N_SPLIT * (h % 2) + s] = 1.0
    return jnp.asarray(tri, BF16), jnp.asarray(perm, BF16)


def kernel(x, w_in, b_f, conv_w, w_out, rel_bias, g_pre_mix, g_post_mix, g_pre_ffn, g_post_ffn,
           w_gate, w_up, w_down):
    B, S, D = x.shape
    depth = w_in.shape[0]
    assert D == D_MODEL and S % CHUNK == 0 and S // BLK <= MAX_BLOCKS
    assert (B * S) % FFN_TILE == 0

    rel_bias = rel_bias.astype(F32)
    tables = _bias_tiles(rel_bias)
    w_in_t = jnp.swapaxes(w_in, 1, 2)
    tri, perm = _cumsum_constants(CHUNK)
    a = ATT_W
    for l in range(depth):
        wl = w_in_t[l]
        wqkv = wl[0:6 * a].astype(BF16)
        wtail = jnp.concatenate([wl[6 * a:6 * a + N_HEADS], jnp.zeros((LANES - N_HEADS, D), wl.dtype),
                                 wl[6 * a + N_HEADS:]], axis=0).astype(BF16)
        bfp = jnp.pad(b_f[l].astype(F32), (0, LANES - N_HEADS)).reshape(1, LANES)
        row = lambda g: g[l].astype(F32).reshape(1, D)

        mq, mk, fq, fkp, vt, sel, conv = _inproj(
            x, row(g_pre_mix), wqkv, wtail, bfp, conv_w[l].astype(F32), tri, perm)
        moba = _attention(mq, mk, vt, sel, tables, rel_bias)
        fox = _attention(fq, fkp, vt)
        x = _out_ffn(moba.reshape(B * S, a), fox.reshape(B * S, a), conv.reshape(B * S, CONV_CH),
                     x.reshape(B * S, D), w_out[l].astype(BF16), w_gate[l].astype(BF16),
                     w_up[l].astype(BF16), w_down[l].astype(BF16),
                     row(g_post_mix), row(g_pre_ffn), row(g_post_ffn)).reshape(B, S, D)
    return x
```

```python
import functools
import math

import jax
import jax.numpy as jnp
import numpy as np
from jax import lax
from jax.experimental import pallas as pl
from jax.experimental.pallas import tpu as pltpu

D_MODEL = 1024
HEAD_DIM = 64
N_HEADS = 6
ATT_W = N_HEADS * HEAD_DIM
PAIR_W = 2 * HEAD_DIM
N_PAIRS = N_HEADS // 2
CONV_CH = 256
CONV_WIDTH = 3
BLK = 256
BLK_SHIFT = BLK.bit_length() - 1
CHUNK = 512
BLOCKS_PER_CHUNK = CHUNK // BLK
MAX_BLOCKS = 16
TOPK = 3
REL_BUCKETS = 32
REL_MAX_DIST = 128
D_FF = 2816
RMS_EPS = 1e-6
NEG = -1e30
LANES = 128
N_SPLIT = 3
LOG2E = math.log2(math.e)
CORNER = REL_MAX_DIST
ONES_ROWS = 16
HEAD_ROWS = HEAD_DIM + ONES_ROWS
PAIR_ROWS = 2 * HEAD_ROWS
XROWS = 16
NEAR_CHUNKS = 3
UNDERFLOW_BITS = 152.0

F32 = jnp.float32
BF16 = jnp.bfloat16
NT_DIMS = (((1,), (1,)), ((), ()))

INPROJ_TILES = 2
FFN_TILE = 512
FFN_CHUNKS = (1536, 1280)
assert sum(FFN_CHUNKS) == D_FF
VMEM_LIMIT = 50 * 1024 * 1024


def _split_bf16(v, n):
    terms = []
    for _ in range(n):
        t = v.astype(BF16)
        terms.append(t)
        v = v - t.astype(F32)
    return terms


def _rms(v, g):
    return v * lax.rsqrt(jnp.mean(v * v, axis=-1, keepdims=True) + RMS_EPS) * g


def _bucket_tiles():
    max_exact = REL_BUCKETS // 2

    def bucket(dist):
        n = np.maximum(dist, 0)
        nf = np.maximum(n, 1).astype(np.float32)
        large = max_exact + (np.log(nf / np.float32(max_exact))
                             / np.float32(math.log(REL_MAX_DIST / max_exact))
                             * np.float32(REL_BUCKETS - max_exact)).astype(np.int32)
        large = np.minimum(large, REL_BUCKETS - 1)
        return np.where(n < max_exact, n, large).astype(np.int32)

    key = np.arange(CHUNK, dtype=np.int32)[:, None]
    qry = np.arange(CHUNK, dtype=np.int32)[None, :]
    d0 = qry - key
    own = np.where(d0 >= 0, bucket(d0), -1).astype(np.int32)
    prev = bucket(d0 + CHUNK)
    corner = prev[CHUNK - CORNER:, :CORNER]
    outside = prev.copy()
    outside[CHUNK - CORNER:, :CORNER] = REL_BUCKETS - 1
    assert (outside == REL_BUCKETS - 1).all()
    return own, np.ascontiguousarray(corner)


def _bias_tile_kernel(rb_ref, own_bm_ref, corner_bm_ref, own_ref, corner_ref):
    h = pl.program_id(0)
    for bm_ref, out_ref in ((own_bm_ref, own_ref), (corner_bm_ref, corner_ref)):
        bm = bm_ref[...]
        t = jnp.full(bm.shape, NEG, F32)
        for b in range(REL_BUCKETS):
            t = jnp.where(bm == b, rb_ref[b, h] * LOG2E, t)
        out_ref[0] = t


def _bias_tiles(rel_bias):
    own_bm, corner_bm = _bucket_tiles()
    return pl.pallas_call(
        _bias_tile_kernel,
        grid=(N_HEADS,),
        in_specs=[pl.BlockSpec(memory_space=pltpu.SMEM),
                  pl.BlockSpec((CHUNK, CHUNK), lambda h: (0, 0)),
                  pl.BlockSpec((CORNER, CORNER), lambda h: (0, 0))],
        out_specs=[pl.BlockSpec((1, CHUNK, CHUNK), lambda h: (h, 0, 0)),
                   pl.BlockSpec((1, CORNER, CORNER), lambda h: (h, 0, 0))],
        out_shape=[jax.ShapeDtypeStruct((N_HEADS, CHUNK, CHUNK), F32),
                   jax.ShapeDtypeStruct((N_HEADS, CORNER, CORNER), F32)],
        name="bias_tiles",
    )(rel_bias, jnp.asarray(own_bm), jnp.asarray(corner_bm))


def _inproj_kernel(*refs):
    km_ref, cum_ref, uc_ref = refs[-3:]

    @pl.when(pl.program_id(1) == 0)
    def _():
        km_ref[...] = jnp.zeros_like(km_ref)
        cum_ref[...] = jnp.zeros_like(cum_ref)
        uc_ref[...] = jnp.zeros_like(uc_ref)

    for sub in range(INPROJ_TILES):
        _inproj_tile(sub, *refs)


def _inproj_tile(sub, x_ref, g_ref, wqkv_ref, wtail_ref, bf_ref, cw_ref, tri_ref, perm_ref,
                 mq_ref, mk_ref, fq_ref, fkp_ref, vt_ref, sel_ref, conv_ref,
                 km_ref, cum_ref, uc_ref):
    tm = CHUNK
    st = pl.program_id(1) * INPROJ_TILES + sub
    tile_rows = slice(sub * tm, (sub + 1) * tm)

    hn = _rms(x_ref[0, tile_rows, :], g_ref[...]).astype(BF16)

    qkv = lax.dot_general(hn, wqkv_ref[...], NT_DIMS, preferred_element_type=F32)
    tail = lax.dot_general(hn, wtail_ref[...], NT_DIMS, preferred_element_type=F32)
    mq = qkv[:, 0 * ATT_W:1 * ATT_W]
    mk = qkv[:, 1 * ATT_W:2 * ATT_W]
    mv = qkv[:, 2 * ATT_W:3 * ATT_W]
    fq = qkv[:, 3 * ATT_W:4 * ATT_W]
    fk = qkv[:, 4 * ATT_W:5 * ATT_W]
    fv = qkv[:, 5 * ATT_W:6 * ATT_W]
    scale = HEAD_DIM ** -0.5 * LOG2E
    mq_ref[0, sub] = (mq * scale).T.astype(BF16)
    mk_ref[0, tile_rows, :] = mk.astype(BF16)
    fq_ref[0, sub] = (fq * scale).T.astype(BF16)
    ones = jnp.ones((ONES_ROWS, tm), BF16)
    for kind, v in enumerate((mv, fv)):
        vt = v.T.astype(BF16)
        for h in range(N_HEADS):
            base = (kind * N_HEADS + h) * HEAD_ROWS
            vt_ref[0, sub, base:base + HEAD_DIM, :] = vt[h * HEAD_DIM:(h + 1) * HEAD_DIM]
            vt_ref[0, sub, base + HEAD_DIM:base + HEAD_ROWS, :] = ones

    rows = lax.broadcasted_iota(jnp.int32, km_ref.shape, 0)
    cols = lax.broadcasted_iota(jnp.int32, km_ref.shape, 1)
    same_head = (rows >> 4) == (cols >> 6)
    km = km_ref[...]
    for r in range(BLOCKS_PER_CHUNK):
        kmean = jnp.mean(mk[r * BLK:(r + 1) * BLK], axis=0, keepdims=True)
        n = st * BLOCKS_PER_CHUNK + r
        km = jnp.where(same_head & ((rows & (MAX_BLOCKS - 1)) == n), kmean, km)
    km_ref[...] = km

    km_hi, km_lo = _split_bf16(km, 2)
    q_hi, q_lo = _split_bf16(mq, 2)
    gate = (lax.dot_general(km_hi, q_hi, NT_DIMS, preferred_element_type=F32)
            + lax.dot_general(km_hi, q_lo, NT_DIMS, preferred_element_type=F32)
            + lax.dot_general(km_lo, q_hi, NT_DIMS, preferred_element_type=F32))

    nidx = lax.broadcasted_iota(jnp.int32, (MAX_BLOCKS, tm), 0)
    own = st * BLOCKS_PER_CHUNK + (lax.broadcasted_iota(jnp.int32, (MAX_BLOCKS, tm), 1) >> BLK_SHIFT)
    valid = nidx < own
    for h in range(N_HEADS):
        gh = jnp.where(valid, gate[h * MAX_BLOCKS:(h + 1) * MAX_BLOCKS], -jnp.inf)
        rank = jnp.zeros((MAX_BLOCKS, tm), jnp.int32)
        for m in range(MAX_BLOCKS):
            gm = gh[m:m + 1, :]
            gt = (gm > gh).astype(jnp.int32)
            ge = (gm >= gh).astype(jnp.int32)
            rank = rank + jnp.where(nidx > m, ge, gt)
        keep = ((rank < TOPK) & valid) | (nidx == own)
        sel_ref[0, h * MAX_BLOCKS:(h + 1) * MAX_BLOCKS, tile_rows] = jnp.where(keep, 0.0, NEG)

    fl = tail[:, 0:LANES] + bf_ref[...]
    lf = jnp.minimum(fl, 0.0) - jnp.log1p(jnp.exp(-jnp.abs(fl)))
    tri = tri_ref[...]
    cum = cum_ref[...]
    for t in _split_bf16(lf, N_SPLIT):
        cum = cum + jnp.dot(tri, t, preferred_element_type=F32)
    cum_ref[...] = cum[tm - 1:tm, :]
    kb = jnp.zeros((tm, ATT_W), F32)
    for s, t in enumerate(_split_bf16(cum * -LOG2E, N_SPLIT)):
        kb = kb + jnp.dot(t, perm_ref[s], preferred_element_type=F32)
    kb = kb.astype(BF16)
    fkb = fk.astype(BF16)
    for p in range(N_PAIRS):
        fkp_ref[0, tile_rows, 2 * p * PAIR_W:(2 * p + 1) * PAIR_W] = fkb[:, p * PAIR_W:(p + 1) * PAIR_W]
        fkp_ref[0, tile_rows, (2 * p + 1) * PAIR_W:(2 * p + 2) * PAIR_W] = kb[:, p * PAIR_W:(p + 1) * PAIR_W]

    cv = tail[:, LANES:]
    cvb = cv[:, 0:CONV_CH]
    u = cv[:, CONV_CH:2 * CONV_CH] * cv[:, 2 * CONV_CH:3 * CONV_CH]
    uc = uc_ref[...]
    prev1 = uc[7:8]
    prev2 = uc[6:7]
    ridx = lax.broadcasted_iota(jnp.int32, u.shape, 0)
    u1 = jnp.where(ridx == 0, prev1, pltpu.roll(u, 1, 0))
    u2 = jnp.where(ridx == 0, prev2, jnp.where(ridx == 1, prev1, pltpu.roll(u, 2, 0)))
    cw = cw_ref[...]
    y = cw[0:1] * u2 + cw[1:2] * u1 + cw[2:3] * u
    conv_ref[0, tile_rows, :] = (cvb * y).astype(BF16)
    uc_ref[...] = u[tm - 8:tm]


def _inproj(x, g, wqkv, wtail, bfp, cw, tri, perm):
    B, S, D = x.shape
    tm = CHUNK
    n, step_rows = INPROJ_TILES, INPROJ_TILES * CHUNK
    const = lambda shape: pl.BlockSpec(shape, lambda b, s: (0,) * len(shape), pipeline_mode=pl.Buffered(1))
    row = lambda w: pl.BlockSpec((1, step_rows, w), lambda b, s: (b, s, 0))
    colmajor = lambda w: pl.BlockSpec((1, n, w, tm), lambda b, s: (b, s, 0, 0))
    return pl.pallas_call(
        _inproj_kernel,
        grid=(B, S // step_rows),
        in_specs=[row(D), const((1, D)), const(wqkv.shape), const(wtail.shape),
                  const(bfp.shape), const(cw.shape), const(tri.shape), const(perm.shape)],
        out_specs=[colmajor(ATT_W), row(ATT_W), colmajor(ATT_W), row(2 * ATT_W),
                   colmajor(2 * N_HEADS * HEAD_ROWS),
                   pl.BlockSpec((1, N_HEADS * MAX_BLOCKS, step_rows), lambda b, s: (b, 0, s)),
                   row(CONV_CH)],
        out_shape=[jax.ShapeDtypeStruct((B, S // tm, ATT_W, tm), BF16),
                   jax.ShapeDtypeStruct((B, S, ATT_W), BF16),
                   jax.ShapeDtypeStruct((B, S // tm, ATT_W, tm), BF16),
                   jax.ShapeDtypeStruct((B, S, 2 * ATT_W), BF16),
                   jax.ShapeDtypeStruct((B, S // tm, 2 * N_HEADS * HEAD_ROWS, tm), BF16),
                   jax.ShapeDtypeStruct((B, N_HEADS * MAX_BLOCKS, S), F32),
                   jax.ShapeDtypeStruct((B, S, CONV_CH), BF16)],
        scratch_shapes=[pltpu.VMEM((N_HEADS * MAX_BLOCKS, ATT_W), F32),
                        pltpu.VMEM((1, LANES), F32),
                        pltpu.VMEM((8, CONV_CH), F32)],
        compiler_params=pltpu.CompilerParams(
            dimension_semantics=("arbitrary", "arbitrary"), vmem_limit_bytes=VMEM_LIMIT),
        name="inproj",
    )(x, g, wqkv, wtail, bfp, cw, tri, perm)


def _attn_kernel(*refs, moba, nchunk):
    if moba:
        rb_ref, qt_ref, k_ref, vt_ref, sel_ref, own_ref, corner_ref, ind_ref, o_ref, *bufs = refs
        pair = pl.program_id(1)
        xrow = lax.broadcasted_iota(jnp.int32, (XROWS, CHUNK), 0)
        far_rows, near_fix = [], []
        for hh in range(2):
            fb = rb_ref[REL_BUCKETS - 1, 2 * pair + hh] * LOG2E
            terms = _split_bf16(jnp.full((XROWS, CHUNK), fb, F32), N_SPLIT)
            rows = jnp.zeros((XROWS, CHUNK), F32)
            for t in range(N_SPLIT):
                rows = jnp.where(xrow == BLOCKS_PER_CHUNK + t, terms[t].astype(F32), rows)
            far_rows.append(rows)
            near_fix.append(corner_ref[hh] - fb)
        xpad = jnp.zeros((PAIR_W - XROWS, CHUNK), BF16)
    else:
        qt_ref, k_ref, vt_ref, o_ref, *bufs = refs
    hidx = [jnp.minimum(pl.program_id(0), 0) + hh for hh in range(2)]

    def weights(qi):
        qt = qt_ref[0, qi]
        frow = lax.broadcasted_iota(jnp.int32, qt.shape, 0)
        zero = jnp.zeros_like(qt)
        ws = []
        for hh in range(2):
            w = jnp.where((frow >= hh * HEAD_DIM) & (frow < (hh + 1) * HEAD_DIM), qt, zero)
            if not moba:
                pick = (frow >= N_SPLIT * hh) & (frow < N_SPLIT * (hh + 1))
                w = jnp.concatenate([w, jnp.where(pick, 1.0, 0.0).astype(BF16)], axis=0)
            ws.append(w)
        return ws

    def query_norms(qi):
        q2 = qt_ref[0, qi].astype(F32) ** 2
        return [jnp.sqrt(jnp.sum(q2[hh * HEAD_DIM:(hh + 1) * HEAD_DIM], axis=0, keepdims=True))
                for hh in range(2)]

    def produce(qi, ws, c, dst_ref):
        own, near = c == qi, c == qi - 1
        kc = k_ref[0, c * CHUNK:(c + 1) * CHUNK, :]
        if moba:
            kc = jnp.concatenate([kc, ind_ref[...]], axis=1)
        maxima = []
        for hh in range(2):
            w = ws[hh]
            if moba:
                rows = jnp.zeros((XROWS, CHUNK), F32) if own else far_rows[hh]
                for r in range(BLOCKS_PER_CHUNK):
                    n = hh * MAX_BLOCKS + c * BLOCKS_PER_CHUNK + r
                    sel = sel_ref[0, n:n + 1, qi * CHUNK:(qi + 1) * CHUNK]
                    rows = jnp.where(xrow == r, sel, rows)
                w = jnp.concatenate([w, rows.astype(BF16), xpad], axis=0)
            s = jnp.dot(kc, w, preferred_element_type=F32)
            if moba and own:
                s = s + own_ref[hh]
            elif moba and near:
                lo = CHUNK - CORNER
                fixed = jnp.concatenate([s[lo:, :CORNER] + near_fix[hh], s[lo:, CORNER:]], axis=1)
                s = jnp.concatenate([s[:lo], fixed], axis=0)
            elif own:
                kidx = lax.broadcasted_iota(jnp.int32, s.shape, 0)
                qidx = lax.broadcasted_iota(jnp.int32, s.shape, 1)
                s = jnp.where(kidx <= qidx, s, NEG)
            dst_ref[hidx[hh]] = s
            maxima.append(jnp.max(s, axis=0, keepdims=True))
        return tuple(maxima)

    def consume(c, src_ref, maxima, state):
        new = []
        for hh in range(2):
            m_old, acc = state[hh]
            m_new = jnp.maximum(m_old, maxima[hh])
            p = jnp.exp2(src_ref[hidx[hh]] - m_new).astype(BF16)
            vth = vt_ref[0, c, hh * HEAD_ROWS:(hh + 1) * HEAD_ROWS, :]
            pv = jnp.dot(vth, p, preferred_element_type=F32)
            new.append((m_new, jnp.exp2(m_old - m_new) * acc + pv))
        return tuple(new)

    def fresh_state():
        return tuple((jnp.full((1, CHUNK), NEG, F32), jnp.zeros((HEAD_ROWS, CHUNK), F32))
                     for _ in range(2))

    def finish(qi, state):
        ot = jnp.concatenate([acc[:HEAD_DIM] * (1.0 / acc[HEAD_DIM:HEAD_DIM + 1])
                              for (_, acc) in state], axis=0)
        o_ref[0, qi * CHUNK:(qi + 1) * CHUNK, :] = ot.T.astype(BF16)

    if moba:
        g = 0
        ws = weights(0)
        maxima = produce(0, ws, 0, bufs[0])
        for qi in range(nchunk):
            state = fresh_state()
            ws_next = weights(qi + 1) if qi + 1 < nchunk else None
            for c in range(qi, -1, -1):
                nxt = None
                if c > 0:
                    nxt = produce(qi, ws, c - 1, bufs[(g + 1) % 2])
                elif ws_next is not None:
                    nxt = produce(qi + 1, ws_next, qi + 1, bufs[(g + 1) % 2])
                state = consume(c, bufs[g % 2], maxima, state)
                maxima = nxt
                g += 1
            ws = ws_next
            finish(qi, state)
        return

    own_bufs, far_bufs = bufs[0:2], bufs[2:4]
    kall = k_ref[0][:, 0:PAIR_W].astype(F32) ** 2
    lane = lax.broadcasted_iota(jnp.int32, (1, PAIR_W), 1)
    key_norm = [jnp.sqrt(jnp.max(jnp.sum(jnp.where((lane >= hh * HEAD_DIM) & (lane < (hh + 1) * HEAD_DIM),
                                                      kall, 0.0), axis=1, keepdims=True)))
                for hh in range(2)]

    def all_older_underflow(qi, state, first_far):
        last_key = (first_far + 1) * CHUNK
        decay = k_ref[0, last_key - 16:last_key, PAIR_W:2 * PAIR_W].astype(F32)
        qn = query_norms(qi)
        worst = None
        for hh in range(2):
            mine = (lane >= N_SPLIT * hh) & (lane < N_SPLIT * (hh + 1))
            newest = jnp.max(jnp.sum(jnp.where(mine, decay, 0.0), axis=1, keepdims=True))
            gap = jnp.max(qn[hh] * key_norm[hh] + newest - state[hh][0])
            worst = gap if worst is None else jnp.maximum(worst, gap)
        return worst < -UNDERFLOW_BITS

    ws = weights(0)
    maxima = produce(0, ws, 0, own_bufs[0])
    for qi in range(nchunk):
        state = fresh_state()
        ws_next = weights(qi + 1) if qi + 1 < nchunk else None
        n_near = min(qi + 1, NEAR_CHUNKS)
        src = own_bufs[qi % 2]
        next_own = None
        for j in range(n_near):
            c = qi - j
            if j < n_near - 1:
                dst = far_bufs[j % 2]
                nxt = produce(qi, ws, c - 1, dst)
            elif ws_next is not None:
                next_own = produce(qi + 1, ws_next, qi + 1, own_bufs[(qi + 1) % 2])
            state = consume(c, src, maxima, state)
            if j < n_near - 1:
                src, maxima = dst, nxt
        first_far = qi - NEAR_CHUNKS
        if first_far >= 0:
            def far_stages(st, qi=qi, ws=ws, first_far=first_far):
                mx = produce(qi, ws, first_far, far_bufs[0])
                for i, c in enumerate(range(first_far, -1, -1)):
                    nxt = produce(qi, ws, c - 1, far_bufs[(i + 1) % 2]) if c > 0 else None
                    st = consume(c, far_bufs[i % 2], mx, st)
                    mx = nxt
                return st
            state = lax.cond(all_older_underflow(qi, state, first_far), lambda st: st, far_stages, state)
        finish(qi, state)
        ws, maxima = ws_next, next_own


def _block_indicator():
    ind = np.zeros((CHUNK, PAIR_W), np.float32)
    for r in range(BLOCKS_PER_CHUNK):
        ind[r * BLK:(r + 1) * BLK, r] = 1.0
    ind[:, BLOCKS_PER_CHUNK:BLOCKS_PER_CHUNK + N_SPLIT] = 1.0
    return jnp.asarray(ind, BF16)


def _attention(qt, k, vt, sel=None, tables=None, rel_bias=None):
    moba = sel is not None
    B, S, _ = k.shape
    nchunk = S // CHUNK
    kw = k.shape[-1] // N_PAIRS
    vt_off = 0 if moba else N_PAIRS
    in_specs = [pl.BlockSpec((1, nchunk, PAIR_W, CHUNK), lambda b, p: (b, 0, p, 0)),
                pl.BlockSpec((1, S, kw), lambda b, p: (b, 0, p)),
                pl.BlockSpec((1, nchunk, PAIR_ROWS, CHUNK), lambda b, p: (b, 0, p + vt_off, 0))]
    args = [qt, k, vt]
    if moba:
        in_specs = [pl.BlockSpec(memory_space=pltpu.SMEM)] + in_specs
        in_specs += [pl.BlockSpec((1, 2 * MAX_BLOCKS, S), lambda b, p: (b, p, 0)),
                     pl.BlockSpec((2, CHUNK, CHUNK), lambda b, p: (p, 0, 0)),
                     pl.BlockSpec((2, CORNER, CORNER), lambda b, p: (p, 0, 0)),
                     pl.BlockSpec((CHUNK, PAIR_W), lambda b, p: (0, 0))]
        args = [rel_bias] + args + [sel, *tables, _block_indicator()]
    return pl.pallas_call(
        functools.partial(_attn_kernel, moba=moba, nchunk=nchunk),
        grid=(B, N_PAIRS),
        in_specs=in_specs,
        out_specs=pl.BlockSpec((1, S, PAIR_W), lambda b, p: (b, 0, p)),
        out_shape=jax.ShapeDtypeStruct((B, S, ATT_W), BF16),
        scratch_shapes=[pltpu.VMEM((2, CHUNK, CHUNK), F32)] * (2 if moba else 4),
        compiler_params=pltpu.CompilerParams(
            dimension_semantics=("arbitrary", "arbitrary"), vmem_limit_bytes=VMEM_LIMIT),
        name="moba_attn" if moba else "fox_attn",
    )(*args)


def _out_ffn_kernel(moba_ref, fox_ref, conv_ref, x_ref, wo_ref, wg_ref, wu_ref, wd_ref,
                    gpm_ref, gpf_ref, gqf_ref, o_ref):
    mix_in = jnp.concatenate([moba_ref[...], fox_ref[...], conv_ref[...]], axis=1)
    mixed = jnp.dot(mix_in, wo_ref[...], preferred_element_type=F32)
    x1 = x_ref[...] + _rms(mixed, gpm_ref[...])
    h2 = _rms(x1, gpf_ref[...]).astype(BF16)
    ff = jnp.zeros(x1.shape, F32)
    lo = 0
    for width in FFN_CHUNKS:
        gate = jnp.dot(h2, wg_ref[:, lo:lo + width], preferred_element_type=F32)
        up = jnp.dot(h2, wu_ref[:, lo:lo + width], preferred_element_type=F32)
        act = (gate * (1.0 / (1.0 + jnp.exp(-gate))) * up).astype(BF16)
        ff = ff + jnp.dot(act, wd_ref[lo:lo + width, :], preferred_element_type=F32)
        lo += width
    o_ref[...] = x1 + _rms(ff, gqf_ref[...])


def _out_ffn(moba, fox, conv, x, wo, wg, wu, wd, gpm, gpf, gqf):
    M, D = x.shape
    tm = FFN_TILE
    row = lambda w: pl.BlockSpec((tm, w), lambda m: (m, 0))
    const = lambda shape: pl.BlockSpec(shape, lambda m: (0, 0), pipeline_mode=pl.Buffered(1))
    return pl.pallas_call(
        _out_ffn_kernel,
        grid=(M // tm,),
        in_specs=[row(ATT_W), row(ATT_W), row(CONV_CH), row(D),
                  const(wo.shape), const(wg.shape), const(wu.shape), const(wd.shape),
                  const((1, D)), const((1, D)), const((1, D))],
        out_specs=row(D),
        out_shape=jax.ShapeDtypeStruct((M, D), F32),
        compiler_params=pltpu.CompilerParams(
            dimension_semantics=("arbitrary",), vmem_limit_bytes=VMEM_LIMIT),
        name="out_ffn",
    )(moba, fox, conv, x, wo, wg, wu, wd, gpm, gpf, gqf)


def _cumsum_constants(tm):
    tri = np.tril(np.ones((tm, tm), np.float32))
    perm = np.zeros((N_SPLIT, LANES, ATT_W), np.float32)
    for s in range(N_SPLIT):
        for h in range(N_HEADS):
            perm[s, h, PAIR_W * (h // 2) + N_SPLIT * (h % 2) + s] = 1.0
    return jnp.asarray(tri, BF16), jnp.asarray(perm, BF16)


def kernel(x, w_in, b_f, conv_w, w_out, rel_bias, g_pre_mix, g_post_mix, g_pre_ffn, g_post_ffn,
           w_gate, w_up, w_down):
    B, S, D = x.shape
    depth = w_in.shape[0]
    assert D == D_MODEL and S % (INPROJ_TILES * CHUNK) == 0 and S // BLK <= MAX_BLOCKS
    assert (B * S) % FFN_TILE == 0

    rel_bias = rel_bias.astype(F32)
    tables = _bias_tiles(rel_bias)
    w_in_t = jnp.swapaxes(w_in, 1, 2)
    tri, perm = _cumsum_constants(CHUNK)
    a = ATT_W
    for l in range(depth):
        wl = w_in_t[l]
        wqkv = wl[0:6 * a].astype(BF16)
        wtail = jnp.concatenate([wl[6 * a:6 * a + N_HEADS], jnp.zeros((LANES - N_HEADS, D), wl.dtype),
                                 wl[6 * a + N_HEADS:]], axis=0).astype(BF16)
        bfp = jnp.pad(b_f[l].astype(F32), (0, LANES - N_HEADS)).reshape(1, LANES)
        row = lambda g: g[l].astype(F32).reshape(1, D)

        mq, mk, fq, fkp, vt, sel, conv = _inproj(
            x, row(g_pre_mix), wqkv, wtail, bfp, conv_w[l].astype(F32), tri, perm)
        moba = _attention(mq, mk, vt, sel, tables, rel_bias)
        fox = _attention(fq, fkp, vt)
        x = _out_ffn(moba.reshape(B * S, a), fox.reshape(B * S, a), conv.reshape(B * S, CONV_CH),
                     x.reshape(B * S, D), w_out[l].astype(BF16), w_gate[l].astype(BF16),
                     w_up[l].astype(BF16), w_down[l].astype(BF16),
                     row(g_post_mix), row(g_pre_ffn), row(g_post_ffn)).reshape(B, S, D)
    return x
```

```python
import functools
import math

import jax
import jax.numpy as jnp
import numpy as np
from jax import lax
from jax.experimental import pallas as pl
from jax.experimental.pallas import tpu as pltpu

D_MODEL = 1024
HEAD_DIM = 64
N_HEADS = 6
ATT_W = N_HEADS * HEAD_DIM
PAIR_W = 2 * HEAD_DIM
N_PAIRS = N_HEADS // 2
CONV_CH = 256
CONV_WIDTH = 3
BLK = 256
BLK_SHIFT = BLK.bit_length() - 1
CHUNK = 512
BLOCKS_PER_CHUNK = CHUNK // BLK
MAX_BLOCKS = 16
TOPK = 3
REL_BUCKETS = 32
REL_MAX_DIST = 128
D_FF = 2816
RMS_EPS = 1e-6
NEG = -1e30
LANES = 128
N_SPLIT = 3
LOG2E = math.log2(math.e)
CORNER = REL_MAX_DIST
ONES_ROWS = 16
HEAD_ROWS = HEAD_DIM + ONES_ROWS
PAIR_ROWS = 2 * HEAD_ROWS
XROWS = 16
NEAR_CHUNKS = 3
UNDERFLOW_BITS = 152.0

F32 = jnp.float32
BF16 = jnp.bfloat16
NT_DIMS = (((1,), (1,)), ((), ()))

FFN_TILE = 512
FFN_CHUNKS = (1536, 1280)
assert sum(FFN_CHUNKS) == D_FF
VMEM_LIMIT = 50 * 1024 * 1024


def _split_bf16(v, n):
    terms = []
    for _ in range(n):
        t = v.astype(BF16)
        terms.append(t)
        v = v - t.astype(F32)
    return terms


def _rms(v, g):
    return v * lax.rsqrt(jnp.mean(v * v, axis=-1, keepdims=True) + RMS_EPS) * g


def _bucket_tiles():
    max_exact = REL_BUCKETS // 2

    def bucket(dist):
        n = np.maximum(dist, 0)
        nf = np.maximum(n, 1).astype(np.float32)
        large = max_exact + (np.log(nf / np.float32(max_exact))
                             / np.float32(math.log(REL_MAX_DIST / max_exact))
                             * np.float32(REL_BUCKETS - max_exact)).astype(np.int32)
        large = np.minimum(large, REL_BUCKETS - 1)
        return np.where(n < max_exact, n, large).astype(np.int32)

    key = np.arange(CHUNK, dtype=np.int32)[:, None]
    qry = np.arange(CHUNK, dtype=np.int32)[None, :]
    d0 = qry - key
    own = np.where(d0 >= 0, bucket(d0), -1).astype(np.int32)
    prev = bucket(d0 + CHUNK)
    corner = prev[CHUNK - CORNER:, :CORNER]
    outside = prev.copy()
    outside[CHUNK - CORNER:, :CORNER] = REL_BUCKETS - 1
    assert (outside == REL_BUCKETS - 1).all()
    return own, np.ascontiguousarray(corner)


def _bias_tile_kernel(rb_ref, own_bm_ref, corner_bm_ref, own_ref, corner_ref):
    h = pl.program_id(0)
    for bm_ref, out_ref in ((own_bm_ref, own_ref), (corner_bm_ref, corner_ref)):
        bm = bm_ref[...]
        t = jnp.full(bm.shape, NEG, F32)
        for b in range(REL_BUCKETS):
            t = jnp.where(bm == b, rb_ref[b, h] * LOG2E, t)
        out_ref[0] = t


def _bias_tiles(rel_bias):
    own_bm, corner_bm = _bucket_tiles()
    return pl.pallas_call(
        _bias_tile_kernel,
        grid=(N_HEADS,),
        in_specs=[pl.BlockSpec(memory_space=pltpu.SMEM),
                  pl.BlockSpec((CHUNK, CHUNK), lambda h: (0, 0)),
                  pl.BlockSpec((CORNER, CORNER), lambda h: (0, 0))],
        out_specs=[pl.BlockSpec((1, CHUNK, CHUNK), lambda h: (h, 0, 0)),
                   pl.BlockSpec((1, CORNER, CORNER), lambda h: (h, 0, 0))],
        out_shape=[jax.ShapeDtypeStruct((N_HEADS, CHUNK, CHUNK), F32),
                   jax.ShapeDtypeStruct((N_HEADS, CORNER, CORNER), F32)],
        name="bias_tiles",
    )(rel_bias, jnp.asarray(own_bm), jnp.asarray(corner_bm))


def _inproj_kernel(x_ref, g_ref, wqkv_ref, wtail_ref, bf_ref, cw_ref, tri_ref, perm_ref,
                   mq_ref, mk_ref, fq_ref, fkp_ref, vt_ref, gate_ref, conv_ref,
                   km_ref, cum_ref, uc_ref):
    tm = CHUNK
    st = pl.program_id(1)

    @pl.when(st == 0)
    def _():
        km_ref[...] = jnp.zeros_like(km_ref)
        cum_ref[...] = jnp.zeros_like(cum_ref)
        uc_ref[...] = jnp.zeros_like(uc_ref)

    hn = _rms(x_ref[0], g_ref[...]).astype(BF16)

    qkv = lax.dot_general(hn, wqkv_ref[...], NT_DIMS, preferred_element_type=F32)
    tail = lax.dot_general(hn, wtail_ref[...], NT_DIMS, preferred_element_type=F32)
    mq = qkv[:, 0 * ATT_W:1 * ATT_W]
    mk = qkv[:, 1 * ATT_W:2 * ATT_W]
    mv = qkv[:, 2 * ATT_W:3 * ATT_W]
    fq = qkv[:, 3 * ATT_W:4 * ATT_W]
    fk = qkv[:, 4 * ATT_W:5 * ATT_W]
    fv = qkv[:, 5 * ATT_W:6 * ATT_W]
    scale = HEAD_DIM ** -0.5 * LOG2E
    mq_ref[0, 0] = (mq * scale).T.astype(BF16)
    mk_ref[0] = mk.astype(BF16)
    fq_ref[0, 0] = (fq * scale).T.astype(BF16)
    ones = jnp.ones((ONES_ROWS, tm), BF16)
    for kind, v in enumerate((mv, fv)):
        vt = v.T.astype(BF16)
        for h in range(N_HEADS):
            base = (kind * N_HEADS + h) * HEAD_ROWS
            vt_ref[0, 0, base:base + HEAD_DIM, :] = vt[h * HEAD_DIM:(h + 1) * HEAD_DIM]
            vt_ref[0, 0, base + HEAD_DIM:base + HEAD_ROWS, :] = ones

    rows = lax.broadcasted_iota(jnp.int32, km_ref.shape, 0)
    cols = lax.broadcasted_iota(jnp.int32, km_ref.shape, 1)
    same_head = (rows >> 4) == (cols >> 6)
    km = km_ref[...]
    for r in range(BLOCKS_PER_CHUNK):
        kmean = jnp.mean(mk[r * BLK:(r + 1) * BLK], axis=0, keepdims=True)
        n = st * BLOCKS_PER_CHUNK + r
        km = jnp.where(same_head & ((rows & (MAX_BLOCKS - 1)) == n), kmean, km)
    km_ref[...] = km

    km_hi, km_lo = _split_bf16(km, 2)
    q_hi, q_lo = _split_bf16(mq, 2)
    gate = (lax.dot_general(km_hi, q_hi, NT_DIMS, preferred_element_type=F32)
            + lax.dot_general(km_hi, q_lo, NT_DIMS, preferred_element_type=F32)
            + lax.dot_general(km_lo, q_hi, NT_DIMS, preferred_element_type=F32))

    gate_ref[0] = gate

    fl = tail[:, 0:LANES] + bf_ref[...]
    lf = jnp.minimum(fl, 0.0) - jnp.log1p(jnp.exp(-jnp.abs(fl)))
    tri = tri_ref[...]
    cum = cum_ref[...]
    for t in _split_bf16(lf, N_SPLIT):
        cum = cum + jnp.dot(tri, t, preferred_element_type=F32)
    cum_ref[...] = cum[tm - 1:tm, :]
    kb = jnp.zeros((tm, ATT_W), F32)
    for s, t in enumerate(_split_bf16(cum * -LOG2E, N_SPLIT)):
        kb = kb + jnp.dot(t, perm_ref[s], preferred_element_type=F32)
    kb = kb.astype(BF16)
    fkb = fk.astype(BF16)
    for p in range(N_PAIRS):
        fkp_ref[0, :, 2 * p * PAIR_W:(2 * p + 1) * PAIR_W] = fkb[:, p * PAIR_W:(p + 1) * PAIR_W]
        fkp_ref[0, :, (2 * p + 1) * PAIR_W:(2 * p + 2) * PAIR_W] = kb[:, p * PAIR_W:(p + 1) * PAIR_W]

    cv = tail[:, LANES:]
    cvb = cv[:, 0:CONV_CH]
    u = cv[:, CONV_CH:2 * CONV_CH] * cv[:, 2 * CONV_CH:3 * CONV_CH]
    uc = uc_ref[...]
    prev1 = uc[7:8]
    prev2 = uc[6:7]
    ridx = lax.broadcasted_iota(jnp.int32, u.shape, 0)
    u1 = jnp.where(ridx == 0, prev1, pltpu.roll(u, 1, 0))
    u2 = jnp.where(ridx == 0, prev2, jnp.where(ridx == 1, prev1, pltpu.roll(u, 2, 0)))
    cw = cw_ref[...]
    y = cw[0:1] * u2 + cw[1:2] * u1 + cw[2:3] * u
    conv_ref[0] = (cvb * y).astype(BF16)
    uc_ref[...] = u[tm - 8:tm]


def _inproj(x, g, wqkv, wtail, bfp, cw, tri, perm):
    B, S, D = x.shape
    tm = CHUNK
    const = lambda shape: pl.BlockSpec(shape, lambda b, s: (0,) * len(shape))
    row = lambda w: pl.BlockSpec((1, tm, w), lambda b, s: (b, s, 0))
    colmajor = lambda w: pl.BlockSpec((1, 1, w, tm), lambda b, s: (b, s, 0, 0))
    return pl.pallas_call(
        _inproj_kernel,
        grid=(B, S // tm),
        in_specs=[row(D), const((1, D)), const(wqkv.shape), const(wtail.shape),
                  const(bfp.shape), const(cw.shape), const(tri.shape), const(perm.shape)],
        out_specs=[colmajor(ATT_W), row(ATT_W), colmajor(ATT_W), row(2 * ATT_W),
                   pl.BlockSpec((1, 1, 2 * N_HEADS * HEAD_ROWS, tm), lambda b, s: (b, s, 0, 0)),
                   pl.BlockSpec((1, N_HEADS * MAX_BLOCKS, tm), lambda b, s: (b, 0, s)),
                   row(CONV_CH)],
        out_shape=[jax.ShapeDtypeStruct((B, S // tm, ATT_W, tm), BF16),
                   jax.ShapeDtypeStruct((B, S, ATT_W), BF16),
                   jax.ShapeDtypeStruct((B, S // tm, ATT_W, tm), BF16),
                   jax.ShapeDtypeStruct((B, S, 2 * ATT_W), BF16),
                   jax.ShapeDtypeStruct((B, S // tm, 2 * N_HEADS * HEAD_ROWS, tm), BF16),
                   jax.ShapeDtypeStruct((B, N_HEADS * MAX_BLOCKS, S), F32),
                   jax.ShapeDtypeStruct((B, S, CONV_CH), BF16)],
        scratch_shapes=[pltpu.VMEM((N_HEADS * MAX_BLOCKS, ATT_W), F32),
                        pltpu.VMEM((1, LANES), F32),
                        pltpu.VMEM((8, CONV_CH), F32)],
        compiler_params=pltpu.CompilerParams(
            dimension_semantics=("arbitrary", "arbitrary"), vmem_limit_bytes=VMEM_LIMIT),
        name="inproj",
    )(x, g, wqkv, wtail, bfp, cw, tri, perm)


def _attn_kernel(*refs, moba, nchunk):
    if moba:
        rb_ref, qt_ref, k_ref, vt_ref, gate_ref, own_ref, corner_ref, ind_ref, o_ref, *bufs = refs
        pair = pl.program_id(1)
        xrow = lax.broadcasted_iota(jnp.int32, (XROWS, CHUNK), 0)
        far_rows, near_fix = [], []
        for hh in range(2):
            fb = rb_ref[REL_BUCKETS - 1, 2 * pair + hh] * LOG2E
            terms = _split_bf16(jnp.full((XROWS, CHUNK), fb, F32), N_SPLIT)
            rows = jnp.zeros((XROWS, CHUNK), F32)
            for t in range(N_SPLIT):
                rows = jnp.where(xrow == BLOCKS_PER_CHUNK + t, terms[t].astype(F32), rows)
            far_rows.append(rows)
            near_fix.append(corner_ref[hh] - fb)
        xpad = jnp.zeros((PAIR_W - XROWS, CHUNK), BF16)
    else:
        qt_ref, k_ref, vt_ref, o_ref, *bufs = refs
    hidx = [jnp.minimum(pl.program_id(0), 0) + hh for hh in range(2)]

    def weights(qi):
        qt = qt_ref[0, qi]
        frow = lax.broadcasted_iota(jnp.int32, qt.shape, 0)
        zero = jnp.zeros_like(qt)
        ws = []
        for hh in range(2):
            w = jnp.where((frow >= hh * HEAD_DIM) & (frow < (hh + 1) * HEAD_DIM), qt, zero)
            if not moba:
                pick = (frow >= N_SPLIT * hh) & (frow < N_SPLIT * (hh + 1))
                w = jnp.concatenate([w, jnp.where(pick, 1.0, 0.0).astype(BF16)], axis=0)
            ws.append(w)
        return (ws, block_masks(qi)) if moba else ws

    def block_masks(qi):
        nidx = lax.broadcasted_iota(jnp.int32, (MAX_BLOCKS, CHUNK), 0)
        own = qi * BLOCKS_PER_CHUNK + (lax.broadcasted_iota(jnp.int32, (MAX_BLOCKS, CHUNK), 1) >> BLK_SHIFT)
        valid = nidx < own
        masks = []
        for hh in range(2):
            g = gate_ref[0, hh * MAX_BLOCKS:(hh + 1) * MAX_BLOCKS, qi * CHUNK:(qi + 1) * CHUNK]
            gh = jnp.where(valid, g, -jnp.inf)
            rank = jnp.zeros((MAX_BLOCKS, CHUNK), jnp.int32)
            for m in range(MAX_BLOCKS):
                gm = gh[m:m + 1, :]
                gt = (gm > gh).astype(jnp.int32)
                ge = (gm >= gh).astype(jnp.int32)
                rank = rank + jnp.where(nidx > m, ge, gt)
            keep = ((rank < TOPK) & valid) | (nidx == own)
            masks.append(jnp.where(keep, 0.0, NEG))
        return masks

    def query_norms(qi):
        q2 = qt_ref[0, qi].astype(F32) ** 2
        return [jnp.sqrt(jnp.sum(q2[hh * HEAD_DIM:(hh + 1) * HEAD_DIM], axis=0, keepdims=True))
                for hh in range(2)]

    def produce(qi, ws, c, dst_ref):
        own, near = c == qi, c == qi - 1
        kc = k_ref[0, c * CHUNK:(c + 1) * CHUNK, :]
        if moba:
            kc = jnp.concatenate([kc, ind_ref[...]], axis=1)
        maxima = []
        for hh in range(2):
            w = ws[0][hh] if moba else ws[hh]
            if moba:
                rows = jnp.zeros((XROWS, CHUNK), F32) if own else far_rows[hh]
                for r in range(BLOCKS_PER_CHUNK):
                    n = c * BLOCKS_PER_CHUNK + r
                    rows = jnp.where(xrow == r, ws[1][hh][n:n + 1], rows)
                w = jnp.concatenate([w, rows.astype(BF16), xpad], axis=0)
            s = jnp.dot(kc, w, preferred_element_type=F32)
            if moba and own:
                s = s + own_ref[hh]
            elif moba and near:
                lo = CHUNK - CORNER
                fixed = jnp.concatenate([s[lo:, :CORNER] + near_fix[hh], s[lo:, CORNER:]], axis=1)
                s = jnp.concatenate([s[:lo], fixed], axis=0)
            elif own:
                kidx = lax.broadcasted_iota(jnp.int32, s.shape, 0)
                qidx = lax.broadcasted_iota(jnp.int32, s.shape, 1)
                s = jnp.where(kidx <= qidx, s, NEG)
            dst_ref[hidx[hh]] = s
            maxima.append(jnp.max(s, axis=0, keepdims=True))
        return tuple(maxima)

    def consume(c, src_ref, maxima, state):
        new = []
        for hh in range(2):
            m_old, acc = state[hh]
            m_new = jnp.maximum(m_old, maxima[hh])
            p = jnp.exp2(src_ref[hidx[hh]] - m_new).astype(BF16)
            vth = vt_ref[0, c, hh * HEAD_ROWS:(hh + 1) * HEAD_ROWS, :]
            pv = jnp.dot(vth, p, preferred_element_type=F32)
            new.append((m_new, jnp.exp2(m_old - m_new) * acc + pv))
        return tuple(new)

    def fresh_state():
        return tuple((jnp.full((1, CHUNK), NEG, F32), jnp.zeros((HEAD_ROWS, CHUNK), F32))
                     for _ in range(2))

    def finish(qi, state):
        ot = jnp.concatenate([acc[:HEAD_DIM] * (1.0 / acc[HEAD_DIM:HEAD_DIM + 1])
                              for (_, acc) in state], axis=0)
        o_ref[0, qi * CHUNK:(qi + 1) * CHUNK, :] = ot.T.astype(BF16)

    if moba:
        g = 0
        ws = weights(0)
        maxima = produce(0, ws, 0, bufs[0])
        for qi in range(nchunk):
            state = fresh_state()
            ws_next = weights(qi + 1) if qi + 1 < nchunk else None
            for c in range(qi, -1, -1):
                nxt = None
                if c > 0:
                    nxt = produce(qi, ws, c - 1, bufs[(g + 1) % 2])
                elif ws_next is not None:
                    nxt = produce(qi + 1, ws_next, qi + 1, bufs[(g + 1) % 2])
                state = consume(c, bufs[g % 2], maxima, state)
                maxima = nxt
                g += 1
            ws = ws_next
            finish(qi, state)
        return

    own_bufs, far_bufs = bufs[0:2], bufs[2:4]
    feat_max = jnp.max(k_ref[0][:, 0:PAIR_W].astype(F32) ** 2, axis=0, keepdims=True)
    lane = lax.broadcasted_iota(jnp.int32, (1, PAIR_W), 1)
    key_norm = [jnp.sqrt(jnp.sum(jnp.where((lane >= hh * HEAD_DIM) & (lane < (hh + 1) * HEAD_DIM), feat_max, 0.0)))
                for hh in range(2)]

    def all_older_underflow(qi, state, first_far):
        last_key = (first_far + 1) * CHUNK
        decay = k_ref[0, last_key - 16:last_key, PAIR_W:2 * PAIR_W].astype(F32)
        qn = query_norms(qi)
        worst = None
        for hh in range(2):
            mine = (lane >= N_SPLIT * hh) & (lane < N_SPLIT * (hh + 1))
            newest = jnp.max(jnp.sum(jnp.where(mine, decay, 0.0), axis=1, keepdims=True))
            gap = jnp.max(qn[hh] * key_norm[hh] + newest - state[hh][0])
            worst = gap if worst is None else jnp.maximum(worst, gap)
        return worst < -UNDERFLOW_BITS

    ws = weights(0)
    maxima = produce(0, ws, 0, own_bufs[0])
    for qi in range(nchunk):
        state = fresh_state()
        ws_next = weights(qi + 1) if qi + 1 < nchunk else None
        n_near = min(qi + 1, NEAR_CHUNKS)
        src = own_bufs[qi % 2]
        next_own = None
        for j in range(n_near):
            c = qi - j
            if j < n_near - 1:
                dst = far_bufs[j % 2]
                nxt = produce(qi, ws, c - 1, dst)
            elif ws_next is not None:
                next_own = produce(qi + 1, ws_next, qi + 1, own_bufs[(qi + 1) % 2])
            state = consume(c, src, maxima, state)
            if j < n_near - 1:
                src, maxima = dst, nxt
        first_far = qi - NEAR_CHUNKS
        if first_far >= 0:
            def far_stages(st, qi=qi, ws=ws, first_far=first_far):
                mx = produce(qi, ws, first_far, far_bufs[0])
                for i, c in enumerate(range(first_far, -1, -1)):
                    nxt = produce(qi, ws, c - 1, far_bufs[(i + 1) % 2]) if c > 0 else None
                    st = consume(c, far_bufs[i % 2], mx, st)
                    mx = nxt
                return st
            state = lax.cond(all_older_underflow(qi, state, first_far), lambda st: st, far_stages, state)
        finish(qi, state)
        ws, maxima = ws_next, next_own


def _block_indicator():
    ind = np.zeros((CHUNK, PAIR_W), np.float32)
    for r in range(BLOCKS_PER_CHUNK):
        ind[r * BLK:(r + 1) * BLK, r] = 1.0
    ind[:, BLOCKS_PER_CHUNK:BLOCKS_PER_CHUNK + N_SPLIT] = 1.0
    return jnp.asarray(ind, BF16)


def _attention(qt, k, vt, gate=None, tables=None, rel_bias=None):
    moba = gate is not None
    B, S, _ = k.shape
    nchunk = S // CHUNK
    kw = k.shape[-1] // N_PAIRS
    vt_off = 0 if moba else N_PAIRS
    in_specs = [pl.BlockSpec((1, nchunk, PAIR_W, CHUNK), lambda b, p: (b, 0, p, 0)),
                pl.BlockSpec((1, S, kw), lambda b, p: (b, 0, p)),
                pl.BlockSpec((1, nchunk, PAIR_ROWS, CHUNK), lambda b, p: (b, 0, p + vt_off, 0))]
    args = [qt, k, vt]
    if moba:
        in_specs = [pl.BlockSpec(memory_space=pltpu.SMEM)] + in_specs
        in_specs += [pl.BlockSpec((1, 2 * MAX_BLOCKS, S), lambda b, p: (b, p, 0)),
                     pl.BlockSpec((2, CHUNK, CHUNK), lambda b, p: (p, 0, 0)),
                     pl.BlockSpec((2, CORNER, CORNER), lambda b, p: (p, 0, 0)),
                     pl.BlockSpec((CHUNK, PAIR_W), lambda b, p: (0, 0))]
        args = [rel_bias] + args + [gate, *tables, _block_indicator()]
    return pl.pallas_call(
        functools.partial(_attn_kernel, moba=moba, nchunk=nchunk),
        grid=(B, N_PAIRS),
        in_specs=in_specs,
        out_specs=pl.BlockSpec((1, S, PAIR_W), lambda b, p: (b, 0, p)),
        out_shape=jax.ShapeDtypeStruct((B, S, ATT_W), BF16),
        scratch_shapes=[pltpu.VMEM((2, CHUNK, CHUNK), F32)] * (2 if moba else 4),
        compiler_params=pltpu.CompilerParams(
            dimension_semantics=("arbitrary", "arbitrary"), vmem_limit_bytes=VMEM_LIMIT),
        name="moba_attn" if moba else "fox_attn",
    )(*args)


def _out_ffn_kernel(moba_ref, fox_ref, conv_ref, x_ref, wo_ref, wg_ref, wu_ref, wd_ref,
                    gpm_ref, gpf_ref, gqf_ref, o_ref):
    mix_in = jnp.concatenate([moba_ref[...], fox_ref[...], conv_ref[...]], axis=1)
    mixed = jnp.dot(mix_in, wo_ref[...], preferred_element_type=F32)
    x1 = x_ref[...] + _rms(mixed, gpm_ref[...])
    h2 = _rms(x1, gpf_ref[...]).astype(BF16)
    ff = jnp.zeros(x1.shape, F32)
    lo = 0
    for width in FFN_CHUNKS:
        gate = jnp.dot(h2, wg_ref[:, lo:lo + width], preferred_element_type=F32)
        up = jnp.dot(h2, wu_ref[:, lo:lo + width], preferred_element_type=F32)
        act = (gate * (1.0 / (1.0 + jnp.exp(-gate))) * up).astype(BF16)
        ff = ff + jnp.dot(act, wd_ref[lo:lo + width, :], preferred_element_type=F32)
        lo += width
    o_ref[...] = x1 + _rms(ff, gqf_ref[...])


def _out_ffn(moba, fox, conv, x, wo, wg, wu, wd, gpm, gpf, gqf):
    M, D = x.shape
    tm = FFN_TILE
    row = lambda w: pl.BlockSpec((tm, w), lambda m: (m, 0))
    const = lambda shape: pl.BlockSpec(shape, lambda m: (0, 0), pipeline_mode=pl.Buffered(1))
    return pl.pallas_call(
        _out_ffn_kernel,
        grid=(M // tm,),
        in_specs=[row(ATT_W), row(ATT_W), row(CONV_CH), row(D),
                  const(wo.shape), const(wg.shape), const(wu.shape), const(wd.shape),
                  const((1, D)), const((1, D)), const((1, D))],
        out_specs=row(D),
        out_shape=jax.ShapeDtypeStruct((M, D), F32),
        compiler_params=pltpu.CompilerParams(
            dimension_semantics=("arbitrary",), vmem_limit_bytes=VMEM_LIMIT),
        name="out_ffn",
    )(moba, fox, conv, x, wo, wg, wu, wd, gpm, gpf, gqf)


def _cumsum_constants(tm):
    tri = np.tril(np.ones((tm, tm), np.float32))
    perm = np.zeros((N_SPLIT, LANES, ATT_W), np.float32)
    for s in range(N_SPLIT):
        for h in range(N_HEADS):
            perm[s, h, PAIR_W * (h // 2) + N_SPLIT * (h % 2) + s] = 1.0
    return jnp.asarray(tri, BF16), jnp.asarray(perm, BF16)


def kernel(x, w_in, b_f, conv_w, w_out, rel_bias, g_pre_mix, g_post_mix, g_pre_ffn, g_post_ffn,
           w_gate, w_up, w_down):
    B, S, D = x.shape
    depth = w_in.shape[0]
    assert D == D_MODEL and S % CHUNK == 0 and S // BLK <= MAX_BLOCKS
    assert (B * S) % FFN_TILE == 0

    rel_bias = rel_bias.astype(F32)
    tables = _bias_tiles(rel_bias)
    w_in_t = jnp.swapaxes(w_in, 1, 2)
    tri, perm = _cumsum_constants(CHUNK)
    a = ATT_W
    for l in range(depth):
        wl = w_in_t[l]
        wqkv = wl[0:6 * a].astype(BF16)
        wtail = jnp.concatenate([wl[6 * a:6 * a + N_HEADS], jnp.zeros((LANES - N_HEADS, D), wl.dtype),
                                 wl[6 * a + N_HEADS:]], axis=0).astype(BF16)
        bfp = jnp.pad(b_f[l].astype(F32), (0, LANES - N_HEADS)).reshape(1, LANES)
        row = lambda g: g[l].astype(F32).reshape(1, D)

        mq, mk, fq, fkp, vt, gate, conv = _inproj(
            x, row(g_pre_mix), wqkv, wtail, bfp, conv_w[l].astype(F32), tri, perm)
        moba = _attention(mq, mk, vt, gate, tables, rel_bias)
        fox = _attention(fq, fkp, vt)
        x = _out_ffn(moba.reshape(B * S, a), fox.reshape(B * S, a), conv.reshape(B * S, CONV_CH),
                     x.reshape(B * S, D), w_out[l].astype(BF16), w_gate[l].astype(BF16),
                     w_up[l].astype(BF16), w_down[l].astype(BF16),
                     row(g_post_mix), row(g_pre_ffn), row(g_post_ffn)).reshape(B, S, D)
    return x
```

```python
import functools
import math

import jax
import jax.numpy as jnp
import numpy as np
from jax import lax
from jax.experimental import pallas as pl
from jax.experimental.pallas import tpu as pltpu

D_MODEL = 1024
HEAD_DIM = 64
N_HEADS = 6
ATT_W = N_HEADS * HEAD_DIM
PAIR_W = 2 * HEAD_DIM
N_PAIRS = N_HEADS // 2
CONV_CH = 256
CONV_WIDTH = 3
BLK = 256
BLK_SHIFT = BLK.bit_length() - 1
CHUNK = 512
BLOCKS_PER_CHUNK = CHUNK // BLK
assert BLOCKS_PER_CHUNK == 2
MAX_BLOCKS = 16
TOPK = 3
REL_BUCKETS = 32
REL_MAX_DIST = 128
D_FF = 2816
RMS_EPS = 1e-6
NEG = -1e30
LANES = 128
N_SPLIT = 3
LOG2E = math.log2(math.e)
CORNER = REL_MAX_DIST
ONES_ROWS = 16
HEAD_ROWS = HEAD_DIM + ONES_ROWS
PAIR_ROWS = 2 * HEAD_ROWS
XROWS = 16
NEAR_CHUNKS = 2
UNDERFLOW_BITS = 152.0

F32 = jnp.float32
BF16 = jnp.bfloat16
NT_DIMS = (((1,), (1,)), ((), ()))

FFN_TILE = 512
FFN_CHUNKS = (1536, 1280)
assert sum(FFN_CHUNKS) == D_FF
VMEM_LIMIT = 50 * 1024 * 1024


def _split_bf16(v, n):
    terms = []
    for _ in range(n):
        t = v.astype(BF16)
        terms.append(t)
        v = v - t.astype(F32)
    return terms


def _rms(v, g):
    return v * lax.rsqrt(jnp.mean(v * v, axis=-1, keepdims=True) + RMS_EPS) * g


def _bucket_tiles():
    max_exact = REL_BUCKETS // 2

    def bucket(dist):
        n = np.maximum(dist, 0)
        nf = np.maximum(n, 1).astype(np.float32)
        large = max_exact + (np.log(nf / np.float32(max_exact))
                             / np.float32(math.log(REL_MAX_DIST / max_exact))
                             * np.float32(REL_BUCKETS - max_exact)).astype(np.int32)
        large = np.minimum(large, REL_BUCKETS - 1)
        return np.where(n < max_exact, n, large).astype(np.int32)

    key = np.arange(CHUNK, dtype=np.int32)[:, None]
    qry = np.arange(CHUNK, dtype=np.int32)[None, :]
    d0 = qry - key
    own = np.where(d0 >= 0, bucket(d0), -1).astype(np.int32)
    prev = bucket(d0 + CHUNK)
    corner = prev[CHUNK - CORNER:, :CORNER]
    outside = prev.copy()
    outside[CHUNK - CORNER:, :CORNER] = REL_BUCKETS - 1
    assert (outside == REL_BUCKETS - 1).all()
    return own, np.ascontiguousarray(corner)


def _bias_tile_kernel(rb_ref, own_bm_ref, corner_bm_ref, own_ref, corner_ref):
    h = pl.program_id(0)
    for bm_ref, out_ref in ((own_bm_ref, own_ref), (corner_bm_ref, corner_ref)):
        bm = bm_ref[...]
        t = jnp.full(bm.shape, NEG, F32)
        for b in range(REL_BUCKETS):
            t = jnp.where(bm == b, rb_ref[b, h] * LOG2E, t)
        out_ref[0] = t


def _bias_tiles(rel_bias):
    own_bm, corner_bm = _bucket_tiles()
    return pl.pallas_call(
        _bias_tile_kernel,
        grid=(N_HEADS,),
        in_specs=[pl.BlockSpec(memory_space=pltpu.SMEM),
                  pl.BlockSpec((CHUNK, CHUNK), lambda h: (0, 0)),
                  pl.BlockSpec((CORNER, CORNER), lambda h: (0, 0))],
        out_specs=[pl.BlockSpec((1, CHUNK, CHUNK), lambda h: (h, 0, 0)),
                   pl.BlockSpec((1, CORNER, CORNER), lambda h: (h, 0, 0))],
        out_shape=[jax.ShapeDtypeStruct((N_HEADS, CHUNK, CHUNK), F32),
                   jax.ShapeDtypeStruct((N_HEADS, CORNER, CORNER), F32)],
        name="bias_tiles",
    )(rel_bias, jnp.asarray(own_bm), jnp.asarray(corner_bm))


def _inproj_kernel(x_ref, g_ref, wqkv_ref, wtail_ref, bf_ref, cw_ref, tri_ref, perm_ref,
                   mq_ref, mk_ref, fq_ref, fkp_ref, vt_ref, sel_ref, conv_ref,
                   km_ref, cum_ref, uc_ref):
    tm = CHUNK
    st = pl.program_id(1)

    @pl.when(st == 0)
    def _():
        km_ref[...] = jnp.zeros_like(km_ref)
        cum_ref[...] = jnp.zeros_like(cum_ref)
        uc_ref[...] = jnp.zeros_like(uc_ref)

    hn = _rms(x_ref[0], g_ref[...]).astype(BF16)

    qkv = lax.dot_general(hn, wqkv_ref[...], NT_DIMS, preferred_element_type=F32)
    tail = lax.dot_general(hn, wtail_ref[...], NT_DIMS, preferred_element_type=F32)
    mq = qkv[:, 0 * ATT_W:1 * ATT_W]
    mk = qkv[:, 1 * ATT_W:2 * ATT_W]
    mv = qkv[:, 2 * ATT_W:3 * ATT_W]
    fq = qkv[:, 3 * ATT_W:4 * ATT_W]
    fk = qkv[:, 4 * ATT_W:5 * ATT_W]
    fv = qkv[:, 5 * ATT_W:6 * ATT_W]
    scale = HEAD_DIM ** -0.5 * LOG2E
    mq_ref[0, 0] = (mq * scale).T.astype(BF16)
    mk_ref[0] = mk.astype(BF16)
    fq_ref[0, 0] = (fq * scale).T.astype(BF16)
    ones = jnp.ones((ONES_ROWS, tm), BF16)
    for kind, v in enumerate((mv, fv)):
        vt = v.T.astype(BF16)
        for h in range(N_HEADS):
            base = (kind * N_HEADS + h) * HEAD_ROWS
            vt_ref[0, 0, base:base + HEAD_DIM, :] = vt[h * HEAD_DIM:(h + 1) * HEAD_DIM]
            vt_ref[0, 0, base + HEAD_DIM:base + HEAD_ROWS, :] = ones

    rows = lax.broadcasted_iota(jnp.int32, km_ref.shape, 0)
    cols = lax.broadcasted_iota(jnp.int32, km_ref.shape, 1)
    same_head = (rows >> 4) == (cols >> 6)
    km = km_ref[...]
    for r in range(BLOCKS_PER_CHUNK):
        kmean = jnp.mean(mk[r * BLK:(r + 1) * BLK], axis=0, keepdims=True)
        n = st * BLOCKS_PER_CHUNK + r
        km = jnp.where(same_head & ((rows & (MAX_BLOCKS - 1)) == n), kmean, km)
    km_ref[...] = km

    km_hi, km_lo = _split_bf16(km, 2)
    q_hi, q_lo = _split_bf16(mq, 2)
    gate = (lax.dot_general(km_hi, q_hi, NT_DIMS, preferred_element_type=F32)
            + lax.dot_general(km_hi, q_lo, NT_DIMS, preferred_element_type=F32)
            + lax.dot_general(km_lo, q_hi, NT_DIMS, preferred_element_type=F32))

    nidx = lax.broadcasted_iota(jnp.int32, (MAX_BLOCKS, tm), 0)
    own = st * BLOCKS_PER_CHUNK + (lax.broadcasted_iota(jnp.int32, (MAX_BLOCKS, tm), 1) >> BLK_SHIFT)
    valid = nidx < own
    for h in range(N_HEADS):
        gh = jnp.where(valid, gate[h * MAX_BLOCKS:(h + 1) * MAX_BLOCKS], -jnp.inf)
        rank = jnp.zeros((MAX_BLOCKS, tm), jnp.int32)
        for m in range(MAX_BLOCKS):
            gm = gh[m:m + 1, :]
            gt = (gm > gh).astype(jnp.int32)
            ge = (gm >= gh).astype(jnp.int32)
            rank = rank + jnp.where(nidx > m, ge, gt)
        keep = ((rank < TOPK) & valid) | (nidx == own)
        sel_ref[0, h * MAX_BLOCKS:(h + 1) * MAX_BLOCKS, :] = jnp.where(keep, 0.0, NEG)

    fl = tail[:, 0:LANES] + bf_ref[...]
    lf = jnp.minimum(fl, 0.0) - jnp.log1p(jnp.exp(-jnp.abs(fl)))
    tri = tri_ref[...]
    cum = cum_ref[...]
    for t in _split_bf16(lf, N_SPLIT):
        cum = cum + jnp.dot(tri, t, preferred_element_type=F32)
    cum_ref[...] = cum[tm - 1:tm, :]
    kb = jnp.zeros((tm, ATT_W), F32)
    for s, t in enumerate(_split_bf16(cum * -LOG2E, N_SPLIT)):
        kb = kb + jnp.dot(t, perm_ref[s], preferred_element_type=F32)
    kb = kb.astype(BF16)
    fkb = fk.astype(BF16)
    for p in range(N_PAIRS):
        fkp_ref[0, :, 2 * p * PAIR_W:(2 * p + 1) * PAIR_W] = fkb[:, p * PAIR_W:(p + 1) * PAIR_W]
        fkp_ref[0, :, (2 * p + 1) * PAIR_W:(2 * p + 2) * PAIR_W] = kb[:, p * PAIR_W:(p + 1) * PAIR_W]

    cv = tail[:, LANES:]
    cvb = cv[:, 0:CONV_CH]
    u = cv[:, CONV_CH:2 * CONV_CH] * cv[:, 2 * CONV_CH:3 * CONV_CH]
    uc = uc_ref[...]
    prev1 = uc[7:8]
    prev2 = uc[6:7]
    ridx = lax.broadcasted_iota(jnp.int32, u.shape, 0)
    u1 = jnp.where(ridx == 0, prev1, pltpu.roll(u, 1, 0))
    u2 = jnp.where(ridx == 0, prev2, jnp.where(ridx == 1, prev1, pltpu.roll(u, 2, 0)))
    cw = cw_ref[...]
    y = cw[0:1] * u2 + cw[1:2] * u1 + cw[2:3] * u
    conv_ref[0] = (cvb * y).astype(BF16)
    uc_ref[...] = u[tm - 8:tm]


def _inproj(x, g, wqkv, wtail, bfp, cw, tri, perm):
    B, S, D = x.shape
    tm = CHUNK
    const = lambda shape: pl.BlockSpec(shape, lambda b, s: (0,) * len(shape))
    row = lambda w: pl.BlockSpec((1, tm, w), lambda b, s: (b, s, 0))
    colmajor = lambda w: pl.BlockSpec((1, 1, w, tm), lambda b, s: (b, s, 0, 0))
    return pl.pallas_call(
        _inproj_kernel,
        grid=(B, S // tm),
        in_specs=[row(D), const((1, D)), const(wqkv.shape), const(wtail.shape),
                  const(bfp.shape), const(cw.shape), const(tri.shape), const(perm.shape)],
        out_specs=[colmajor(ATT_W), row(ATT_W), colmajor(ATT_W), row(2 * ATT_W),
                   pl.BlockSpec((1, 1, 2 * N_HEADS * HEAD_ROWS, tm), lambda b, s: (b, s, 0, 0)),
                   pl.BlockSpec((1, N_HEADS * MAX_BLOCKS, tm), lambda b, s: (b, 0, s)),
                   row(CONV_CH)],
        out_shape=[jax.ShapeDtypeStruct((B, S // tm, ATT_W, tm), BF16),
                   jax.ShapeDtypeStruct((B, S, ATT_W), BF16),
                   jax.ShapeDtypeStruct((B, S // tm, ATT_W, tm), BF16),
                   jax.ShapeDtypeStruct((B, S, 2 * ATT_W), BF16),
                   jax.ShapeDtypeStruct((B, S // tm, 2 * N_HEADS * HEAD_ROWS, tm), BF16),
                   jax.ShapeDtypeStruct((B, N_HEADS * MAX_BLOCKS, S), F32),
                   jax.ShapeDtypeStruct((B, S, CONV_CH), BF16)],
        scratch_shapes=[pltpu.VMEM((N_HEADS * MAX_BLOCKS, ATT_W), F32),
                        pltpu.VMEM((1, LANES), F32),
                        pltpu.VMEM((8, CONV_CH), F32)],
        compiler_params=pltpu.CompilerParams(
            dimension_semantics=("arbitrary", "arbitrary"), vmem_limit_bytes=VMEM_LIMIT),
        name="inproj",
    )(x, g, wqkv, wtail, bfp, cw, tri, perm)


def _attn_kernel(*refs, moba, nchunk):
    if moba:
        rb_ref, qt_ref, k_ref, vt_ref, sel_ref, own_ref, corner_ref, ind_ref, o_ref, *bufs = refs
        pair = pl.program_id(1)
        xrow = lax.broadcasted_iota(jnp.int32, (XROWS, CHUNK), 0)
        far_rows, near_fix = [], []
        for hh in range(2):
            fb = rb_ref[REL_BUCKETS - 1, 2 * pair + hh] * LOG2E
            terms = _split_bf16(jnp.full((XROWS, CHUNK), fb, F32), N_SPLIT)
            rows = jnp.zeros((XROWS, CHUNK), F32)
            for t in range(N_SPLIT):
                rows = jnp.where(xrow == BLOCKS_PER_CHUNK + t, terms[t].astype(F32), rows)
            far_rows.append(rows)
            near_fix.append(corner_ref[hh] - fb)
        xpad = jnp.zeros((PAIR_W - XROWS, CHUNK), BF16)
    else:
        qt_ref, k_ref, vt_ref, o_ref, *bufs = refs
    hidx = [jnp.minimum(pl.program_id(0), 0) + hh for hh in range(2)]

    def weights(qi):
        qt = qt_ref[0, qi]
        frow = lax.broadcasted_iota(jnp.int32, qt.shape, 0)
        zero = jnp.zeros_like(qt)
        ws = []
        for hh in range(2):
            w = jnp.where((frow >= hh * HEAD_DIM) & (frow < (hh + 1) * HEAD_DIM), qt, zero)
            if not moba:
                pick = (frow >= N_SPLIT * hh) & (frow < N_SPLIT * (hh + 1))
                w = jnp.concatenate([w, jnp.where(pick, 1.0, 0.0).astype(BF16)], axis=0)
            ws.append(w)
        return ws

    def query_norms(qi):
        q2 = qt_ref[0, qi].astype(F32) ** 2
        return [jnp.sqrt(jnp.sum(q2[hh * HEAD_DIM:(hh + 1) * HEAD_DIM], axis=0, keepdims=True))
                for hh in range(2)]

    def produce(qi, ws, c, dst_ref):
        own, near = c == qi, c == qi - 1
        kc = k_ref[0, c * CHUNK:(c + 1) * CHUNK, :]
        if moba:
            kc = jnp.concatenate([kc, ind_ref[...]], axis=1)
        maxima = []
        for hh in range(2):
            w = ws[hh]
            if moba:
                rows = jnp.zeros((XROWS, CHUNK), F32) if own else far_rows[hh]
                for r in range(BLOCKS_PER_CHUNK):
                    n = hh * MAX_BLOCKS + c * BLOCKS_PER_CHUNK + r
                    sel = sel_ref[0, n:n + 1, qi * CHUNK:(qi + 1) * CHUNK]
                    rows = jnp.where(xrow == r, sel, rows)
                w = jnp.concatenate([w, rows.astype(BF16), xpad], axis=0)
            if own:
                top = jnp.dot(kc[:BLK], w, preferred_element_type=F32)
                bot = jnp.dot(kc[BLK:], w[:, BLK:], preferred_element_type=F32)
                if moba:
                    top = top + own_ref[hh, :BLK, :]
                    bot = bot + own_ref[hh, BLK:, BLK:]
                else:
                    kidx = lax.broadcasted_iota(jnp.int32, (BLK, BLK), 0)
                    qidx = lax.broadcasted_iota(jnp.int32, (BLK, BLK), 1)
                    causal = kidx <= qidx
                    top = jnp.concatenate([jnp.where(causal, top[:, :BLK], NEG), top[:, BLK:]], axis=1)
                    bot = jnp.where(causal, bot, NEG)
                dst_ref[hidx[hh], :BLK, :] = top
                dst_ref[hidx[hh], BLK:, BLK:] = bot
                mt = jnp.max(top, axis=0, keepdims=True)
                mb = jnp.max(bot, axis=0, keepdims=True)
                maxima.append(jnp.concatenate([mt[:, :BLK], jnp.maximum(mt[:, BLK:], mb)], axis=1))
                continue
            s = jnp.dot(kc, w, preferred_element_type=F32)
            if moba and near:
                lo = CHUNK - CORNER
                fixed = jnp.concatenate([s[lo:, :CORNER] + near_fix[hh], s[lo:, CORNER:]], axis=1)
                s = jnp.concatenate([s[:lo], fixed], axis=0)
            dst_ref[hidx[hh]] = s
            maxima.append(jnp.max(s, axis=0, keepdims=True))
        return tuple(maxima)

    def consume(c, src_ref, maxima, state, own=False):
        new = []
        for hh in range(2):
            m_old, acc = state[hh]
            m_new = jnp.maximum(m_old, maxima[hh])
            vth = vt_ref[0, c, hh * HEAD_ROWS:(hh + 1) * HEAD_ROWS, :]
            if own:
                p_top = jnp.exp2(src_ref[hidx[hh], :BLK, :] - m_new).astype(BF16)
                p_bot = jnp.exp2(src_ref[hidx[hh], BLK:, BLK:] - m_new[:, BLK:]).astype(BF16)
                pv_top = jnp.dot(vth[:, :BLK], p_top, preferred_element_type=F32)
                pv_bot = jnp.dot(vth[:, BLK:], p_bot, preferred_element_type=F32)
                pv = jnp.concatenate([pv_top[:, :BLK], pv_top[:, BLK:] + pv_bot], axis=1)
            else:
                p = jnp.exp2(src_ref[hidx[hh]] - m_new).astype(BF16)
                pv = jnp.dot(vth, p, preferred_element_type=F32)
            new.append((m_new, jnp.exp2(m_old - m_new) * acc + pv))
        return tuple(new)

    def fresh_state():
        return tuple((jnp.full((1, CHUNK), NEG, F32), jnp.zeros((HEAD_ROWS, CHUNK), F32))
                     for _ in range(2))

    def finish(qi, state):
        ot = jnp.concatenate([acc[:HEAD_DIM] * (1.0 / acc[HEAD_DIM:HEAD_DIM + 1])
                              for (_, acc) in state], axis=0)
        o_ref[0, qi * CHUNK:(qi + 1) * CHUNK, :] = ot.T.astype(BF16)

    if moba:
        g = 0
        ws = weights(0)
        maxima = produce(0, ws, 0, bufs[0])
        for qi in range(nchunk):
            state = fresh_state()
            ws_next = weights(qi + 1) if qi + 1 < nchunk else None
            for c in range(qi, -1, -1):
                nxt = None
                if c > 0:
                    nxt = produce(qi, ws, c - 1, bufs[(g + 1) % 2])
                elif ws_next is not None:
                    nxt = produce(qi + 1, ws_next, qi + 1, bufs[(g + 1) % 2])
                state = consume(c, bufs[g % 2], maxima, state, own=(c == qi))
                maxima = nxt
                g += 1
            ws = ws_next
            finish(qi, state)
        return

    own_bufs, far_bufs = bufs[0:2], bufs[2:4]
    kall = k_ref[0][:, 0:PAIR_W].astype(F32) ** 2
    lane = lax.broadcasted_iota(jnp.int32, (1, PAIR_W), 1)
    key_norm = [jnp.sqrt(jnp.max(jnp.sum(jnp.where((lane >= hh * HEAD_DIM) & (lane < (hh + 1) * HEAD_DIM),
                                                      kall, 0.0), axis=1, keepdims=True)))
                for hh in range(2)]

    def gap(qn, decay_rows, hh, m, newer_queries_only):
        mine = (lane >= N_SPLIT * hh) & (lane < N_SPLIT * (hh + 1))
        newest = jnp.max(jnp.sum(jnp.where(mine, decay_rows, 0.0), axis=1, keepdims=True))
        bound = qn[hh] * key_norm[hh] + newest - m
        if newer_queries_only:
            qlane = lax.broadcasted_iota(jnp.int32, bound.shape, 1)
            bound = jnp.where(qlane >= BLK, bound, -jnp.inf)
        return jnp.max(bound)

    def only_band_matters(qi, state):
        base = (qi - 2) * CHUNK
        older_blk = k_ref[0, base + BLK - 16:base + BLK, PAIR_W:2 * PAIR_W].astype(F32)
        newer_blk = k_ref[0, base + CHUNK - 16:base + CHUNK, PAIR_W:2 * PAIR_W].astype(F32)
        qn = query_norms(qi)
        worst = None
        for hh in range(2):
            g = jnp.maximum(gap(qn, older_blk, hh, state[hh][0], False),
                            gap(qn, newer_blk, hh, state[hh][0], True))
            worst = g if worst is None else jnp.maximum(worst, g)
        return worst < -UNDERFLOW_BITS

    def band_stage(qi, ws, state):
        c = qi - 2
        kq = k_ref[0, c * CHUNK + BLK:(c + 1) * CHUNK, :]
        new = []
        for hh in range(2):
            m_old, acc = state[hh]
            s = jnp.dot(kq, ws[hh][:, :BLK], preferred_element_type=F32)
            m_left = jnp.maximum(m_old[:, :BLK], jnp.max(s, axis=0, keepdims=True))
            p = jnp.exp2(s - m_left).astype(BF16)
            vth = vt_ref[0, c, hh * HEAD_ROWS:(hh + 1) * HEAD_ROWS, BLK:]
            left = jnp.exp2(m_old[:, :BLK] - m_left) * acc[:, :BLK] + jnp.dot(vth, p, preferred_element_type=F32)
            new.append((jnp.concatenate([m_left, m_old[:, BLK:]], axis=1),
                        jnp.concatenate([left, acc[:, BLK:]], axis=1)))
        return tuple(new)

    ws = weights(0)
    maxima = produce(0, ws, 0, own_bufs[0])
    for qi in range(nchunk):
        state = fresh_state()
        ws_next = weights(qi + 1) if qi + 1 < nchunk else None
        n_near = min(qi + 1, NEAR_CHUNKS)
        src = own_bufs[qi % 2]
        next_own = None
        for j in range(n_near):
            c = qi - j
            if j < n_near - 1:
                dst = far_bufs[j % 2]
                nxt = produce(qi, ws, c - 1, dst)
            elif ws_next is not None:
                next_own = produce(qi + 1, ws_next, qi + 1, own_bufs[(qi + 1) % 2])
            state = consume(c, src, maxima, state, own=(j == 0))
            if j < n_near - 1:
                src, maxima = dst, nxt
        first_far = qi - NEAR_CHUNKS
        if first_far >= 0:
            def far_stages(st, qi=qi, ws=ws, first_far=first_far):
                mx = produce(qi, ws, first_far, far_bufs[0])
                for i, c in enumerate(range(first_far, -1, -1)):
                    nxt = produce(qi, ws, c - 1, far_bufs[(i + 1) % 2]) if c > 0 else None
                    st = consume(c, far_bufs[i % 2], mx, st)
                    mx = nxt
                return st
            state = lax.cond(only_band_matters(qi, state),
                             functools.partial(band_stage, qi, ws), far_stages, state)
        finish(qi, state)
        ws, maxima = ws_next, next_own


def _block_indicator():
    ind = np.zeros((CHUNK, PAIR_W), np.float32)
    for r in range(BLOCKS_PER_CHUNK):
        ind[r * BLK:(r + 1) * BLK, r] = 1.0
    ind[:, BLOCKS_PER_CHUNK:BLOCKS_PER_CHUNK + N_SPLIT] = 1.0
    return jnp.asarray(ind, BF16)


def _attention(qt, k, vt, sel=None, tables=None, rel_bias=None):
    moba = sel is not None
    B, S, _ = k.shape
    nchunk = S // CHUNK
    kw = k.shape[-1] // N_PAIRS
    vt_off = 0 if moba else N_PAIRS
    in_specs = [pl.BlockSpec((1, nchunk, PAIR_W, CHUNK), lambda b, p: (b, 0, p, 0)),
                pl.BlockSpec((1, S, kw), lambda b, p: (b, 0, p)),
                pl.BlockSpec((1, nchunk, PAIR_ROWS, CHUNK), lambda b, p: (b, 0, p + vt_off, 0))]
    args = [qt, k, vt]
    if moba:
        in_specs = [pl.BlockSpec(memory_space=pltpu.SMEM)] + in_specs
        in_specs += [pl.BlockSpec((1, 2 * MAX_BLOCKS, S), lambda b, p: (b, p, 0)),
                     pl.BlockSpec((2, CHUNK, CHUNK), lambda b, p: (p, 0, 0)),
                     pl.BlockSpec((2, CORNER, CORNER), lambda b, p: (p, 0, 0)),
                     pl.BlockSpec((CHUNK, PAIR_W), lambda b, p: (0, 0))]
        args = [rel_bias] + args + [sel, *tables, _block_indicator()]
    return pl.pallas_call(
        functools.partial(_attn_kernel, moba=moba, nchunk=nchunk),
        grid=(B, N_PAIRS),
        in_specs=in_specs,
        out_specs=pl.BlockSpec((1, S, PAIR_W), lambda b, p: (b, 0, p)),
        out_shape=jax.ShapeDtypeStruct((B, S, ATT_W), BF16),
        scratch_shapes=[pltpu.VMEM((2, CHUNK, CHUNK), F32)] * (2 if moba else 4),
        compiler_params=pltpu.CompilerParams(
            dimension_semantics=("arbitrary", "arbitrary"), vmem_limit_bytes=VMEM_LIMIT),
        name="moba_attn" if moba else "fox_attn",
    )(*args)


def _out_ffn_kernel(moba_ref, fox_ref, conv_ref, x_ref, wo_ref, wg_ref, wu_ref, wd_ref,
                    gpm_ref, gpf_ref, gqf_ref, o_ref):
    mix_in = jnp.concatenate([moba_ref[...], fox_ref[...], conv_ref[...]], axis=1)
    mixed = jnp.dot(mix_in, wo_ref[...], preferred_element_type=F32)
    x1 = x_ref[...] + _rms(mixed, gpm_ref[...])
    h2 = _rms(x1, gpf_ref[...]).astype(BF16)
    ff = jnp.zeros(x1.shape, F32)
    lo = 0
    for width in FFN_CHUNKS:
        gate = jnp.dot(h2, wg_ref[:, lo:lo + width], preferred_element_type=F32)
        up = jnp.dot(h2, wu_ref[:, lo:lo + width], preferred_element_type=F32)
        act = (gate * (1.0 / (1.0 + jnp.exp(-gate))) * up).astype(BF16)
        ff = ff + jnp.dot(act, wd_ref[lo:lo + width, :], preferred_element_type=F32)
        lo += width
    o_ref[...] = x1 + _rms(ff, gqf_ref[...])


def _out_ffn(moba, fox, conv, x, wo, wg, wu, wd, gpm, gpf, gqf):
    M, D = x.shape
    tm = FFN_TILE
    row = lambda w: pl.BlockSpec((tm, w), lambda m: (m, 0))
    const = lambda shape: pl.BlockSpec(shape, lambda m: (0, 0), pipeline_mode=pl.Buffered(1))
    return pl.pallas_call(
        _out_ffn_kernel,
        grid=(M // tm,),
        in_specs=[row(ATT_W), row(ATT_W), row(CONV_CH), row(D),
                  const(wo.shape), const(wg.shape), const(wu.shape), const(wd.shape),
                  const((1, D)), const((1, D)), const((1, D))],
        out_specs=row(D),
        out_shape=jax.ShapeDtypeStruct((M, D), F32),
        compiler_params=pltpu.CompilerParams(
            dimension_semantics=("arbitrary",), vmem_limit_bytes=VMEM_LIMIT),
        name="out_ffn",
    )(moba, fox, conv, x, wo, wg, wu, wd, gpm, gpf, gqf)


def _cumsum_constants(tm):
    tri = np.tril(np.ones((tm, tm), np.float32))
    perm = np.zeros((N_SPLIT, LANES, ATT_W), np.float32)
    for s in range(N_SPLIT):
        for h in range(N_HEADS):
            perm[s, h, PAIR_W * (h // 2) + N_SPLIT * (h % 2) + s] = 1.0
    return jnp.asarray(tri, BF16), jnp.asarray(perm, BF16)


def kernel(x, w_in, b_f, conv_w, w_out, rel_bias, g_pre_mix, g_post_mix, g_pre_ffn, g_post_ffn,
           w_gate, w_up, w_down):
    B, S, D = x.shape
    depth = w_in.shape[0]
    assert D == D_MODEL and S % CHUNK == 0 and S // BLK <= MAX_BLOCKS
    assert (B * S) % FFN_TILE == 0

    rel_bias = rel_bias.astype(F32)
    tables = _bias_tiles(rel_bias)
    w_in_t = jnp.swapaxes(w_in, 1, 2)
    tri, perm = _cumsum_constants(CHUNK)
    a = ATT_W
    for l in range(depth):
        wl = w_in_t[l]
        wqkv = wl[0:6 * a].astype(BF16)
        wtail = jnp.concatenate([wl[6 * a:6 * a + N_HEADS], jnp.zeros((LANES - N_HEADS, D), wl.dtype),
                                 wl[6 * a + N_HEADS:]], axis=0).astype(BF16)
        bfp = jnp.pad(b_f[l].astype(F32), (0, LANES - N_HEADS)).reshape(1, LANES)
        row = lambda g: g[l].astype(F32).reshape(1, D)

        mq, mk, fq, fkp, vt, sel, conv = _inproj(
            x, row(g_pre_mix), wqkv, wtail, bfp, conv_w[l].astype(F32), tri, perm)
        moba = _attention(mq, mk, vt, sel, tables, rel_bias)
        fox = _attention(fq, fkp, vt)
        x = _out_ffn(moba.reshape(B * S, a), fox.reshape(B * S, a), conv.reshape(B * S, CONV_CH),
                     x.reshape(B * S, D), w_out[l].astype(BF16), w_gate[l].astype(BF16),
                     w_up[l].astype(BF16), w_down[l].astype(BF16),
                     row(g_post_mix), row(g_pre_ffn), row(g_post_ffn)).reshape(B, S, D)
    return x
```

```python
import functools
import math

import jax
import jax.numpy as jnp
import numpy as np
from jax import lax
from jax.experimental import pallas as pl
from jax.experimental.pallas import tpu as pltpu

D_MODEL = 1024
HEAD_DIM = 64
N_HEADS = 6
ATT_W = N_HEADS * HEAD_DIM
PAIR_W = 2 * HEAD_DIM
N_PAIRS = N_HEADS // 2
CONV_CH = 256
CONV_WIDTH = 3
BLK = 256
BLK_SHIFT = BLK.bit_length() - 1
CHUNK = 512
BLOCKS_PER_CHUNK = CHUNK // BLK
assert BLOCKS_PER_CHUNK == 2
MAX_BLOCKS = 16
TOPK = 3
REL_BUCKETS = 32
REL_MAX_DIST = 128
D_FF = 2816
RMS_EPS = 1e-6
NEG = -1e30
LANES = 128
N_SPLIT = 3
LOG2E = math.log2(math.e)
CORNER = REL_MAX_DIST
ONES_ROWS = 16
HEAD_ROWS = HEAD_DIM + ONES_ROWS
PAIR_ROWS = 2 * HEAD_ROWS
XROWS = 16
NEAR_CHUNKS = 2
UNDERFLOW_BITS = 152.0

F32 = jnp.float32
BF16 = jnp.bfloat16
NT_DIMS = (((1,), (1,)), ((), ()))

FFN_TILE = 512
FFN_CHUNKS = (1536, 1280)
assert sum(FFN_CHUNKS) == D_FF
VMEM_LIMIT = 50 * 1024 * 1024


def _split_bf16(v, n):
    terms = []
    for _ in range(n):
        t = v.astype(BF16)
        terms.append(t)
        v = v - t.astype(F32)
    return terms


def _rms(v, g):
    return v * lax.rsqrt(jnp.mean(v * v, axis=-1, keepdims=True) + RMS_EPS) * g


def _bucket_tiles():
    max_exact = REL_BUCKETS // 2

    def bucket(dist):
        n = np.maximum(dist, 0)
        nf = np.maximum(n, 1).astype(np.float32)
        large = max_exact + (np.log(nf / np.float32(max_exact))
                             / np.float32(math.log(REL_MAX_DIST / max_exact))
                             * np.float32(REL_BUCKETS - max_exact)).astype(np.int32)
        large = np.minimum(large, REL_BUCKETS - 1)
        return np.where(n < max_exact, n, large).astype(np.int32)

    key = np.arange(CHUNK, dtype=np.int32)[:, None]
    qry = np.arange(CHUNK, dtype=np.int32)[None, :]
    d0 = qry - key
    own = np.where(d0 >= 0, bucket(d0), -1).astype(np.int32)
    prev = bucket(d0 + CHUNK)
    corner = prev[CHUNK - CORNER:, :CORNER]
    outside = prev.copy()
    outside[CHUNK - CORNER:, :CORNER] = REL_BUCKETS - 1
    assert (outside == REL_BUCKETS - 1).all()
    return own, np.ascontiguousarray(corner)


def _bias_tile_kernel(rb_ref, own_bm_ref, corner_bm_ref, own_ref, corner_ref):
    h = pl.program_id(0)
    for bm_ref, out_ref in ((own_bm_ref, own_ref), (corner_bm_ref, corner_ref)):
        bm = bm_ref[...]
        t = jnp.full(bm.shape, NEG, F32)
        for b in range(REL_BUCKETS):
            t = jnp.where(bm == b, rb_ref[b, h] * LOG2E, t)
        out_ref[0] = t


def _bias_tiles(rel_bias):
    own_bm, corner_bm = _bucket_tiles()
    return pl.pallas_call(
        _bias_tile_kernel,
        grid=(N_HEADS,),
        in_specs=[pl.BlockSpec(memory_space=pltpu.SMEM),
                  pl.BlockSpec((CHUNK, CHUNK), lambda h: (0, 0)),
                  pl.BlockSpec((CORNER, CORNER), lambda h: (0, 0))],
        out_specs=[pl.BlockSpec((1, CHUNK, CHUNK), lambda h: (h, 0, 0)),
                   pl.BlockSpec((1, CORNER, CORNER), lambda h: (h, 0, 0))],
        out_shape=[jax.ShapeDtypeStruct((N_HEADS, CHUNK, CHUNK), F32),
                   jax.ShapeDtypeStruct((N_HEADS, CORNER, CORNER), F32)],
        name="bias_tiles",
    )(rel_bias, jnp.asarray(own_bm), jnp.asarray(corner_bm))


def _inproj_kernel(x_ref, g_ref, wqkv_ref, wtail_ref, bf_ref, cw_ref, tri_ref, perm_ref,
                   mq_ref, mk_ref, fq_ref, fkp_ref, vt_ref, sel_ref, conv_ref,
                   km_ref, cum_ref, uc_ref):
    tm = CHUNK
    st = pl.program_id(1)

    @pl.when(st == 0)
    def _():
        km_ref[...] = jnp.zeros_like(km_ref)
        cum_ref[...] = jnp.zeros_like(cum_ref)
        uc_ref[...] = jnp.zeros_like(uc_ref)

    hn = _rms(x_ref[0], g_ref[...]).astype(BF16)

    qkv = lax.dot_general(hn, wqkv_ref[...], NT_DIMS, preferred_element_type=F32)
    tail = lax.dot_general(hn, wtail_ref[...], NT_DIMS, preferred_element_type=F32)
    mq = qkv[:, 0 * ATT_W:1 * ATT_W]
    mk = qkv[:, 1 * ATT_W:2 * ATT_W]
    mv = qkv[:, 2 * ATT_W:3 * ATT_W]
    fq = qkv[:, 3 * ATT_W:4 * ATT_W]
    fk = qkv[:, 4 * ATT_W:5 * ATT_W]
    fv = qkv[:, 5 * ATT_W:6 * ATT_W]
    scale = HEAD_DIM ** -0.5 * LOG2E
    mq_ref[0, 0] = (mq * scale).T.astype(BF16)
    mk_ref[0] = mk.astype(BF16)
    fq_ref[0, 0] = (fq * scale).T.astype(BF16)
    ones = jnp.ones((ONES_ROWS, tm), BF16)
    for kind, v in enumerate((mv, fv)):
        vt = v.T.astype(BF16)
        for h in range(N_HEADS):
            base = (kind * N_HEADS + h) * HEAD_ROWS
            vt_ref[0, 0, base:base + HEAD_DIM, :] = vt[h * HEAD_DIM:(h + 1) * HEAD_DIM]
            vt_ref[0, 0, base + HEAD_DIM:base + HEAD_ROWS, :] = ones

    rows = lax.broadcasted_iota(jnp.int32, km_ref.shape, 0)
    cols = lax.broadcasted_iota(jnp.int32, km_ref.shape, 1)
    same_head = (rows >> 4) == (cols >> 6)
    km = km_ref[...]
    for r in range(BLOCKS_PER_CHUNK):
        kmean = jnp.mean(mk[r * BLK:(r + 1) * BLK], axis=0, keepdims=True)
        n = st * BLOCKS_PER_CHUNK + r
        km = jnp.where(same_head & ((rows & (MAX_BLOCKS - 1)) == n), kmean, km)
    km_ref[...] = km

    km_hi, km_lo = _split_bf16(km, 2)
    q_hi, q_lo = _split_bf16(mq, 2)
    gate = (lax.dot_general(km_hi, q_hi, NT_DIMS, preferred_element_type=F32)
            + lax.dot_general(km_hi, q_lo, NT_DIMS, preferred_element_type=F32)
            + lax.dot_general(km_lo, q_hi, NT_DIMS, preferred_element_type=F32))

    nidx = lax.broadcasted_iota(jnp.int32, (MAX_BLOCKS, tm), 0)
    own = st * BLOCKS_PER_CHUNK + (lax.broadcasted_iota(jnp.int32, (MAX_BLOCKS, tm), 1) >> BLK_SHIFT)
    valid = nidx < own
    for h in range(N_HEADS):
        gh = jnp.where(valid, gate[h * MAX_BLOCKS:(h + 1) * MAX_BLOCKS], -jnp.inf)
        rank = jnp.zeros((MAX_BLOCKS, tm), jnp.int32)
        for m in range(MAX_BLOCKS):
            gm = gh[m:m + 1, :]
            gt = (gm > gh).astype(jnp.int32)
            ge = (gm >= gh).astype(jnp.int32)
            rank = rank + jnp.where(nidx > m, ge, gt)
        keep = ((rank < TOPK) & valid) | (nidx == own)
        sel_ref[0, h * MAX_BLOCKS:(h + 1) * MAX_BLOCKS, :] = jnp.where(keep, 0.0, NEG)

    fl = tail[:, 0:LANES] + bf_ref[...]
    lf = jnp.minimum(fl, 0.0) - jnp.log1p(jnp.exp(-jnp.abs(fl)))
    tri = tri_ref[...]
    cum = cum_ref[...]
    for t in _split_bf16(lf, N_SPLIT):
        cum = cum + jnp.dot(tri, t, preferred_element_type=F32)
    cum_ref[...] = cum[tm - 1:tm, :]
    kb = jnp.zeros((tm, ATT_W), F32)
    for s, t in enumerate(_split_bf16(cum * -LOG2E, N_SPLIT)):
        kb = kb + jnp.dot(t, perm_ref[s], preferred_element_type=F32)
    kb = kb.astype(BF16)
    fkb = fk.astype(BF16)
    for p in range(N_PAIRS):
        fkp_ref[0, :, 2 * p * PAIR_W:(2 * p + 1) * PAIR_W] = fkb[:, p * PAIR_W:(p + 1) * PAIR_W]
        fkp_ref[0, :, (2 * p + 1) * PAIR_W:(2 * p + 2) * PAIR_W] = kb[:, p * PAIR_W:(p + 1) * PAIR_W]

    cv = tail[:, LANES:]
    cvb = cv[:, 0:CONV_CH]
    u = cv[:, CONV_CH:2 * CONV_CH] * cv[:, 2 * CONV_CH:3 * CONV_CH]
    uc = uc_ref[...]
    prev1 = uc[7:8]
    prev2 = uc[6:7]
    ridx = lax.broadcasted_iota(jnp.int32, u.shape, 0)
    u1 = jnp.where(ridx == 0, prev1, pltpu.roll(u, 1, 0))
    u2 = jnp.where(ridx == 0, prev2, jnp.where(ridx == 1, prev1, pltpu.roll(u, 2, 0)))
    cw = cw_ref[...]
    y = cw[0:1] * u2 + cw[1:2] * u1 + cw[2:3] * u
    conv_ref[0] = (cvb * y).astype(BF16)
    uc_ref[...] = u[tm - 8:tm]


def _inproj(x, g, wqkv, wtail, bfp, cw, tri, perm):
    B, S, D = x.shape
    tm = CHUNK
    const = lambda shape: pl.BlockSpec(shape, lambda b, s: (0,) * len(shape))
    row = lambda w: pl.BlockSpec((1, tm, w), lambda b, s: (b, s, 0))
    colmajor = lambda w: pl.BlockSpec((1, 1, w, tm), lambda b, s: (b, s, 0, 0))
    return pl.pallas_call(
        _inproj_kernel,
        grid=(B, S // tm),
        in_specs=[row(D), const((1, D)), const(wqkv.shape), const(wtail.shape),
                  const(bfp.shape), const(cw.shape), const(tri.shape), const(perm.shape)],
        out_specs=[colmajor(ATT_W), row(ATT_W), colmajor(ATT_W), row(2 * ATT_W),
                   pl.BlockSpec((1, 1, 2 * N_HEADS * HEAD_ROWS, tm), lambda b, s: (b, s, 0, 0)),
                   pl.BlockSpec((1, N_HEADS * MAX_BLOCKS, tm), lambda b, s: (b, 0, s)),
                   row(CONV_CH)],
        out_shape=[jax.ShapeDtypeStruct((B, S // tm, ATT_W, tm), BF16),
                   jax.ShapeDtypeStruct((B, S, ATT_W), BF16),
                   jax.ShapeDtypeStruct((B, S // tm, ATT_W, tm), BF16),
                   jax.ShapeDtypeStruct((B, S, 2 * ATT_W), BF16),
                   jax.ShapeDtypeStruct((B, S // tm, 2 * N_HEADS * HEAD_ROWS, tm), BF16),
                   jax.ShapeDtypeStruct((B, N_HEADS * MAX_BLOCKS, S), F32),
                   jax.ShapeDtypeStruct((B, S, CONV_CH), BF16)],
        scratch_shapes=[pltpu.VMEM((N_HEADS * MAX_BLOCKS, ATT_W), F32),
                        pltpu.VMEM((1, LANES), F32),
                        pltpu.VMEM((8, CONV_CH), F32)],
        compiler_params=pltpu.CompilerParams(
            dimension_semantics=("arbitrary", "arbitrary"), vmem_limit_bytes=VMEM_LIMIT),
        name="inproj",
    )(x, g, wqkv, wtail, bfp, cw, tri, perm)


def _attn_kernel(*refs, moba, nchunk):
    if moba:
        rb_ref, qt_ref, k_ref, vt_ref, sel_ref, own_ref, corner_ref, ind_ref, o_ref, *bufs = refs
        pair = pl.program_id(1)
        xrow = lax.broadcasted_iota(jnp.int32, (XROWS, CHUNK), 0)
        far_rows, near_fix = [], []
        for hh in range(2):
            fb = rb_ref[REL_BUCKETS - 1, 2 * pair + hh] * LOG2E
            terms = _split_bf16(jnp.full((XROWS, CHUNK), fb, F32), N_SPLIT)
            rows = jnp.zeros((XROWS, CHUNK), F32)
            for t in range(N_SPLIT):
                rows = jnp.where(xrow == BLOCKS_PER_CHUNK + t, terms[t].astype(F32), rows)
            far_rows.append(rows)
            near_fix.append(corner_ref[hh] - fb)
        xpad = jnp.zeros((PAIR_W - XROWS, CHUNK), BF16)
    else:
        qt_ref, k_ref, vt_ref, o_ref, *bufs = refs
    hidx = [jnp.minimum(pl.program_id(0), 0) + hh for hh in range(2)]

    def weights(qi):
        qt = qt_ref[0, qi]
        frow = lax.broadcasted_iota(jnp.int32, qt.shape, 0)
        zero = jnp.zeros_like(qt)
        ws = []
        for hh in range(2):
            w = jnp.where((frow >= hh * HEAD_DIM) & (frow < (hh + 1) * HEAD_DIM), qt, zero)
            if not moba:
                pick = (frow >= N_SPLIT * hh) & (frow < N_SPLIT * (hh + 1))
                w = jnp.concatenate([w, jnp.where(pick, 1.0, 0.0).astype(BF16)], axis=0)
            ws.append(w)
        return ws

    def query_norms(qi):
        q2 = qt_ref[0, qi].astype(F32) ** 2
        return [jnp.sqrt(jnp.sum(q2[hh * HEAD_DIM:(hh + 1) * HEAD_DIM], axis=0, keepdims=True))
                for hh in range(2)]

    def produce(qi, ws, c, dst_ref):
        own, near = c == qi, c == qi - 1
        kc = k_ref[0, c * CHUNK:(c + 1) * CHUNK, :]
        if moba:
            kc = jnp.concatenate([kc, ind_ref[...]], axis=1)
        maxima = []
        for hh in range(2):
            w = ws[hh]
            if moba:
                rows = jnp.zeros((XROWS, CHUNK), F32) if own else far_rows[hh]
                for r in range(BLOCKS_PER_CHUNK):
                    n = hh * MAX_BLOCKS + c * BLOCKS_PER_CHUNK + r
                    sel = sel_ref[0, n:n + 1, qi * CHUNK:(qi + 1) * CHUNK]
                    rows = jnp.where(xrow == r, sel, rows)
                w = jnp.concatenate([w, rows.astype(BF16), xpad], axis=0)
            if own:
                top = jnp.dot(kc[:BLK], w, preferred_element_type=F32)
                bot = jnp.dot(kc[BLK:], w[:, BLK:], preferred_element_type=F32)
                if moba:
                    top = top + own_ref[hh, :BLK, :]
                    bot = bot + own_ref[hh, BLK:, BLK:]
                else:
                    kidx = lax.broadcasted_iota(jnp.int32, (BLK, BLK), 0)
                    qidx = lax.broadcasted_iota(jnp.int32, (BLK, BLK), 1)
                    causal = kidx <= qidx
                    top = jnp.concatenate([jnp.where(causal, top[:, :BLK], NEG), top[:, BLK:]], axis=1)
                    bot = jnp.where(causal, bot, NEG)
                dst_ref[hidx[hh], :BLK, :] = top
                dst_ref[hidx[hh], BLK:, BLK:] = bot
                mt = jnp.max(top, axis=0, keepdims=True)
                mb = jnp.max(bot, axis=0, keepdims=True)
                maxima.append(jnp.concatenate([mt[:, :BLK], jnp.maximum(mt[:, BLK:], mb)], axis=1))
                continue
            s = jnp.dot(kc, w, preferred_element_type=F32)
            if moba and near:
                lo = CHUNK - CORNER
                fixed = jnp.concatenate([s[lo:, :CORNER] + near_fix[hh], s[lo:, CORNER:]], axis=1)
                s = jnp.concatenate([s[:lo], fixed], axis=0)
            dst_ref[hidx[hh]] = s
            maxima.append(jnp.max(s, axis=0, keepdims=True))
        return tuple(maxima)

    def consume(c, src_ref, maxima, state, own=False):
        new = []
        for hh in range(2):
            m_old, acc = state[hh]
            m_new = jnp.maximum(m_old, maxima[hh])
            vth = vt_ref[0, c, hh * HEAD_ROWS:(hh + 1) * HEAD_ROWS, :]
            if own:
                p_top = jnp.exp2(src_ref[hidx[hh], :BLK, :] - m_new).astype(BF16)
                p_bot = jnp.exp2(src_ref[hidx[hh], BLK:, BLK:] - m_new[:, BLK:]).astype(BF16)
                pv_top = jnp.dot(vth[:, :BLK], p_top, preferred_element_type=F32)
                pv_bot = jnp.dot(vth[:, BLK:], p_bot, preferred_element_type=F32)
                pv = jnp.concatenate([pv_top[:, :BLK], pv_top[:, BLK:] + pv_bot], axis=1)
            else:
                p = jnp.exp2(src_ref[hidx[hh]] - m_new).astype(BF16)
                pv = jnp.dot(vth, p, preferred_element_type=F32)
            new.append((m_new, jnp.exp2(m_old - m_new) * acc + pv))
        return tuple(new)

    def fresh_state():
        return tuple((jnp.full((1, CHUNK), NEG, F32), jnp.zeros((HEAD_ROWS, CHUNK), F32))
                     for _ in range(2))

    def finish(qi, state):
        ot = jnp.concatenate([acc[:HEAD_DIM] * (1.0 / acc[HEAD_DIM:HEAD_DIM + 1])
                              for (_, acc) in state], axis=0)
        o_ref[0, qi * CHUNK:(qi + 1) * CHUNK, :] = ot.T.astype(BF16)

    if moba:
        g = 0
        ws = weights(0)
        maxima = produce(0, ws, 0, bufs[0])
        for qi in range(nchunk):
            state = fresh_state()
            ws_next = weights(qi + 1) if qi + 1 < nchunk else None
            for c in range(qi, -1, -1):
                nxt = None
                if c > 0:
                    nxt = produce(qi, ws, c - 1, bufs[(g + 1) % 2])
                elif ws_next is not None:
                    nxt = produce(qi + 1, ws_next, qi + 1, bufs[(g + 1) % 2])
                state = consume(c, bufs[g % 2], maxima, state, own=(c == qi))
                maxima = nxt
                g += 1
            ws = ws_next
            finish(qi, state)
        return

    own_bufs, far_bufs = bufs[0:2], bufs[2:4]
    kall = k_ref[0][:, 0:PAIR_W].astype(F32) ** 2
    lane = lax.broadcasted_iota(jnp.int32, (1, PAIR_W), 1)
    key_norm = [jnp.sqrt(jnp.max(jnp.sum(jnp.where((lane >= hh * HEAD_DIM) & (lane < (hh + 1) * HEAD_DIM),
                                                      kall, 0.0), axis=1, keepdims=True)))
                for hh in range(2)]

    def gap(qn, decay_rows, hh, m, newer_queries_only):
        mine = (lane >= N_SPLIT * hh) & (lane < N_SPLIT * (hh + 1))
        newest = jnp.max(jnp.sum(jnp.where(mine, decay_rows, 0.0), axis=1, keepdims=True))
        bound = qn[hh] * key_norm[hh] + newest - m
        if newer_queries_only:
            qlane = lax.broadcasted_iota(jnp.int32, bound.shape, 1)
            bound = jnp.where(qlane >= BLK, bound, -jnp.inf)
        return jnp.max(bound)

    def only_band_matters(qi, state):
        base = (qi - 2) * CHUNK
        older_blk = k_ref[0, base + BLK - 16:base + BLK, PAIR_W:2 * PAIR_W].astype(F32)
        newer_blk = k_ref[0, base + CHUNK - 16:base + CHUNK, PAIR_W:2 * PAIR_W].astype(F32)
        qn = query_norms(qi)
        worst = None
        for hh in range(2):
            g = jnp.maximum(gap(qn, older_blk, hh, state[hh][0], False),
                            gap(qn, newer_blk, hh, state[hh][0], True))
            worst = g if worst is None else jnp.maximum(worst, g)
        return worst < -UNDERFLOW_BITS

    band_buf = far_bufs[1]

    def produce_band(qi, ws):
        c = qi - 2
        kq = k_ref[0, c * CHUNK + BLK:(c + 1) * CHUNK, :]
        maxima = []
        for hh in range(2):
            s = jnp.dot(kq, ws[hh][:, :BLK], preferred_element_type=F32)
            band_buf[hidx[hh], :BLK, :BLK] = s
            maxima.append(jnp.max(s, axis=0, keepdims=True))
        return tuple(maxima)

    def consume_band(qi, maxima, state):
        c = qi - 2
        new = []
        for hh in range(2):
            m_old, acc = state[hh]
            m_left = jnp.maximum(m_old[:, :BLK], maxima[hh])
            p = jnp.exp2(band_buf[hidx[hh], :BLK, :BLK] - m_left).astype(BF16)
            vth = vt_ref[0, c, hh * HEAD_ROWS:(hh + 1) * HEAD_ROWS, BLK:]
            left = jnp.exp2(m_old[:, :BLK] - m_left) * acc[:, :BLK] + jnp.dot(vth, p, preferred_element_type=F32)
            new.append((jnp.concatenate([m_left, m_old[:, BLK:]], axis=1),
                        jnp.concatenate([left, acc[:, BLK:]], axis=1)))
        return tuple(new)

    ws = weights(0)
    maxima = produce(0, ws, 0, own_bufs[0])
    for qi in range(nchunk):
        state = fresh_state()
        ws_next = weights(qi + 1) if qi + 1 < nchunk else None
        own_src, next_dst = own_bufs[qi % 2], own_bufs[(qi + 1) % 2]

        def next_own_scores(qi=qi, ws_next=ws_next, next_dst=next_dst):
            return produce(qi + 1, ws_next, qi + 1, next_dst)

        if qi < NEAR_CHUNKS:
            src = own_src
            for j, c in enumerate(range(qi, -1, -1)):
                if c > 0:
                    dst = far_bufs[j % 2]
                    nxt = produce(qi, ws, c - 1, dst)
                elif ws_next is not None:
                    next_own = next_own_scores()
                state = consume(c, src, maxima, state, own=(j == 0))
                if c > 0:
                    src, maxima = dst, nxt
        else:
            nxt = produce(qi, ws, qi - 1, far_bufs[0])
            state = consume(qi, own_src, maxima, state, own=True)
            band_mx = produce_band(qi, ws)
            state = consume(qi - 1, far_bufs[0], nxt, state)

            def band_only(st, qi=qi, band_mx=band_mx, ws_next=ws_next):
                own_mx = next_own_scores() if ws_next is not None else band_mx
                return consume_band(qi, band_mx, st), own_mx

            def everything(st, qi=qi, ws=ws, ws_next=ws_next, band_mx=band_mx):
                own_mx = next_own_scores() if ws_next is not None else band_mx
                first_far = qi - 2
                mx = produce(qi, ws, first_far, far_bufs[0])
                for i, c in enumerate(range(first_far, -1, -1)):
                    nx = produce(qi, ws, c - 1, far_bufs[(i + 1) % 2]) if c > 0 else None
                    st = consume(c, far_bufs[i % 2], mx, st)
                    mx = nx
                return st, own_mx

            state, next_own = lax.cond(only_band_matters(qi, state), band_only, everything, state)
        finish(qi, state)
        ws, maxima = ws_next, next_own


def _block_indicator():
    ind = np.zeros((CHUNK, PAIR_W), np.float32)
    for r in range(BLOCKS_PER_CHUNK):
        ind[r * BLK:(r + 1) * BLK, r] = 1.0
    ind[:, BLOCKS_PER_CHUNK:BLOCKS_PER_CHUNK + N_SPLIT] = 1.0
    return jnp.asarray(ind, BF16)


def _attention(qt, k, vt, sel=None, tables=None, rel_bias=None):
    moba = sel is not None
    B, S, _ = k.shape
    nchunk = S // CHUNK
    kw = k.shape[-1] // N_PAIRS
    vt_off = 0 if moba else N_PAIRS
    in_specs = [pl.BlockSpec((1, nchunk, PAIR_W, CHUNK), lambda b, p: (b, 0, p, 0)),
                pl.BlockSpec((1, S, kw), lambda b, p: (b, 0, p)),
                pl.BlockSpec((1, nchunk, PAIR_ROWS, CHUNK), lambda b, p: (b, 0, p + vt_off, 0))]
    args = [qt, k, vt]
    if moba:
        in_specs = [pl.BlockSpec(memory_space=pltpu.SMEM)] + in_specs
        in_specs += [pl.BlockSpec((1, 2 * MAX_BLOCKS, S), lambda b, p: (b, p, 0)),
                     pl.BlockSpec((2, CHUNK, CHUNK), lambda b, p: (p, 0, 0)),
                     pl.BlockSpec((2, CORNER, CORNER), lambda b, p: (p, 0, 0)),
                     pl.BlockSpec((CHUNK, PAIR_W), lambda b, p: (0, 0))]
        args = [rel_bias] + args + [sel, *tables, _block_indicator()]
    return pl.pallas_call(
        functools.partial(_attn_kernel, moba=moba, nchunk=nchunk),
        grid=(B, N_PAIRS),
        in_specs=in_specs,
        out_specs=pl.BlockSpec((1, S, PAIR_W), lambda b, p: (b, 0, p)),
        out_shape=jax.ShapeDtypeStruct((B, S, ATT_W), BF16),
        scratch_shapes=[pltpu.VMEM((2, CHUNK, CHUNK), F32)] * (2 if moba else 4),
        compiler_params=pltpu.CompilerParams(
            dimension_semantics=("arbitrary", "arbitrary"), vmem_limit_bytes=VMEM_LIMIT),
        name="moba_attn" if moba else "fox_attn",
    )(*args)


def _out_ffn_kernel(moba_ref, fox_ref, conv_ref, x_ref, wo_ref, wg_ref, wu_ref, wd_ref,
                    gpm_ref, gpf_ref, gqf_ref, o_ref):
    mix_in = jnp.concatenate([moba_ref[...], fox_ref[...], conv_ref[...]], axis=1)
    mixed = jnp.dot(mix_in, wo_ref[...], preferred_element_type=F32)
    x1 = x_ref[...] + _rms(mixed, gpm_ref[...])
    h2 = _rms(x1, gpf_ref[...]).astype(BF16)
    ff = jnp.zeros(x1.shape, F32)
    lo = 0
    for width in FFN_CHUNKS:
        gate = jnp.dot(h2, wg_ref[:, lo:lo + width], preferred_element_type=F32)
        up = jnp.dot(h2, wu_ref[:, lo:lo + width], preferred_element_type=F32)
        act = (gate * (1.0 / (1.0 + jnp.exp(-gate))) * up).astype(BF16)
        ff = ff + jnp.dot(act, wd_ref[lo:lo + width, :], preferred_element_type=F32)
        lo += width
    o_ref[...] = x1 + _rms(ff, gqf_ref[...])


def _out_ffn(moba, fox, conv, x, wo, wg, wu, wd, gpm, gpf, gqf):
    M, D = x.shape
    tm = FFN_TILE
    row = lambda w: pl.BlockSpec((tm, w), lambda m: (m, 0))
    const = lambda shape: pl.BlockSpec(shape, lambda m: (0, 0), pipeline_mode=pl.Buffered(1))
    return pl.pallas_call(
        _out_ffn_kernel,
        grid=(M // tm,),
        in_specs=[row(ATT_W), row(ATT_W), row(CONV_CH), row(D),
                  const(wo.shape), const(wg.shape), const(wu.shape), const(wd.shape),
                  const((1, D)), const((1, D)), const((1, D))],
        out_specs=row(D),
        out_shape=jax.ShapeDtypeStruct((M, D), F32),
        compiler_params=pltpu.CompilerParams(
            dimension_semantics=("arbitrary",), vmem_limit_bytes=VMEM_LIMIT),
        name="out_ffn",
    )(moba, fox, conv, x, wo, wg, wu, wd, gpm, gpf, gqf)


def _cumsum_constants(tm):
    tri = np.tril(np.ones((tm, tm), np.float32))
    perm = np.zeros((N_SPLIT, LANES, ATT_W), np.float32)
    for s in range(N_SPLIT):
        for h in range(N_HEADS):
            perm[s, h, PAIR_W * (h // 2) + N_SPLIT * (h % 2) + s] = 1.0
    return jnp.asarray(tri, BF16), jnp.asarray(perm, BF16)


def kernel(x, w_in, b_f, conv_w, w_out, rel_bias, g_pre_mix, g_post_mix, g_pre_ffn, g_post_ffn,
           w_gate, w_up, w_down):
    B, S, D = x.shape
    depth = w_in.shape[0]
    assert D == D_MODEL and S % CHUNK == 0 and S // BLK <= MAX_BLOCKS
    assert (B * S) % FFN_TILE == 0

    rel_bias = rel_bias.astype(F32)
    tables = _bias_tiles(rel_bias)
    w_in_t = jnp.swapaxes(w_in, 1, 2)
    tri, perm = _cumsum_constants(CHUNK)
    a = ATT_W
    for l in range(depth):
        wl = w_in_t[l]
        wqkv = wl[0:6 * a].astype(BF16)
        wtail = jnp.concatenate([wl[6 * a:6 * a + N_HEADS], jnp.zeros((LANES - N_HEADS, D), wl.dtype),
                                 wl[6 * a + N_HEADS:]], axis=0).astype(BF16)
        bfp = jnp.pad(b_f[l].astype(F32), (0, LANES - N_HEADS)).reshape(1, LANES)
        row = lambda g: g[l].astype(F32).reshape(1, D)

        mq, mk, fq, fkp, vt, sel, conv = _inproj(
            x, row(g_pre_mix), wqkv, wtail, bfp, conv_w[l].astype(F32), tri, perm)
        moba = _attention(mq, mk, vt, sel, tables, rel_bias)
        fox = _attention(fq, fkp, vt)
        x = _out_ffn(moba.reshape(B * S, a), fox.reshape(B * S, a), conv.reshape(B * S, CONV_CH),
                     x.reshape(B * S, D), w_out[l].astype(BF16), w_gate[l].astype(BF16),
                     w_up[l].astype(BF16), w_down[l].astype(BF16),
                     row(g_post_mix), row(g_pre_ffn), row(g_post_ffn)).reshape(B, S, D)
    return x
```

```python
import functools
import math

import jax
import jax.numpy as jnp
import numpy as np
from jax import lax
from jax.experimental import pallas as pl
from jax.experimental.pallas import tpu as pltpu

D_MODEL = 1024
HEAD_DIM = 64
N_HEADS = 6
ATT_W = N_HEADS * HEAD_DIM
PAIR_W = 2 * HEAD_DIM
N_PAIRS = N_HEADS // 2
CONV_CH = 256
CONV_WIDTH = 3
BLK = 256
BLK_SHIFT = BLK.bit_length() - 1
CHUNK = 512
BLOCKS_PER_CHUNK = CHUNK // BLK
assert BLOCKS_PER_CHUNK == 2
MAX_BLOCKS = 16
TOPK = 3
REL_BUCKETS = 32
REL_MAX_DIST = 128
D_FF = 2816
RMS_EPS = 1e-6
NEG = -1e30
LANES = 128
N_SPLIT = 3
LOG2E = math.log2(math.e)
CORNER = REL_MAX_DIST
ONES_ROWS = 16
HEAD_ROWS = HEAD_DIM + ONES_ROWS
PAIR_ROWS = 2 * HEAD_ROWS
XROWS = 16
NEAR_CHUNKS = 2
UNDERFLOW_BITS = 152.0

F32 = jnp.float32
BF16 = jnp.bfloat16
NT_DIMS = (((1,), (1,)), ((), ()))

FFN_TILE = 512
FFN_CHUNKS = (1536, 1280)
assert sum(FFN_CHUNKS) == D_FF
VMEM_LIMIT = 50 * 1024 * 1024


def _split_bf16(v, n):
    terms = []
    for _ in range(n):
        t = v.astype(BF16)
        terms.append(t)
        v = v - t.astype(F32)
    return terms


def _rms(v, g):
    return v * lax.rsqrt(jnp.mean(v * v, axis=-1, keepdims=True) + RMS_EPS) * g


def _bucket_tiles():
    max_exact = REL_BUCKETS // 2

    def bucket(dist):
        n = np.maximum(dist, 0)
        nf = np.maximum(n, 1).astype(np.float32)
        large = max_exact + (np.log(nf / np.float32(max_exact))
                             / np.float32(math.log(REL_MAX_DIST / max_exact))
                             * np.float32(REL_BUCKETS - max_exact)).astype(np.int32)
        large = np.minimum(large, REL_BUCKETS - 1)
        return np.where(n < max_exact, n, large).astype(np.int32)

    key = np.arange(CHUNK, dtype=np.int32)[:, None]
    qry = np.arange(CHUNK, dtype=np.int32)[None, :]
    d0 = qry - key
    own = np.where(d0 >= 0, bucket(d0), -1).astype(np.int32)
    prev = bucket(d0 + CHUNK)
    corner = prev[CHUNK - CORNER:, :CORNER]
    outside = prev.copy()
    outside[CHUNK - CORNER:, :CORNER] = REL_BUCKETS - 1
    assert (outside == REL_BUCKETS - 1).all()
    return own, np.ascontiguousarray(corner)


def _bias_tile_kernel(rb_ref, own_bm_ref, corner_bm_ref, own_ref, corner_ref):
    h = pl.program_id(0)
    for bm_ref, out_ref in ((own_bm_ref, own_ref), (corner_bm_ref, corner_ref)):
        bm = bm_ref[...]
        t = jnp.full(bm.shape, NEG, F32)
        for b in range(REL_BUCKETS):
            t = jnp.where(bm == b, rb_ref[b, h] * LOG2E, t)
        out_ref[0] = t


def _bias_tiles(rel_bias):
    own_bm, corner_bm = _bucket_tiles()
    return pl.pallas_call(
        _bias_tile_kernel,
        grid=(N_HEADS,),
        in_specs=[pl.BlockSpec(memory_space=pltpu.SMEM),
                  pl.BlockSpec((CHUNK, CHUNK), lambda h: (0, 0)),
                  pl.BlockSpec((CORNER, CORNER), lambda h: (0, 0))],
        out_specs=[pl.BlockSpec((1, CHUNK, CHUNK), lambda h: (h, 0, 0)),
                   pl.BlockSpec((1, CORNER, CORNER), lambda h: (h, 0, 0))],
        out_shape=[jax.ShapeDtypeStruct((N_HEADS, CHUNK, CHUNK), F32),
                   jax.ShapeDtypeStruct((N_HEADS, CORNER, CORNER), F32)],
        name="bias_tiles",
    )(rel_bias, jnp.asarray(own_bm), jnp.asarray(corner_bm))


def _inproj_kernel(x_ref, g_ref, wqkv_ref, wtail_ref, bf_ref, cw_ref, tri_ref, perm_ref,
                   mq_ref, mk_ref, fq_ref, fkp_ref, vt_ref, sel_ref, conv_ref,
                   km_ref, cum_ref, uc_ref):
    tm = CHUNK
    st = pl.program_id(1)

    @pl.when(st == 0)
    def _():
        km_ref[...] = jnp.zeros_like(km_ref)
        cum_ref[...] = jnp.zeros_like(cum_ref)
        uc_ref[...] = jnp.zeros_like(uc_ref)

    hn = _rms(x_ref[0], g_ref[...]).astype(BF16)

    qkv = lax.dot_general(hn, wqkv_ref[...], NT_DIMS, preferred_element_type=F32)
    tail = lax.dot_general(hn, wtail_ref[...], NT_DIMS, preferred_element_type=F32)
    mq = qkv[:, 0 * ATT_W:1 * ATT_W]
    mk = qkv[:, 1 * ATT_W:2 * ATT_W]
    mv = qkv[:, 2 * ATT_W:3 * ATT_W]
    fq = qkv[:, 3 * ATT_W:4 * ATT_W]
    fk = qkv[:, 4 * ATT_W:5 * ATT_W]
    fv = qkv[:, 5 * ATT_W:6 * ATT_W]
    scale = HEAD_DIM ** -0.5 * LOG2E
    mq_ref[0, 0] = (mq * scale).T.astype(BF16)
    mk_ref[0] = mk.astype(BF16)
    fq_ref[0, 0] = (fq * scale).T.astype(BF16)
    ones = jnp.ones((ONES_ROWS, tm), BF16)
    for kind, v in enumerate((mv, fv)):
        vt = v.T.astype(BF16)
        for h in range(N_HEADS):
            base = (kind * N_HEADS + h) * HEAD_ROWS
            vt_ref[0, 0, base:base + HEAD_DIM, :] = vt[h * HEAD_DIM:(h + 1) * HEAD_DIM]
            vt_ref[0, 0, base + HEAD_DIM:base + HEAD_ROWS, :] = ones

    rows = lax.broadcasted_iota(jnp.int32, km_ref.shape, 0)
    cols = lax.broadcasted_iota(jnp.int32, km_ref.shape, 1)
    same_head = (rows >> 4) == (cols >> 6)
    km = km_ref[...]
    for r in range(BLOCKS_PER_CHUNK):
        kmean = jnp.mean(mk[r * BLK:(r + 1) * BLK], axis=0, keepdims=True)
        n = st * BLOCKS_PER_CHUNK + r
        km = jnp.where(same_head & ((rows & (MAX_BLOCKS - 1)) == n), kmean, km)
    km_ref[...] = km

    km_hi, km_lo = _split_bf16(km, 2)
    q_hi, q_lo = _split_bf16(mq, 2)
    gate = (lax.dot_general(km_hi, q_hi, NT_DIMS, preferred_element_type=F32)
            + lax.dot_general(km_hi, q_lo, NT_DIMS, preferred_element_type=F32)
            + lax.dot_general(km_lo, q_hi, NT_DIMS, preferred_element_type=F32))

    nidx = lax.broadcasted_iota(jnp.int32, (MAX_BLOCKS, tm), 0)
    own = st * BLOCKS_PER_CHUNK + (lax.broadcasted_iota(jnp.int32, (MAX_BLOCKS, tm), 1) >> BLK_SHIFT)
    valid = nidx < own
    for h in range(N_HEADS):
        gh = jnp.where(valid, gate[h * MAX_BLOCKS:(h + 1) * MAX_BLOCKS], -jnp.inf)
        rank = jnp.zeros((MAX_BLOCKS, tm), jnp.int32)
        for m in range(MAX_BLOCKS):
            gm = gh[m:m + 1, :]
            gt = (gm > gh).astype(jnp.int32)
            ge = (gm >= gh).astype(jnp.int32)
            rank = rank + jnp.where(nidx > m, ge, gt)
        keep = ((rank < TOPK) & valid) | (nidx == own)
        sel_ref[0, h * MAX_BLOCKS:(h + 1) * MAX_BLOCKS, :] = jnp.where(keep, 0.0, NEG)

    fl = tail[:, 0:LANES] + bf_ref[...]
    lf = jnp.minimum(fl, 0.0) - jnp.log1p(jnp.exp(-jnp.abs(fl)))
    lf_terms = jnp.concatenate(_split_bf16(lf, N_SPLIT), axis=1)
    tri = tri_ref[...]
    carry = cum_ref[...]
    groups = []
    for r in range(BLOCKS_PER_CHUNK):
        part = jnp.dot(tri, lf_terms[r * BLK:(r + 1) * BLK], preferred_element_type=F32)
        grp = carry + sum(part[:, t * LANES:(t + 1) * LANES] for t in range(N_SPLIT))
        carry = grp[BLK - 1:BLK, :]
        groups.append(grp)
    cum = jnp.concatenate(groups, axis=0)
    cum_ref[...] = carry
    neg_terms = jnp.concatenate(_split_bf16(cum * -LOG2E, N_SPLIT), axis=1)
    kb = jnp.dot(neg_terms, perm_ref[...], preferred_element_type=F32).astype(BF16)
    fkb = fk.astype(BF16)
    for p in range(N_PAIRS):
        fkp_ref[0, :, 2 * p * PAIR_W:(2 * p + 1) * PAIR_W] = fkb[:, p * PAIR_W:(p + 1) * PAIR_W]
        fkp_ref[0, :, (2 * p + 1) * PAIR_W:(2 * p + 2) * PAIR_W] = kb[:, p * PAIR_W:(p + 1) * PAIR_W]

    cv = tail[:, LANES:]
    cvb = cv[:, 0:CONV_CH]
    u = cv[:, CONV_CH:2 * CONV_CH] * cv[:, 2 * CONV_CH:3 * CONV_CH]
    uc = uc_ref[...]
    prev1 = uc[7:8]
    prev2 = uc[6:7]
    ridx = lax.broadcasted_iota(jnp.int32, u.shape, 0)
    u1 = jnp.where(ridx == 0, prev1, pltpu.roll(u, 1, 0))
    u2 = jnp.where(ridx == 0, prev2, jnp.where(ridx == 1, prev1, pltpu.roll(u, 2, 0)))
    cw = cw_ref[...]
    y = cw[0:1] * u2 + cw[1:2] * u1 + cw[2:3] * u
    conv_ref[0] = (cvb * y).astype(BF16)
    uc_ref[...] = u[tm - 8:tm]


def _inproj(x, g, wqkv, wtail, bfp, cw, tri, perm):
    B, S, D = x.shape
    tm = CHUNK
    const = lambda shape: pl.BlockSpec(shape, lambda b, s: (0,) * len(shape))
    row = lambda w: pl.BlockSpec((1, tm, w), lambda b, s: (b, s, 0))
    colmajor = lambda w: pl.BlockSpec((1, 1, w, tm), lambda b, s: (b, s, 0, 0))
    return pl.pallas_call(
        _inproj_kernel,
        grid=(B, S // tm),
        in_specs=[row(D), const((1, D)), const(wqkv.shape), const(wtail.shape),
                  const(bfp.shape), const(cw.shape), const(tri.shape), const(perm.shape)],
        out_specs=[colmajor(ATT_W), row(ATT_W), colmajor(ATT_W), row(2 * ATT_W),
                   pl.BlockSpec((1, 1, 2 * N_HEADS * HEAD_ROWS, tm), lambda b, s: (b, s, 0, 0)),
                   pl.BlockSpec((1, N_HEADS * MAX_BLOCKS, tm), lambda b, s: (b, 0, s)),
                   row(CONV_CH)],
        out_shape=[jax.ShapeDtypeStruct((B, S // tm, ATT_W, tm), BF16),
                   jax.ShapeDtypeStruct((B, S, ATT_W), BF16),
                   jax.ShapeDtypeStruct((B, S // tm, ATT_W, tm), BF16),
                   jax.ShapeDtypeStruct((B, S, 2 * ATT_W), BF16),
                   jax.ShapeDtypeStruct((B, S // tm, 2 * N_HEADS * HEAD_ROWS, tm), BF16),
                   jax.ShapeDtypeStruct((B, N_HEADS * MAX_BLOCKS, S), F32),
                   jax.ShapeDtypeStruct((B, S, CONV_CH), BF16)],
        scratch_shapes=[pltpu.VMEM((N_HEADS * MAX_BLOCKS, ATT_W), F32),
                        pltpu.VMEM((1, LANES), F32),
                        pltpu.VMEM((8, CONV_CH), F32)],
        compiler_params=pltpu.CompilerParams(
            dimension_semantics=("arbitrary", "arbitrary"), vmem_limit_bytes=VMEM_LIMIT),
        name="inproj",
    )(x, g, wqkv, wtail, bfp, cw, tri, perm)


def _attn_kernel(*refs, moba, nchunk):
    if moba:
        rb_ref, qt_ref, k_ref, vt_ref, sel_ref, own_ref, corner_ref, ind_ref, o_ref, *bufs = refs
        pair = pl.program_id(1)
        xrow = lax.broadcasted_iota(jnp.int32, (XROWS, CHUNK), 0)
        far_rows, near_fix = [], []
        for hh in range(2):
            fb = rb_ref[REL_BUCKETS - 1, 2 * pair + hh] * LOG2E
            terms = _split_bf16(jnp.full((XROWS, CHUNK), fb, F32), N_SPLIT)
            rows = jnp.zeros((XROWS, CHUNK), F32)
            for t in range(N_SPLIT):
                rows = jnp.where(xrow == BLOCKS_PER_CHUNK + t, terms[t].astype(F32), rows)
            far_rows.append(rows)
            near_fix.append(corner_ref[hh] - fb)
        xpad = jnp.zeros((PAIR_W - XROWS, CHUNK), BF16)
    else:
        qt_ref, k_ref, vt_ref, o_ref, *bufs = refs
    hidx = [jnp.minimum(pl.program_id(0), 0) + hh for hh in range(2)]

    def weights(qi):
        qt = qt_ref[0, qi]
        frow = lax.broadcasted_iota(jnp.int32, qt.shape, 0)
        zero = jnp.zeros_like(qt)
        ws = []
        for hh in range(2):
            w = jnp.where((frow >= hh * HEAD_DIM) & (frow < (hh + 1) * HEAD_DIM), qt, zero)
            if not moba:
                pick = (frow >= N_SPLIT * hh) & (frow < N_SPLIT * (hh + 1))
                w = jnp.concatenate([w, jnp.where(pick, 1.0, 0.0).astype(BF16)], axis=0)
            ws.append(w)
        return ws

    def query_norms(qi):
        q2 = qt_ref[0, qi].astype(F32) ** 2
        return [jnp.sqrt(jnp.sum(q2[hh * HEAD_DIM:(hh + 1) * HEAD_DIM], axis=0, keepdims=True))
                for hh in range(2)]

    def produce(qi, ws, c, dst_ref):
        own, near = c == qi, c == qi - 1
        kc = k_ref[0, c * CHUNK:(c + 1) * CHUNK, :]
        if moba:
            kc = jnp.concatenate([kc, ind_ref[...]], axis=1)
        maxima = []
        for hh in range(2):
            w = ws[hh]
            if moba:
                rows = jnp.zeros((XROWS, CHUNK), F32) if own else far_rows[hh]
                for r in range(BLOCKS_PER_CHUNK):
                    n = hh * MAX_BLOCKS + c * BLOCKS_PER_CHUNK + r
                    sel = sel_ref[0, n:n + 1, qi * CHUNK:(qi + 1) * CHUNK]
                    rows = jnp.where(xrow == r, sel, rows)
                w = jnp.concatenate([w, rows.astype(BF16), xpad], axis=0)
            if own:
                top = jnp.dot(kc[:BLK], w, preferred_element_type=F32)
                bot = jnp.dot(kc[BLK:], w[:, BLK:], preferred_element_type=F32)
                if moba:
                    top = top + own_ref[hh, :BLK, :]
                    bot = bot + own_ref[hh, BLK:, BLK:]
                else:
                    kidx = lax.broadcasted_iota(jnp.int32, (BLK, BLK), 0)
                    qidx = lax.broadcasted_iota(jnp.int32, (BLK, BLK), 1)
                    causal = kidx <= qidx
                    top = jnp.concatenate([jnp.where(causal, top[:, :BLK], NEG), top[:, BLK:]], axis=1)
                    bot = jnp.where(causal, bot, NEG)
                dst_ref[hidx[hh], :BLK, :] = top
                dst_ref[hidx[hh], BLK:, BLK:] = bot
                mt = jnp.max(top, axis=0, keepdims=True)
                mb = jnp.max(bot, axis=0, keepdims=True)
                maxima.append(jnp.concatenate([mt[:, :BLK], jnp.maximum(mt[:, BLK:], mb)], axis=1))
                continue
            s = jnp.dot(kc, w, preferred_element_type=F32)
            if moba and near:
                lo = CHUNK - CORNER
                fixed = jnp.concatenate([s[lo:, :CORNER] + near_fix[hh], s[lo:, CORNER:]], axis=1)
                s = jnp.concatenate([s[:lo], fixed], axis=0)
            dst_ref[hidx[hh]] = s
            maxima.append(jnp.max(s, axis=0, keepdims=True))
        return tuple(maxima)

    def consume(c, src_ref, maxima, state, own=False):
        new = []
        for hh in range(2):
            m_old, acc = state[hh]
            m_new = jnp.maximum(m_old, maxima[hh])
            vth = vt_ref[0, c, hh * HEAD_ROWS:(hh + 1) * HEAD_ROWS, :]
            if own:
                p_top = jnp.exp2(src_ref[hidx[hh], :BLK, :] - m_new).astype(BF16)
                p_bot = jnp.exp2(src_ref[hidx[hh], BLK:, BLK:] - m_new[:, BLK:]).astype(BF16)
                pv_top = jnp.dot(vth[:, :BLK], p_top, preferred_element_type=F32)
                pv_bot = jnp.dot(vth[:, BLK:], p_bot, preferred_element_type=F32)
                pv = jnp.concatenate([pv_top[:, :BLK], pv_top[:, BLK:] + pv_bot], axis=1)
            else:
                p = jnp.exp2(src_ref[hidx[hh]] - m_new).astype(BF16)
                pv = jnp.dot(vth, p, preferred_element_type=F32)
            new.append((m_new, jnp.exp2(m_old - m_new) * acc + pv))
        return tuple(new)

    def fresh_state():
        return tuple((jnp.full((1, CHUNK), NEG, F32), jnp.zeros((HEAD_ROWS, CHUNK), F32))
                     for _ in range(2))

    def finish(qi, state):
        ot = jnp.concatenate([acc[:HEAD_DIM] * (1.0 / acc[HEAD_DIM:HEAD_DIM + 1])
                              for (_, acc) in state], axis=0)
        o_ref[0, qi * CHUNK:(qi + 1) * CHUNK, :] = ot.T.astype(BF16)

    if moba:
        g = 0
        ws = weights(0)
        maxima = produce(0, ws, 0, bufs[0])
        for qi in range(nchunk):
            state = fresh_state()
            ws_next = weights(qi + 1) if qi + 1 < nchunk else None
            for c in range(qi, -1, -1):
                nxt = None
                if c > 0:
                    nxt = produce(qi, ws, c - 1, bufs[(g + 1) % 2])
                elif ws_next is not None:
                    nxt = produce(qi + 1, ws_next, qi + 1, bufs[(g + 1) % 2])
                state = consume(c, bufs[g % 2], maxima, state, own=(c == qi))
                maxima = nxt
                g += 1
            ws = ws_next
            finish(qi, state)
        return

    own_bufs, far_bufs = bufs[0:2], bufs[2:4]
    kall = k_ref[0][:, 0:PAIR_W].astype(F32) ** 2
    lane = lax.broadcasted_iota(jnp.int32, (1, PAIR_W), 1)
    key_norm = [jnp.sqrt(jnp.max(jnp.sum(jnp.where((lane >= hh * HEAD_DIM) & (lane < (hh + 1) * HEAD_DIM),
                                                      kall, 0.0), axis=1, keepdims=True)))
                for hh in range(2)]

    def gap(qn, decay_rows, hh, m, newer_queries_only):
        mine = (lane >= N_SPLIT * hh) & (lane < N_SPLIT * (hh + 1))
        newest = jnp.max(jnp.sum(jnp.where(mine, decay_rows, 0.0), axis=1, keepdims=True))
        bound = qn[hh] * key_norm[hh] + newest - m
        if newer_queries_only:
            qlane = lax.broadcasted_iota(jnp.int32, bound.shape, 1)
            bound = jnp.where(qlane >= BLK, bound, -jnp.inf)
        return jnp.max(bound)

    def only_band_matters(qi, state):
        base = (qi - 2) * CHUNK
        older_blk = k_ref[0, base + BLK - 16:base + BLK, PAIR_W:2 * PAIR_W].astype(F32)
        newer_blk = k_ref[0, base + CHUNK - 16:base + CHUNK, PAIR_W:2 * PAIR_W].astype(F32)
        qn = query_norms(qi)
        worst = None
        for hh in range(2):
            g = jnp.maximum(gap(qn, older_blk, hh, state[hh][0], False),
                            gap(qn, newer_blk, hh, state[hh][0], True))
            worst = g if worst is None else jnp.maximum(worst, g)
        return worst < -UNDERFLOW_BITS

    band_buf = far_bufs[1]

    def produce_band(qi, ws):
        c = qi - 2
        kq = k_ref[0, c * CHUNK + BLK:(c + 1) * CHUNK, :]
        maxima = []
        for hh in range(2):
            s = jnp.dot(kq, ws[hh][:, :BLK], preferred_element_type=F32)
            band_buf[hidx[hh], :BLK, :BLK] = s
            maxima.append(jnp.max(s, axis=0, keepdims=True))
        return tuple(maxima)

    def consume_band(qi, maxima, state):
        c = qi - 2
        new = []
        for hh in range(2):
            m_old, acc = state[hh]
            m_left = jnp.maximum(m_old[:, :BLK], maxima[hh])
            p = jnp.exp2(band_buf[hidx[hh], :BLK, :BLK] - m_left).astype(BF16)
            vth = vt_ref[0, c, hh * HEAD_ROWS:(hh + 1) * HEAD_ROWS, BLK:]
            left = jnp.exp2(m_old[:, :BLK] - m_left) * acc[:, :BLK] + jnp.dot(vth, p, preferred_element_type=F32)
            new.append((jnp.concatenate([m_left, m_old[:, BLK:]], axis=1),
                        jnp.concatenate([left, acc[:, BLK:]], axis=1)))
        return tuple(new)

    ws = weights(0)
    maxima = produce(0, ws, 0, own_bufs[0])
    for qi in range(nchunk):
        state = fresh_state()
        ws_next = weights(qi + 1) if qi + 1 < nchunk else None
        own_src, next_dst = own_bufs[qi % 2], own_bufs[(qi + 1) % 2]

        def next_own_scores(qi=qi, ws_next=ws_next, next_dst=next_dst):
            return produce(qi + 1, ws_next, qi + 1, next_dst)

        if qi < NEAR_CHUNKS:
            src = own_src
            for j, c in enumerate(range(qi, -1, -1)):
                if c > 0:
                    dst = far_bufs[j % 2]
                    nxt = produce(qi, ws, c - 1, dst)
                elif ws_next is not None:
                    next_own = next_own_scores()
                state = consume(c, src, maxima, state, own=(j == 0))
                if c > 0:
                    src, maxima = dst, nxt
        else:
            nxt = produce(qi, ws, qi - 1, far_bufs[0])
            state = consume(qi, own_src, maxima, state, own=True)
            band_mx = produce_band(qi, ws)
            state = consume(qi - 1, far_bufs[0], nxt, state)

            def band_only(st, qi=qi, band_mx=band_mx, ws_next=ws_next):
                own_mx = next_own_scores() if ws_next is not None else band_mx
                return consume_band(qi, band_mx, st), own_mx

            def everything(st, qi=qi, ws=ws, ws_next=ws_next, band_mx=band_mx):
                own_mx = next_own_scores() if ws_next is not None else band_mx
                first_far = qi - 2
                mx = produce(qi, ws, first_far, far_bufs[0])
                for i, c in enumerate(range(first_far, -1, -1)):
                    nx = produce(qi, ws, c - 1, far_bufs[(i + 1) % 2]) if c > 0 else None
                    st = consume(c, far_bufs[i % 2], mx, st)
                    mx = nx
                return st, own_mx

            state, next_own = lax.cond(only_band_matters(qi, state), band_only, everything, state)
        finish(qi, state)
        ws, maxima = ws_next, next_own


def _block_indicator():
    ind = np.zeros((CHUNK, PAIR_W), np.float32)
    for r in range(BLOCKS_PER_CHUNK):
        ind[r * BLK:(r + 1) * BLK, r] = 1.0
    ind[:, BLOCKS_PER_CHUNK:BLOCKS_PER_CHUNK + N_SPLIT] = 1.0
    return jnp.asarray(ind, BF16)


def _attention(qt, k, vt, sel=None, tables=None, rel_bias=None):
    moba = sel is not None
    B, S, _ = k.shape
    nchunk = S // CHUNK
    kw = k.shape[-1] // N_PAIRS
    vt_off = 0 if moba else N_PAIRS
    in_specs = [pl.BlockSpec((1, nchunk, PAIR_W, CHUNK), lambda b, p: (b, 0, p, 0)),
                pl.BlockSpec((1, S, kw), lambda b, p: (b, 0, p)),
                pl.BlockSpec((1, nchunk, PAIR_ROWS, CHUNK), lambda b, p: (b, 0, p + vt_off, 0))]
    args = [qt, k, vt]
    if moba:
        in_specs = [pl.BlockSpec(memory_space=pltpu.SMEM)] + in_specs
        in_specs += [pl.BlockSpec((1, 2 * MAX_BLOCKS, S), lambda b, p: (b, p, 0)),
                     pl.BlockSpec((2, CHUNK, CHUNK), lambda b, p: (p, 0, 0)),
                     pl.BlockSpec((2, CORNER, CORNER), lambda b, p: (p, 0, 0)),
                     pl.BlockSpec((CHUNK, PAIR_W), lambda b, p: (0, 0))]
        args = [rel_bias] + args + [sel, *tables, _block_indicator()]
    return pl.pallas_call(
        functools.partial(_attn_kernel, moba=moba, nchunk=nchunk),
        grid=(B, N_PAIRS),
        in_specs=in_specs,
        out_specs=pl.BlockSpec((1, S, PAIR_W), lambda b, p: (b, 0, p)),
        out_shape=jax.ShapeDtypeStruct((B, S, ATT_W), BF16),
        scratch_shapes=[pltpu.VMEM((2, CHUNK, CHUNK), F32)] * (2 if moba else 4),
        compiler_params=pltpu.CompilerParams(
            dimension_semantics=("arbitrary", "arbitrary"), vmem_limit_bytes=VMEM_LIMIT),
        name="moba_attn" if moba else "fox_attn",
    )(*args)


def _out_ffn_kernel(moba_ref, fox_ref, conv_ref, x_ref, wo_ref, wg_ref, wu_ref, wd_ref,
                    gpm_ref, gpf_ref, gqf_ref, o_ref):
    mix_in = jnp.concatenate([moba_ref[...], fox_ref[...], conv_ref[...]], axis=1)
    mixed = jnp.dot(mix_in, wo_ref[...], preferred_element_type=F32)
    x1 = x_ref[...] + _rms(mixed, gpm_ref[...])
    h2 = _rms(x1, gpf_ref[...]).astype(BF16)
    ff = jnp.zeros(x1.shape, F32)
    lo = 0
    for width in FFN_CHUNKS:
        gate = jnp.dot(h2, wg_ref[:, lo:lo + width], preferred_element_type=F32)
        up = jnp.dot(h2, wu_ref[:, lo:lo + width], preferred_element_type=F32)
        act = (gate * (1.0 / (1.0 + jnp.exp(-gate))) * up).astype(BF16)
        ff = ff + jnp.dot(act, wd_ref[lo:lo + width, :], preferred_element_type=F32)
        lo += width
    o_ref[...] = x1 + _rms(ff, gqf_ref[...])


def _out_ffn(moba, fox, conv, x, wo, wg, wu, wd, gpm, gpf, gqf):
    M, D = x.shape
    tm = FFN_TILE
    row = lambda w: pl.BlockSpec((tm, w), lambda m: (m, 0))
    const = lambda shape: pl.BlockSpec(shape, lambda m: (0, 0), pipeline_mode=pl.Buffered(1))
    return pl.pallas_call(
        _out_ffn_kernel,
        grid=(M // tm,),
        in_specs=[row(ATT_W), row(ATT_W), row(CONV_CH), row(D),
                  const(wo.shape), const(wg.shape), const(wu.shape), const(wd.shape),
                  const((1, D)), const((1, D)), const((1, D))],
        out_specs=row(D),
        out_shape=jax.ShapeDtypeStruct((M, D), F32),
        compiler_params=pltpu.CompilerParams(
            dimension_semantics=("arbitrary",), vmem_limit_bytes=VMEM_LIMIT),
        name="out_ffn",
    )(moba, fox, conv, x, wo, wg, wu, wd, gpm, gpf, gqf)


def _cumsum_constants():
    tri = np.tril(np.ones((BLK, BLK), np.float32))
    perm = np.zeros((N_SPLIT * LANES, ATT_W), np.float32)
    for s in range(N_SPLIT):
        for h in range(N_HEADS):
            perm[s * LANES + h, PAIR_W * (h // 2) + N_SPLIT * (h % 2) + s] = 1.0
    return jnp.asarray(tri, BF16), jnp.asarray(perm, BF16)


def kernel(x, w_in, b_f, conv_w, w_out, rel_bias, g_pre_mix, g_post_mix, g_pre_ffn, g_post_ffn,
           w_gate, w_up, w_down):
    B, S, D = x.shape
    depth = w_in.shape[0]
    assert D == D_MODEL and S % CHUNK == 0 and S // BLK <= MAX_BLOCKS
    assert (B * S) % FFN_TILE == 0

    rel_bias = rel_bias.astype(F32)
    tables = _bias_tiles(rel_bias)
    w_in_t = jnp.swapaxes(w_in, 1, 2)
    tri, perm = _cumsum_constants()
    a = ATT_W
    for l in range(depth):
        wl = w_in_t[l]
        wqkv = wl[0:6 * a].astype(BF16)
        wtail = jnp.concatenate([wl[6 * a:6 * a + N_HEADS], jnp.zeros((LANES - N_HEADS, D), wl.dtype),
                                 wl[6 * a + N_HEADS:]], axis=0).astype(BF16)
        bfp = jnp.pad(b_f[l].astype(F32), (0, LANES - N_HEADS)).reshape(1, LANES)
        row = lambda g: g[l].astype(F32).reshape(1, D)

        mq, mk, fq, fkp, vt, sel, conv = _inproj(
            x, row(g_pre_mix), wqkv, wtail, bfp, conv_w[l].astype(F32), tri, perm)
        moba = _attention(mq, mk, vt, sel, tables, rel_bias)
        fox = _attention(fq, fkp, vt)
        x = _out_ffn(moba.reshape(B * S, a), fox.reshape(B * S, a), conv.reshape(B * S, CONV_CH),
                     x.reshape(B * S, D), w_out[l].astype(BF16), w_gate[l].astype(BF16),
                     w_up[l].astype(BF16), w_down[l].astype(BF16),
                     row(g_post_mix), row(g_pre_ffn), row(g_post_ffn)).reshape(B, S, D)
    return x
```

```python
import functools
import math

import jax
import jax.numpy as jnp
import numpy as np
from jax import lax
from jax.experimental import pallas as pl
from jax.experimental.pallas import tpu as pltpu

D_MODEL = 1024
HEAD_DIM = 64
N_HEADS = 6
ATT_W = N_HEADS * HEAD_DIM
PAIR_W = 2 * HEAD_DIM
N_PAIRS = N_HEADS // 2
CONV_CH = 256
CONV_WIDTH = 3
BLK = 256
BLK_SHIFT = BLK.bit_length() - 1
CHUNK = 512
BLOCKS_PER_CHUNK = CHUNK // BLK
assert BLOCKS_PER_CHUNK == 2
MAX_BLOCKS = 16
MAX_BLOCKS_SHIFT = MAX_BLOCKS.bit_length() - 1
HEAD_SHIFT = HEAD_DIM.bit_length() - 1
BF16_TILE_ROWS = 16
TOPK = 3
REL_BUCKETS = 32
REL_MAX_DIST = 128
D_FF = 2816
RMS_EPS = 1e-6
NEG = -1e30
LANES = 128
SUBLANES = 8
N_SPLIT = 3
LOG2E = math.log2(math.e)
CORNER = REL_MAX_DIST
ONES_ROWS = BF16_TILE_ROWS
HEAD_ROWS = HEAD_DIM + ONES_ROWS
PAIR_ROWS = 2 * HEAD_ROWS
XROWS = BF16_TILE_ROWS
NEAR_CHUNKS = 2
UNDERFLOW_BITS = 152.0

F32 = jnp.float32
BF16 = jnp.bfloat16
NT_DIMS = (((1,), (1,)), ((), ()))

FFN_TILE = 512
FFN_CHUNKS = (1536, 1280)
assert sum(FFN_CHUNKS) == D_FF
VMEM_LIMIT = 50 * 1024 * 1024


def _split_bf16(v, n):
    terms = []
    for _ in range(n):
        t = v.astype(BF16)
        terms.append(t)
        v = v - t.astype(F32)
    return terms


def _rms(v, g):
    return v * lax.rsqrt(jnp.mean(v * v, axis=-1, keepdims=True) + RMS_EPS) * g


def _bucket_tiles():
    max_exact = REL_BUCKETS // 2

    def bucket(dist):
        n = np.maximum(dist, 0)
        nf = np.maximum(n, 1).astype(np.float32)
        large = max_exact + (np.log(nf / np.float32(max_exact))
                             / np.float32(math.log(REL_MAX_DIST / max_exact))
                             * np.float32(REL_BUCKETS - max_exact)).astype(np.int32)
        large = np.minimum(large, REL_BUCKETS - 1)
        return np.where(n < max_exact, n, large).astype(np.int32)

    key = np.arange(CHUNK, dtype=np.int32)[:, None]
    qry = np.arange(CHUNK, dtype=np.int32)[None, :]
    d0 = qry - key
    own = np.where(d0 >= 0, bucket(d0), -1).astype(np.int32)
    prev = bucket(d0 + CHUNK)
    corner = prev[CHUNK - CORNER:, :CORNER]
    outside = prev.copy()
    outside[CHUNK - CORNER:, :CORNER] = REL_BUCKETS - 1
    assert (outside == REL_BUCKETS - 1).all()
    return own, np.ascontiguousarray(corner)


def _bias_tile_kernel(rb_ref, own_bm_ref, corner_bm_ref, own_ref, corner_ref):
    h = pl.program_id(0)
    for bm_ref, out_ref in ((own_bm_ref, own_ref), (corner_bm_ref, corner_ref)):
        bm = bm_ref[...]
        t = jnp.full(bm.shape, NEG, F32)
        for b in range(REL_BUCKETS):
            t = jnp.where(bm == b, rb_ref[b, h] * LOG2E, t)
        out_ref[0] = t


def _bias_tiles(rel_bias):
    own_bm, corner_bm = _bucket_tiles()
    return pl.pallas_call(
        _bias_tile_kernel,
        grid=(N_HEADS,),
        in_specs=[pl.BlockSpec(memory_space=pltpu.SMEM),
                  pl.BlockSpec((CHUNK, CHUNK), lambda h: (0, 0)),
                  pl.BlockSpec((CORNER, CORNER), lambda h: (0, 0))],
        out_specs=[pl.BlockSpec((1, CHUNK, CHUNK), lambda h: (h, 0, 0)),
                   pl.BlockSpec((1, CORNER, CORNER), lambda h: (h, 0, 0))],
        out_shape=[jax.ShapeDtypeStruct((N_HEADS, CHUNK, CHUNK), F32),
                   jax.ShapeDtypeStruct((N_HEADS, CORNER, CORNER), F32)],
        name="bias_tiles",
    )(rel_bias, jnp.asarray(own_bm), jnp.asarray(corner_bm))


def _inproj_kernel(x_ref, g_ref, wqkv_ref, wtail_ref, bf_ref, cw_ref, tri_ref, perm_ref,
                   mq_ref, mk_ref, fq_ref, fkp_ref, vt_ref, sel_ref, conv_ref,
                   km_ref, cum_ref, uc_ref):
    tm = CHUNK
    st = pl.program_id(1)

    @pl.when(st == 0)
    def _():
        km_ref[...] = jnp.zeros_like(km_ref)
        cum_ref[...] = jnp.zeros_like(cum_ref)
        uc_ref[...] = jnp.zeros_like(uc_ref)

    hn = _rms(x_ref[0], g_ref[...]).astype(BF16)

    qkv = lax.dot_general(hn, wqkv_ref[...], NT_DIMS, preferred_element_type=F32)
    tail = lax.dot_general(hn, wtail_ref[...], NT_DIMS, preferred_element_type=F32)
    mq = qkv[:, 0 * ATT_W:1 * ATT_W]
    mk = qkv[:, 1 * ATT_W:2 * ATT_W]
    mv = qkv[:, 2 * ATT_W:3 * ATT_W]
    fq = qkv[:, 3 * ATT_W:4 * ATT_W]
    fk = qkv[:, 4 * ATT_W:5 * ATT_W]
    fv = qkv[:, 5 * ATT_W:6 * ATT_W]
    scale = HEAD_DIM ** -0.5 * LOG2E
    mqt = mq.T
    mq_ref[0, 0] = (mqt * scale).astype(BF16)
    mk_ref[0] = mk.astype(BF16)
    fq_ref[0, 0] = (fq * scale).T.astype(BF16)
    ones = jnp.ones((ONES_ROWS, tm), BF16)
    for kind, v in enumerate((mv, fv)):
        vt = v.T.astype(BF16)
        for h in range(N_HEADS):
            base = (kind * N_HEADS + h) * HEAD_ROWS
            vt_ref[0, 0, base:base + HEAD_DIM, :] = vt[h * HEAD_DIM:(h + 1) * HEAD_DIM]
            vt_ref[0, 0, base + HEAD_DIM:base + HEAD_ROWS, :] = ones

    rows = lax.broadcasted_iota(jnp.int32, km_ref.shape, 0)
    cols = lax.broadcasted_iota(jnp.int32, km_ref.shape, 1)
    same_head = (rows >> MAX_BLOCKS_SHIFT) == (cols >> HEAD_SHIFT)
    km = km_ref[...]
    for r in range(BLOCKS_PER_CHUNK):
        kmean = jnp.mean(mk[r * BLK:(r + 1) * BLK], axis=0, keepdims=True)
        n = st * BLOCKS_PER_CHUNK + r
        km = jnp.where(same_head & ((rows & (MAX_BLOCKS - 1)) == n), kmean, km)
    km_ref[...] = km

    km_hi, km_lo = _split_bf16(km, 2)
    q_hi, q_lo = _split_bf16(mqt, 2)
    gate = (jnp.dot(km_hi, q_hi, preferred_element_type=F32)
            + jnp.dot(km_hi, q_lo, preferred_element_type=F32)
            + jnp.dot(km_lo, q_hi, preferred_element_type=F32))

    nidx = lax.broadcasted_iota(jnp.int32, (MAX_BLOCKS, tm), 0)
    own = st * BLOCKS_PER_CHUNK + (lax.broadcasted_iota(jnp.int32, (MAX_BLOCKS, tm), 1) >> BLK_SHIFT)
    valid = nidx < own
    for h in range(N_HEADS):
        gh = jnp.where(valid, gate[h * MAX_BLOCKS:(h + 1) * MAX_BLOCKS], -jnp.inf)
        rank = jnp.zeros((MAX_BLOCKS, tm), jnp.int32)
        for m in range(MAX_BLOCKS):
            gm = gh[m:m + 1, :]
            gt = (gm > gh).astype(jnp.int32)
            ge = (gm >= gh).astype(jnp.int32)
            rank = rank + jnp.where(nidx > m, ge, gt)
        keep = ((rank < TOPK) & valid) | (nidx == own)
        sel_ref[0, h * MAX_BLOCKS:(h + 1) * MAX_BLOCKS, :] = jnp.where(keep, 0.0, NEG)

    fl = tail[:, 0:LANES] + bf_ref[...]
    lf = jnp.minimum(fl, 0.0) - jnp.log1p(jnp.exp(-jnp.abs(fl)))
    lf_terms = jnp.concatenate(_split_bf16(lf, N_SPLIT), axis=1)
    tri = tri_ref[...]
    carry = cum_ref[...]
    groups = []
    for r in range(BLOCKS_PER_CHUNK):
        part = jnp.dot(tri, lf_terms[r * BLK:(r + 1) * BLK], preferred_element_type=F32)
        grp = carry + sum(part[:, t * LANES:(t + 1) * LANES] for t in range(N_SPLIT))
        carry = grp[BLK - 1:BLK, :]
        groups.append(grp)
    cum = jnp.concatenate(groups, axis=0)
    cum_ref[...] = carry
    neg_terms = jnp.concatenate(_split_bf16(cum * -LOG2E, N_SPLIT), axis=1)
    kb = jnp.dot(neg_terms, perm_ref[...], preferred_element_type=F32).astype(BF16)
    fkb = fk.astype(BF16)
    for p in range(N_PAIRS):
        fkp_ref[0, :, 2 * p * PAIR_W:(2 * p + 1) * PAIR_W] = fkb[:, p * PAIR_W:(p + 1) * PAIR_W]
        fkp_ref[0, :, (2 * p + 1) * PAIR_W:(2 * p + 2) * PAIR_W] = kb[:, p * PAIR_W:(p + 1) * PAIR_W]

    cv = tail[:, LANES:]
    cvb = cv[:, 0:CONV_CH]
    u = cv[:, CONV_CH:2 * CONV_CH] * cv[:, 2 * CONV_CH:3 * CONV_CH]
    uc = uc_ref[...]
    prev1 = uc[SUBLANES - 1:SUBLANES]
    prev2 = uc[SUBLANES - 2:SUBLANES - 1]
    ridx = lax.broadcasted_iota(jnp.int32, u.shape, 0)
    u1 = jnp.where(ridx == 0, prev1, pltpu.roll(u, 1, 0))
    u2 = jnp.where(ridx == 0, prev2, jnp.where(ridx == 1, prev1, pltpu.roll(u, 2, 0)))
    cw = cw_ref[...]
    y = cw[0:1] * u2 + cw[1:2] * u1 + cw[2:3] * u
    conv_ref[0] = (cvb * y).astype(BF16)
    uc_ref[...] = u[tm - SUBLANES:tm]


def _inproj(x, g, wqkv, wtail, bfp, cw, tri, perm):
    B, S, D = x.shape
    tm = CHUNK
    const = lambda shape: pl.BlockSpec(shape, lambda b, s: (0,) * len(shape))
    row = lambda w: pl.BlockSpec((1, tm, w), lambda b, s: (b, s, 0))
    colmajor = lambda w: pl.BlockSpec((1, 1, w, tm), lambda b, s: (b, s, 0, 0))
    return pl.pallas_call(
        _inproj_kernel,
        grid=(B, S // tm),
        in_specs=[row(D), const((1, D)), const(wqkv.shape), const(wtail.shape),
                  const(bfp.shape), const(cw.shape), const(tri.shape), const(perm.shape)],
        out_specs=[colmajor(ATT_W), row(ATT_W), colmajor(ATT_W), row(2 * ATT_W),
                   pl.BlockSpec((1, 1, 2 * N_HEADS * HEAD_ROWS, tm), lambda b, s: (b, s, 0, 0)),
                   pl.BlockSpec((1, N_HEADS * MAX_BLOCKS, tm), lambda b, s: (b, 0, s)),
                   row(CONV_CH)],
        out_shape=[jax.ShapeDtypeStruct((B, S // tm, ATT_W, tm), BF16),
                   jax.ShapeDtypeStruct((B, S, ATT_W), BF16),
                   jax.ShapeDtypeStruct((B, S // tm, ATT_W, tm), BF16),
                   jax.ShapeDtypeStruct((B, S, 2 * ATT_W), BF16),
                   jax.ShapeDtypeStruct((B, S // tm, 2 * N_HEADS * HEAD_ROWS, tm), BF16),
                   jax.ShapeDtypeStruct((B, N_HEADS * MAX_BLOCKS, S), F32),
                   jax.ShapeDtypeStruct((B, S, CONV_CH), BF16)],
        scratch_shapes=[pltpu.VMEM((N_HEADS * MAX_BLOCKS, ATT_W), F32),
                        pltpu.VMEM((1, LANES), F32),
                        pltpu.VMEM((SUBLANES, CONV_CH), F32)],
        compiler_params=pltpu.CompilerParams(
            dimension_semantics=("arbitrary", "arbitrary"), vmem_limit_bytes=VMEM_LIMIT),
        name="inproj",
    )(x, g, wqkv, wtail, bfp, cw, tri, perm)


def _attn_kernel(*refs, moba, nchunk):
    if moba:
        rb_ref, qt_ref, k_ref, vt_ref, sel_ref, own_ref, corner_ref, ind_ref, o_ref, *bufs = refs
        pair = pl.program_id(1)
        xrow = lax.broadcasted_iota(jnp.int32, (XROWS, CHUNK), 0)
        far_rows, near_fix = [], []
        for hh in range(2):
            fb = rb_ref[REL_BUCKETS - 1, 2 * pair + hh] * LOG2E
            terms = _split_bf16(jnp.full((XROWS, CHUNK), fb, F32), N_SPLIT)
            rows = jnp.zeros((XROWS, CHUNK), F32)
            for t in range(N_SPLIT):
                rows = jnp.where(xrow == BLOCKS_PER_CHUNK + t, terms[t].astype(F32), rows)
            far_rows.append(rows)
            near_fix.append(corner_ref[hh] - fb)
        xpad = jnp.zeros((PAIR_W - XROWS, CHUNK), BF16)
    else:
        qt_ref, k_ref, vt_ref, o_ref, *bufs = refs
    hidx = [jnp.minimum(pl.program_id(0), 0) + hh for hh in range(2)]

    def weights(qi):
        qt = qt_ref[0, qi]
        frow = lax.broadcasted_iota(jnp.int32, qt.shape, 0)
        zero = jnp.zeros_like(qt)
        ws = []
        for hh in range(2):
            w = jnp.where((frow >= hh * HEAD_DIM) & (frow < (hh + 1) * HEAD_DIM), qt, zero)
            if not moba:
                pick = (frow >= N_SPLIT * hh) & (frow < N_SPLIT * (hh + 1))
                w = jnp.concatenate([w, jnp.where(pick, 1.0, 0.0).astype(BF16)], axis=0)
            ws.append(w)
        return ws

    def query_norms(qi):
        q2 = qt_ref[0, qi].astype(F32) ** 2
        return [jnp.sqrt(jnp.sum(q2[hh * HEAD_DIM:(hh + 1) * HEAD_DIM], axis=0, keepdims=True))
                for hh in range(2)]

    def produce(qi, ws, c, dst_ref):
        own, near = c == qi, c == qi - 1
        kc = k_ref[0, c * CHUNK:(c + 1) * CHUNK, :]
        if moba:
            kc = jnp.concatenate([kc, ind_ref[...]], axis=1)
        maxima = []
        for hh in range(2):
            w = ws[hh]
            if moba:
                rows = jnp.zeros((XROWS, CHUNK), F32) if own else far_rows[hh]
                for r in range(BLOCKS_PER_CHUNK):
                    n = hh * MAX_BLOCKS + c * BLOCKS_PER_CHUNK + r
                    sel = sel_ref[0, n:n + 1, qi * CHUNK:(qi + 1) * CHUNK]
                    rows = jnp.where(xrow == r, sel, rows)
                w = jnp.concatenate([w, rows.astype(BF16), xpad], axis=0)
            if own:
                top = jnp.dot(kc[:BLK], w, preferred_element_type=F32)
                bot = jnp.dot(kc[BLK:], w[:, BLK:], preferred_element_type=F32)
                if moba:
                    top = top + own_ref[hh, :BLK, :]
                    bot = bot + own_ref[hh, BLK:, BLK:]
                else:
                    kidx = lax.broadcasted_iota(jnp.int32, (BLK, BLK), 0)
                    qidx = lax.broadcasted_iota(jnp.int32, (BLK, BLK), 1)
                    causal = kidx <= qidx
                    top = jnp.concatenate([jnp.where(causal, top[:, :BLK], NEG), top[:, BLK:]], axis=1)
                    bot = jnp.where(causal, bot, NEG)
                dst_ref[hidx[hh], :BLK, :] = top
                dst_ref[hidx[hh], BLK:, BLK:] = bot
                mt = jnp.max(top, axis=0, keepdims=True)
                mb = jnp.max(bot, axis=0, keepdims=True)
                maxima.append(jnp.concatenate([mt[:, :BLK], jnp.maximum(mt[:, BLK:], mb)], axis=1))
                continue
            s = jnp.dot(kc, w, preferred_element_type=F32)
            if moba and near:
                lo = CHUNK - CORNER
                fixed = jnp.concatenate([s[lo:, :CORNER] + near_fix[hh], s[lo:, CORNER:]], axis=1)
                s = jnp.concatenate([s[:lo], fixed], axis=0)
            dst_ref[hidx[hh]] = s
            maxima.append(jnp.max(s, axis=0, keepdims=True))
        return tuple(maxima)

    def consume(c, src_ref, maxima, state, own=False):
        new = []
        for hh in range(2):
            m_old, acc = state[hh]
            m_new = jnp.maximum(m_old, maxima[hh])
            vth = vt_ref[0, c, hh * HEAD_ROWS:(hh + 1) * HEAD_ROWS, :]
            if own:
                p_top = jnp.exp2(src_ref[hidx[hh], :BLK, :] - m_new).astype(BF16)
                p_bot = jnp.exp2(src_ref[hidx[hh], BLK:, BLK:] - m_new[:, BLK:]).astype(BF16)
                pv_top = jnp.dot(vth[:, :BLK], p_top, preferred_element_type=F32)
                pv_bot = jnp.dot(vth[:, BLK:], p_bot, preferred_element_type=F32)
                pv = jnp.concatenate([pv_top[:, :BLK], pv_top[:, BLK:] + pv_bot], axis=1)
            else:
                p = jnp.exp2(src_ref[hidx[hh]] - m_new).astype(BF16)
                pv = jnp.dot(vth, p, preferred_element_type=F32)
            new.append((m_new, jnp.exp2(m_old - m_new) * acc + pv))
        return tuple(new)

    def fresh_state():
        return tuple((jnp.full((1, CHUNK), NEG, F32), jnp.zeros((HEAD_ROWS, CHUNK), F32))
                     for _ in range(2))

    def finish(qi, state):
        ot = jnp.concatenate([acc[:HEAD_DIM] * (1.0 / acc[HEAD_DIM:HEAD_DIM + 1])
                              for (_, acc) in state], axis=0)
        o_ref[0, qi * CHUNK:(qi + 1) * CHUNK, :] = ot.T.astype(BF16)

    if moba:
        g = 0
        ws = weights(0)
        maxima = produce(0, ws, 0, bufs[0])
        for qi in range(nchunk):
            state = fresh_state()
            ws_next = weights(qi + 1) if qi + 1 < nchunk else None
            for c in range(qi, -1, -1):
                nxt = None
                if c > 0:
                    nxt = produce(qi, ws, c - 1, bufs[(g + 1) % 2])
                elif ws_next is not None:
                    nxt = produce(qi + 1, ws_next, qi + 1, bufs[(g + 1) % 2])
                state = consume(c, bufs[g % 2], maxima, state, own=(c == qi))
                maxima = nxt
                g += 1
            ws = ws_next
            finish(qi, state)
        return

    own_bufs, far_bufs = bufs[0:2], bufs[2:4]
    kall = k_ref[0][:, 0:PAIR_W].astype(F32) ** 2
    lane = lax.broadcasted_iota(jnp.int32, (1, PAIR_W), 1)
    key_norm = [jnp.sqrt(jnp.max(jnp.sum(jnp.where((lane >= hh * HEAD_DIM) & (lane < (hh + 1) * HEAD_DIM),
                                                      kall, 0.0), axis=1, keepdims=True)))
                for hh in range(2)]

    def gap(qn, decay_rows, hh, m, newer_queries_only):
        mine = (lane >= N_SPLIT * hh) & (lane < N_SPLIT * (hh + 1))
        newest = jnp.max(jnp.sum(jnp.where(mine, decay_rows, 0.0), axis=1, keepdims=True))
        bound = qn[hh] * key_norm[hh] + newest - m
        if newer_queries_only:
            qlane = lax.broadcasted_iota(jnp.int32, bound.shape, 1)
            bound = jnp.where(qlane >= BLK, bound, -jnp.inf)
        return jnp.max(bound)

    def only_band_matters(qi, state):
        base = (qi - 2) * CHUNK
        older_blk = k_ref[0, base + BLK - BF16_TILE_ROWS:base + BLK, PAIR_W:2 * PAIR_W].astype(F32)
        newer_blk = k_ref[0, base + CHUNK - BF16_TILE_ROWS:base + CHUNK, PAIR_W:2 * PAIR_W].astype(F32)
        qn = query_norms(qi)
        worst = None
        for hh in range(2):
            g = jnp.maximum(gap(qn, older_blk, hh, state[hh][0], False),
                            gap(qn, newer_blk, hh, state[hh][0], True))
            worst = g if worst is None else jnp.maximum(worst, g)
        return worst < -UNDERFLOW_BITS

    band_buf = far_bufs[1]

    def produce_band(qi, ws):
        c = qi - 2
        kq = k_ref[0, c * CHUNK + BLK:(c + 1) * CHUNK, :]
        maxima = []
        for hh in range(2):
            s = jnp.dot(kq, ws[hh][:, :BLK], preferred_element_type=F32)
            band_buf[hidx[hh], :BLK, :BLK] = s
            maxima.append(jnp.max(s, axis=0, keepdims=True))
        return tuple(maxima)

    def consume_band(qi, maxima, state):
        c = qi - 2
        new = []
        for hh in range(2):
            m_old, acc = state[hh]
            m_left = jnp.maximum(m_old[:, :BLK], maxima[hh])
            p = jnp.exp2(band_buf[hidx[hh], :BLK, :BLK] - m_left).astype(BF16)
            vth = vt_ref[0, c, hh * HEAD_ROWS:(hh + 1) * HEAD_ROWS, BLK:]
            left = jnp.exp2(m_old[:, :BLK] - m_left) * acc[:, :BLK] + jnp.dot(vth, p, preferred_element_type=F32)
            new.append((jnp.concatenate([m_left, m_old[:, BLK:]], axis=1),
                        jnp.concatenate([left, acc[:, BLK:]], axis=1)))
        return tuple(new)

    ws = weights(0)
    maxima = produce(0, ws, 0, own_bufs[0])
    for qi in range(nchunk):
        state = fresh_state()
        ws_next = weights(qi + 1) if qi + 1 < nchunk else None
        own_src, next_dst = own_bufs[qi % 2], own_bufs[(qi + 1) % 2]

        def next_own_scores(qi=qi, ws_next=ws_next, next_dst=next_dst):
            return produce(qi + 1, ws_next, qi + 1, next_dst)

        if qi < NEAR_CHUNKS:
            src = own_src
            for j, c in enumerate(range(qi, -1, -1)):
                if c > 0:
                    dst = far_bufs[j % 2]
                    nxt = produce(qi, ws, c - 1, dst)
                elif ws_next is not None:
                    next_own = next_own_scores()
                state = consume(c, src, maxima, state, own=(j == 0))
                if c > 0:
                    src, maxima = dst, nxt
        else:
            nxt = produce(qi, ws, qi - 1, far_bufs[0])
            state = consume(qi, own_src, maxima, state, own=True)
            band_mx = produce_band(qi, ws)
            state = consume(qi - 1, far_bufs[0], nxt, state)

            def band_only(st, qi=qi, band_mx=band_mx, ws_next=ws_next):
                own_mx = next_own_scores() if ws_next is not None else band_mx
                return consume_band(qi, band_mx, st), own_mx

            def everything(st, qi=qi, ws=ws, ws_next=ws_next, band_mx=band_mx):
                own_mx = next_own_scores() if ws_next is not None else band_mx
                first_far = qi - 2
                mx = produce(qi, ws, first_far, far_bufs[0])
                for i, c in enumerate(range(first_far, -1, -1)):
                    nx = produce(qi, ws, c - 1, far_bufs[(i + 1) % 2]) if c > 0 else None
                    st = consume(c, far_bufs[i % 2], mx, st)
                    mx = nx
                return st, own_mx

            state, next_own = lax.cond(only_band_matters(qi, state), band_only, everything, state)
        finish(qi, state)
        ws, maxima = ws_next, next_own


def _block_indicator():
    ind = np.zeros((CHUNK, PAIR_W), np.float32)
    for r in range(BLOCKS_PER_CHUNK):
        ind[r * BLK:(r + 1) * BLK, r] = 1.0
    ind[:, BLOCKS_PER_CHUNK:BLOCKS_PER_CHUNK + N_SPLIT] = 1.0
    return jnp.asarray(ind, BF16)


def _attention(qt, k, vt, sel=None, tables=None, rel_bias=None):
    moba = sel is not None
    B, S, _ = k.shape
    nchunk = S // CHUNK
    kw = k.shape[-1] // N_PAIRS
    vt_off = 0 if moba else N_PAIRS
    in_specs = [pl.BlockSpec((1, nchunk, PAIR_W, CHUNK), lambda b, p: (b, 0, p, 0)),
                pl.BlockSpec((1, S, kw), lambda b, p: (b, 0, p)),
                pl.BlockSpec((1, nchunk, PAIR_ROWS, CHUNK), lambda b, p: (b, 0, p + vt_off, 0))]
    args = [qt, k, vt]
    if moba:
        in_specs = [pl.BlockSpec(memory_space=pltpu.SMEM)] + in_specs
        in_specs += [pl.BlockSpec((1, 2 * MAX_BLOCKS, S), lambda b, p: (b, p, 0)),
                     pl.BlockSpec((2, CHUNK, CHUNK), lambda b, p: (p, 0, 0)),
                     pl.BlockSpec((2, CORNER, CORNER), lambda b, p: (p, 0, 0)),
                     pl.BlockSpec((CHUNK, PAIR_W), lambda b, p: (0, 0))]
        args = [rel_bias] + args + [sel, *tables, _block_indicator()]
    return pl.pallas_call(
        functools.partial(_attn_kernel, moba=moba, nchunk=nchunk),
        grid=(B, N_PAIRS),
        in_specs=in_specs,
        out_specs=pl.BlockSpec((1, S, PAIR_W), lambda b, p: (b, 0, p)),
        out_shape=jax.ShapeDtypeStruct((B, S, ATT_W), BF16),
        scratch_shapes=[pltpu.VMEM((2, CHUNK, CHUNK), F32)] * (2 if moba else 4),
        compiler_params=pltpu.CompilerParams(
            dimension_semantics=("arbitrary", "arbitrary"), vmem_limit_bytes=VMEM_LIMIT),
        name="moba_attn" if moba else "fox_attn",
    )(*args)


def _out_ffn_kernel(moba_ref, fox_ref, conv_ref, x_ref, wo_ref, wg_ref, wu_ref, wd_ref,
                    gpm_ref, gpf_ref, gqf_ref, o_ref):
    mix_in = jnp.concatenate([moba_ref[...], fox_ref[...], conv_ref[...]], axis=1)
    mixed = jnp.dot(mix_in, wo_ref[...], preferred_element_type=F32)
    x1 = x_ref[...] + _rms(mixed, gpm_ref[...])
    h2 = _rms(x1, gpf_ref[...]).astype(BF16)
    ff = jnp.zeros(x1.shape, F32)
    lo = 0
    for width in FFN_CHUNKS:
        gate = jnp.dot(h2, wg_ref[:, lo:lo + width], preferred_element_type=F32)
        up = jnp.dot(h2, wu_ref[:, lo:lo + width], preferred_element_type=F32)
        act = (gate * (1.0 / (1.0 + jnp.exp(-gate))) * up).astype(BF16)
        ff = ff + jnp.dot(act, wd_ref[lo:lo + width, :], preferred_element_type=F32)
        lo += width
    o_ref[...] = x1 + _rms(ff, gqf_ref[...])


def _out_ffn(moba, fox, conv, x, wo, wg, wu, wd, gpm, gpf, gqf):
    M, D = x.shape
    tm = FFN_TILE
    row = lambda w: pl.BlockSpec((tm, w), lambda m: (m, 0))
    const = lambda shape: pl.BlockSpec(shape, lambda m: (0, 0), pipeline_mode=pl.Buffered(1))
    return pl.pallas_call(
        _out_ffn_kernel,
        grid=(M // tm,),
        in_specs=[row(ATT_W), row(ATT_W), row(CONV_CH), row(D),
                  const(wo.shape), const(wg.shape), const(wu.shape), const(wd.shape),
                  const((1, D)), const((1, D)), const((1, D))],
        out_specs=row(D),
        out_shape=jax.ShapeDtypeStruct((M, D), F32),
        compiler_params=pltpu.CompilerParams(
            dimension_semantics=("arbitrary",), vmem_limit_bytes=VMEM_LIMIT),
        name="out_ffn",
    )(moba, fox, conv, x, wo, wg, wu, wd, gpm, gpf, gqf)


def _cumsum_constants():
    tri = np.tril(np.ones((BLK, BLK), np.float32))
    perm = np.zeros((N_SPLIT * LANES, ATT_W), np.float32)
    for s in range(N_SPLIT):
        for h in range(N_HEADS):
            perm[s * LANES + h, PAIR_W * (h // 2) + N_SPLIT * (h % 2) + s] = 1.0
    return jnp.asarray(tri, BF16), jnp.asarray(perm, BF16)


def kernel(x, w_in, b_f, conv_w, w_out, rel_bias, g_pre_mix, g_post_mix, g_pre_ffn, g_post_ffn,
           w_gate, w_up, w_down):
    B, S, D = x.shape
    depth = w_in.shape[0]
    assert D == D_MODEL and S % CHUNK == 0 and S // BLK <= MAX_BLOCKS
    assert (B * S) % FFN_TILE == 0

    rel_bias = rel_bias.astype(F32)
    tables = _bias_tiles(rel_bias)
    w_in_t = jnp.swapaxes(w_in, 1, 2)
    tri, perm = _cumsum_constants()
    a = ATT_W
    for l in range(depth):
        wl = w_in_t[l]
        wqkv = wl[0:6 * a].astype(BF16)
        wtail = jnp.concatenate([wl[6 * a:6 * a + N_HEADS], jnp.zeros((LANES - N_HEADS, D), wl.dtype),
                                 wl[6 * a + N_HEADS:]], axis=0).astype(BF16)
        bfp = jnp.pad(b_f[l].astype(F32), (0, LANES - N_HEADS)).reshape(1, LANES)
        row = lambda g: g[l].astype(F32).reshape(1, D)

        mq, mk, fq, fkp, vt, sel, conv = _inproj(
            x, row(g_pre_mix), wqkv, wtail, bfp, conv_w[l].astype(F32), tri, perm)
        moba = _attention(mq, mk, vt, sel, tables, rel_bias)
        fox = _attention(fq, fkp, vt)
        x = _out_ffn(moba.reshape(B * S, a), fox.reshape(B * S, a), conv.reshape(B * S, CONV_CH),
                     x.reshape(B * S, D), w_out[l].astype(BF16), w_gate[l].astype(BF16),
                     w_up[l].astype(BF16), w_down[l].astype(BF16),
                     row(g_post_mix), row(g_pre_ffn), row(g_post_ffn)).reshape(B, S, D)
    return x
```

```python
import functools
import math

import jax
import jax.numpy as jnp
import numpy as np
from jax import lax
from jax.experimental import pallas as pl
from jax.experimental.pallas import tpu as pltpu

D_MODEL = 1024
HEAD_DIM = 64
N_HEADS = 6
ATT_W = N_HEADS * HEAD_DIM
PAIR_W = 2 * HEAD_DIM
N_PAIRS = N_HEADS // 2
CONV_CH = 256
CONV_WIDTH = 3
BLK = 256
BLK_SHIFT = BLK.bit_length() - 1
CHUNK = 512
BLOCKS_PER_CHUNK = CHUNK // BLK
assert BLOCKS_PER_CHUNK == 2
MAX_BLOCKS = 16
MAX_BLOCKS_SHIFT = MAX_BLOCKS.bit_length() - 1
HEAD_SHIFT = HEAD_DIM.bit_length() - 1
BF16_TILE_ROWS = 16
TOPK = 3
REL_BUCKETS = 32
REL_MAX_DIST = 128
D_FF = 2816
RMS_EPS = 1e-6
NEG = -1e30
LANES = 128
SUBLANES = 8
N_SPLIT = 3
LOG2E = math.log2(math.e)
CORNER = REL_MAX_DIST
ONES_ROWS = BF16_TILE_ROWS
HEAD_ROWS = HEAD_DIM + ONES_ROWS
PAIR_ROWS = 2 * HEAD_ROWS
XROWS = BF16_TILE_ROWS
NEAR_CHUNKS = 2
UNDERFLOW_BITS = 152.0

F32 = jnp.float32
BF16 = jnp.bfloat16
NT_DIMS = (((1,), (1,)), ((), ()))

FFN_TILE = 512
FFN_CHUNKS = (1536, 1280)
assert sum(FFN_CHUNKS) == D_FF
VMEM_LIMIT = 50 * 1024 * 1024


def _split_bf16(v, n):
    terms = []
    for _ in range(n):
        t = v.astype(BF16)
        terms.append(t)
        v = v - t.astype(F32)
    return terms


def _rms(v, g):
    return v * lax.rsqrt(jnp.mean(v * v, axis=-1, keepdims=True) + RMS_EPS) * g


def _bucket_tiles():
    max_exact = REL_BUCKETS // 2

    def bucket(dist):
        n = np.maximum(dist, 0)
        nf = np.maximum(n, 1).astype(np.float32)
        large = max_exact + (np.log(nf / np.float32(max_exact))
                             / np.float32(math.log(REL_MAX_DIST / max_exact))
                             * np.float32(REL_BUCKETS - max_exact)).astype(np.int32)
        large = np.minimum(large, REL_BUCKETS - 1)
        return np.where(n < max_exact, n, large).astype(np.int32)

    key = np.arange(CHUNK, dtype=np.int32)[:, None]
    qry = np.arange(CHUNK, dtype=np.int32)[None, :]
    d0 = qry - key
    own = np.where(d0 >= 0, bucket(d0), -1).astype(np.int32)
    prev = bucket(d0 + CHUNK)
    corner = prev[CHUNK - CORNER:, :CORNER]
    outside = prev.copy()
    outside[CHUNK - CORNER:, :CORNER] = REL_BUCKETS - 1
    assert (outside == REL_BUCKETS - 1).all()
    return own, np.ascontiguousarray(corner)


def _bias_tile_kernel(rb_ref, own_bm_ref, corner_bm_ref, own_ref, corner_ref):
    h = pl.program_id(0)
    for bm_ref, out_ref in ((own_bm_ref, own_ref), (corner_bm_ref, corner_ref)):
        bm = bm_ref[...]
        t = jnp.full(bm.shape, NEG, F32)
        for b in range(REL_BUCKETS):
            t = jnp.where(bm == b, rb_ref[b, h] * LOG2E, t)
        out_ref[0] = t


def _bias_tiles(rel_bias):
    own_bm, corner_bm = _bucket_tiles()
    return pl.pallas_call(
        _bias_tile_kernel,
        grid=(N_HEADS,),
        in_specs=[pl.BlockSpec(memory_space=pltpu.SMEM),
                  pl.BlockSpec((CHUNK, CHUNK), lambda h: (0, 0)),
                  pl.BlockSpec((CORNER, CORNER), lambda h: (0, 0))],
        out_specs=[pl.BlockSpec((1, CHUNK, CHUNK), lambda h: (h, 0, 0)),
                   pl.BlockSpec((1, CORNER, CORNER), lambda h: (h, 0, 0))],
        out_shape=[jax.ShapeDtypeStruct((N_HEADS, CHUNK, CHUNK), F32),
                   jax.ShapeDtypeStruct((N_HEADS, CORNER, CORNER), F32)],
        name="bias_tiles",
    )(rel_bias, jnp.asarray(own_bm), jnp.asarray(corner_bm))


def _inproj_kernel(x_ref, g_ref, wqkv_ref, wtail_ref, bf_ref, cw_ref, tri_ref, perm_ref,
                   mq_ref, mk_ref, fq_ref, fkp_ref, vt_ref, sel_ref, conv_ref,
                   km_ref, cum_ref, uc_ref, gh_ref):
    tm = CHUNK
    st = pl.program_id(1)

    @pl.when(st == 0)
    def _():
        km_ref[...] = jnp.zeros_like(km_ref)
        cum_ref[...] = jnp.zeros_like(cum_ref)
        uc_ref[...] = jnp.zeros_like(uc_ref)

    hn = _rms(x_ref[0], g_ref[...]).astype(BF16)

    qkv = lax.dot_general(hn, wqkv_ref[...], NT_DIMS, preferred_element_type=F32)
    tail = lax.dot_general(hn, wtail_ref[...], NT_DIMS, preferred_element_type=F32)
    mq = qkv[:, 0 * ATT_W:1 * ATT_W]
    mk = qkv[:, 1 * ATT_W:2 * ATT_W]
    mv = qkv[:, 2 * ATT_W:3 * ATT_W]
    fq = qkv[:, 3 * ATT_W:4 * ATT_W]
    fk = qkv[:, 4 * ATT_W:5 * ATT_W]
    fv = qkv[:, 5 * ATT_W:6 * ATT_W]
    scale = HEAD_DIM ** -0.5 * LOG2E
    mqt = mq.T
    mq_ref[0, 0] = (mqt * scale).astype(BF16)
    mk_ref[0] = mk.astype(BF16)
    fq_ref[0, 0] = (fq * scale).T.astype(BF16)
    ones = jnp.ones((ONES_ROWS, tm), BF16)
    for kind, v in enumerate((mv, fv)):
        vt = v.T.astype(BF16)
        for h in range(N_HEADS):
            base = (kind * N_HEADS + h) * HEAD_ROWS
            vt_ref[0, 0, base:base + HEAD_DIM, :] = vt[h * HEAD_DIM:(h + 1) * HEAD_DIM]
            vt_ref[0, 0, base + HEAD_DIM:base + HEAD_ROWS, :] = ones

    rows = lax.broadcasted_iota(jnp.int32, km_ref.shape, 0)
    cols = lax.broadcasted_iota(jnp.int32, km_ref.shape, 1)
    same_head = (rows >> MAX_BLOCKS_SHIFT) == (cols >> HEAD_SHIFT)
    km = km_ref[...]
    for r in range(BLOCKS_PER_CHUNK):
        kmean = jnp.mean(mk[r * BLK:(r + 1) * BLK], axis=0, keepdims=True)
        n = st * BLOCKS_PER_CHUNK + r
        km = jnp.where(same_head & ((rows & (MAX_BLOCKS - 1)) == n), kmean, km)
    km_ref[...] = km

    km_hi, km_lo = _split_bf16(km, 2)
    q_hi, q_lo = _split_bf16(mqt, 2)
    gate = (jnp.dot(km_hi, q_hi, preferred_element_type=F32)
            + jnp.dot(km_hi, q_lo, preferred_element_type=F32)
            + jnp.dot(km_lo, q_hi, preferred_element_type=F32))

    nidx = lax.broadcasted_iota(jnp.int32, (MAX_BLOCKS, tm), 0)
    own = st * BLOCKS_PER_CHUNK + (lax.broadcasted_iota(jnp.int32, (MAX_BLOCKS, tm), 1) >> BLK_SHIFT)
    valid = nidx < own
    for h in range(N_HEADS):
        gh_ref[h * MAX_BLOCKS:(h + 1) * MAX_BLOCKS, :] = jnp.where(
            valid, gate[h * MAX_BLOCKS:(h + 1) * MAX_BLOCKS], -jnp.inf)

    def outranked_by(m, ranks):
        older = nidx > m
        new = []
        for h in range(N_HEADS):
            gh = gh_ref[h * MAX_BLOCKS:(h + 1) * MAX_BLOCKS, :]
            gm = gh_ref[pl.ds(h * MAX_BLOCKS + m, 1), :]
            gt = (gm > gh).astype(jnp.int32)
            ge = (gm >= gh).astype(jnp.int32)
            new.append(ranks[h] + jnp.where(older, ge, gt))
        return tuple(new)

    ranks = lax.fori_loop(0, st * BLOCKS_PER_CHUNK + 1, outranked_by,
                          tuple(jnp.zeros((MAX_BLOCKS, tm), jnp.int32) for _ in range(N_HEADS)))
    for h in range(N_HEADS):
        keep = ((ranks[h] < TOPK) & valid) | (nidx == own)
        sel_ref[0, h * MAX_BLOCKS:(h + 1) * MAX_BLOCKS, :] = jnp.where(keep, 0.0, NEG)

    fl = tail[:, 0:LANES] + bf_ref[...]
    lf = jnp.minimum(fl, 0.0) - jnp.log1p(jnp.exp(-jnp.abs(fl)))
    lf_terms = jnp.concatenate(_split_bf16(lf, N_SPLIT), axis=1)
    tri = tri_ref[...]
    carry = cum_ref[...]
    groups = []
    for r in range(BLOCKS_PER_CHUNK):
        part = jnp.dot(tri, lf_terms[r * BLK:(r + 1) * BLK], preferred_element_type=F32)
        grp = carry + sum(part[:, t * LANES:(t + 1) * LANES] for t in range(N_SPLIT))
        carry = grp[BLK - 1:BLK, :]
        groups.append(grp)
    cum = jnp.concatenate(groups, axis=0)
    cum_ref[...] = carry
    neg_terms = jnp.concatenate(_split_bf16(cum * -LOG2E, N_SPLIT), axis=1)
    kb = jnp.dot(neg_terms, perm_ref[...], preferred_element_type=F32).astype(BF16)
    fkb = fk.astype(BF16)
    for p in range(N_PAIRS):
        fkp_ref[0, :, 2 * p * PAIR_W:(2 * p + 1) * PAIR_W] = fkb[:, p * PAIR_W:(p + 1) * PAIR_W]
        fkp_ref[0, :, (2 * p + 1) * PAIR_W:(2 * p + 2) * PAIR_W] = kb[:, p * PAIR_W:(p + 1) * PAIR_W]

    cv = tail[:, LANES:]
    cvb = cv[:, 0:CONV_CH]
    u = cv[:, CONV_CH:2 * CONV_CH] * cv[:, 2 * CONV_CH:3 * CONV_CH]
    uc = uc_ref[...]
    prev1 = uc[SUBLANES - 1:SUBLANES]
    prev2 = uc[SUBLANES - 2:SUBLANES - 1]
    ridx = lax.broadcasted_iota(jnp.int32, u.shape, 0)
    u1 = jnp.where(ridx == 0, prev1, pltpu.roll(u, 1, 0))
    u2 = jnp.where(ridx == 0, prev2, jnp.where(ridx == 1, prev1, pltpu.roll(u, 2, 0)))
    cw = cw_ref[...]
    y = cw[0:1] * u2 + cw[1:2] * u1 + cw[2:3] * u
    conv_ref[0] = (cvb * y).astype(BF16)
    uc_ref[...] = u[tm - SUBLANES:tm]


def _inproj(x, g, wqkv, wtail, bfp, cw, tri, perm):
    B, S, D = x.shape
    tm = CHUNK
    const = lambda shape: pl.BlockSpec(shape, lambda b, s: (0,) * len(shape))
    row = lambda w: pl.BlockSpec((1, tm, w), lambda b, s: (b, s, 0))
    colmajor = lambda w: pl.BlockSpec((1, 1, w, tm), lambda b, s: (b, s, 0, 0))
    return pl.pallas_call(
        _inproj_kernel,
        grid=(B, S // tm),
        in_specs=[row(D), const((1, D)), const(wqkv.shape), const(wtail.shape),
                  const(bfp.shape), const(cw.shape), const(tri.shape), const(perm.shape)],
        out_specs=[colmajor(ATT_W), row(ATT_W), colmajor(ATT_W), row(2 * ATT_W),
                   pl.BlockSpec((1, 1, 2 * N_HEADS * HEAD_ROWS, tm), lambda b, s: (b, s, 0, 0)),
                   pl.BlockSpec((1, N_HEADS * MAX_BLOCKS, tm), lambda b, s: (b, 0, s)),
                   row(CONV_CH)],
        out_shape=[jax.ShapeDtypeStruct((B, S // tm, ATT_W, tm), BF16),
                   jax.ShapeDtypeStruct((B, S, ATT_W), BF16),
                   jax.ShapeDtypeStruct((B, S // tm, ATT_W, tm), BF16),
                   jax.ShapeDtypeStruct((B, S, 2 * ATT_W), BF16),
                   jax.ShapeDtypeStruct((B, S // tm, 2 * N_HEADS * HEAD_ROWS, tm), BF16),
                   jax.ShapeDtypeStruct((B, N_HEADS * MAX_BLOCKS, S), F32),
                   jax.ShapeDtypeStruct((B, S, CONV_CH), BF16)],
        scratch_shapes=[pltpu.VMEM((N_HEADS * MAX_BLOCKS, ATT_W), F32),
                        pltpu.VMEM((1, LANES), F32),
                        pltpu.VMEM((SUBLANES, CONV_CH), F32),
                        pltpu.VMEM((N_HEADS * MAX_BLOCKS, CHUNK), F32)],
        compiler_params=pltpu.CompilerParams(
            dimension_semantics=("arbitrary", "arbitrary"), vmem_limit_bytes=VMEM_LIMIT),
        name="inproj",
    )(x, g, wqkv, wtail, bfp, cw, tri, perm)


def _attn_kernel(*refs, moba, nchunk):
    if moba:
        rb_ref, qt_ref, k_ref, vt_ref, sel_ref, own_ref, corner_ref, ind_ref, o_ref, *bufs = refs
        pair = pl.program_id(1)
        xrow = lax.broadcasted_iota(jnp.int32, (XROWS, CHUNK), 0)
        far_rows, near_fix = [], []
        for hh in range(2):
            fb = rb_ref[REL_BUCKETS - 1, 2 * pair + hh] * LOG2E
            terms = _split_bf16(jnp.full((XROWS, CHUNK), fb, F32), N_SPLIT)
            rows = jnp.zeros((XROWS, CHUNK), F32)
            for t in range(N_SPLIT):
                rows = jnp.where(xrow == BLOCKS_PER_CHUNK + t, terms[t].astype(F32), rows)
            far_rows.append(rows)
            near_fix.append(corner_ref[hh] - fb)
        xpad = jnp.zeros((PAIR_W - XROWS, CHUNK), BF16)
    else:
        qt_ref, k_ref, vt_ref, o_ref, *bufs = refs
    hidx = [jnp.minimum(pl.program_id(0), 0) + hh for hh in range(2)]

    def weights(qi):
        qt = qt_ref[0, qi]
        frow = lax.broadcasted_iota(jnp.int32, qt.shape, 0)
        zero = jnp.zeros_like(qt)
        ws = []
        for hh in range(2):
            w = jnp.where((frow >= hh * HEAD_DIM) & (frow < (hh + 1) * HEAD_DIM), qt, zero)
            if not moba:
                pick = (frow >= N_SPLIT * hh) & (frow < N_SPLIT * (hh + 1))
                w = jnp.concatenate([w, jnp.where(pick, 1.0, 0.0).astype(BF16)], axis=0)
            ws.append(w)
        return ws

    def query_norms(qi):
        q2 = qt_ref[0, qi].astype(F32) ** 2
        return [jnp.sqrt(jnp.sum(q2[hh * HEAD_DIM:(hh + 1) * HEAD_DIM], axis=0, keepdims=True))
                for hh in range(2)]

    def produce(qi, ws, c, dst_ref):
        own, near = c == qi, c == qi - 1
        kc = k_ref[0, c * CHUNK:(c + 1) * CHUNK, :]
        if moba:
            kc = jnp.concatenate([kc, ind_ref[...]], axis=1)
        maxima = []
        for hh in range(2):
            w = ws[hh]
            if moba:
                rows = jnp.zeros((XROWS, CHUNK), F32) if own else far_rows[hh]
                for r in range(BLOCKS_PER_CHUNK):
                    n = hh * MAX_BLOCKS + c * BLOCKS_PER_CHUNK + r
                    sel = sel_ref[0, n:n + 1, qi * CHUNK:(qi + 1) * CHUNK]
                    rows = jnp.where(xrow == r, sel, rows)
                w = jnp.concatenate([w, rows.astype(BF16), xpad], axis=0)
            if own:
                top = jnp.dot(kc[:BLK], w, preferred_element_type=F32)
                bot = jnp.dot(kc[BLK:], w[:, BLK:], preferred_element_type=F32)
                if moba:
                    top = top + own_ref[hh, :BLK, :]
                    bot = bot + own_ref[hh, BLK:, BLK:]
                else:
                    kidx = lax.broadcasted_iota(jnp.int32, (BLK, BLK), 0)
                    qidx = lax.broadcasted_iota(jnp.int32, (BLK, BLK), 1)
                    causal = kidx <= qidx
                    top = jnp.concatenate([jnp.where(causal, top[:, :BLK], NEG), top[:, BLK:]], axis=1)
                    bot = jnp.where(causal, bot, NEG)
                dst_ref[hidx[hh], :BLK, :] = top
                dst_ref[hidx[hh], BLK:, BLK:] = bot
                mt = jnp.max(top, axis=0, keepdims=True)
                mb = jnp.max(bot, axis=0, keepdims=True)
                maxima.append(jnp.concatenate([mt[:, :BLK], jnp.maximum(mt[:, BLK:], mb)], axis=1))
                continue
            s = jnp.dot(kc, w, preferred_element_type=F32)
            if moba and near:
                lo = CHUNK - CORNER
                fixed = jnp.concatenate([s[lo:, :CORNER] + near_fix[hh], s[lo:, CORNER:]], axis=1)
                s = jnp.concatenate([s[:lo], fixed], axis=0)
            dst_ref[hidx[hh]] = s
            maxima.append(jnp.max(s, axis=0, keepdims=True))
        return tuple(maxima)

    def consume(c, src_ref, maxima, state, own=False):
        new = []
        for hh in range(2):
            m_old, acc = state[hh]
            m_new = jnp.maximum(m_old, maxima[hh])
            vth = vt_ref[0, c, hh * HEAD_ROWS:(hh + 1) * HEAD_ROWS, :]
            if own:
                p_top = jnp.exp2(src_ref[hidx[hh], :BLK, :] - m_new).astype(BF16)
                p_bot = jnp.exp2(src_ref[hidx[hh], BLK:, BLK:] - m_new[:, BLK:]).astype(BF16)
                pv_top = jnp.dot(vth[:, :BLK], p_top, preferred_element_type=F32)
                pv_bot = jnp.dot(vth[:, BLK:], p_bot, preferred_element_type=F32)
                pv = jnp.concatenate([pv_top[:, :BLK], pv_top[:, BLK:] + pv_bot], axis=1)
            else:
                p = jnp.exp2(src_ref[hidx[hh]] - m_new).astype(BF16)
                pv = jnp.dot(vth, p, preferred_element_type=F32)
            new.append((m_new, jnp.exp2(m_old - m_new) * acc + pv))
        return tuple(new)

    def fresh_state():
        return tuple((jnp.full((1, CHUNK), NEG, F32), jnp.zeros((HEAD_ROWS, CHUNK), F32))
                     for _ in range(2))

    def finish(qi, state):
        ot = jnp.concatenate([acc[:HEAD_DIM] * (1.0 / acc[HEAD_DIM:HEAD_DIM + 1])
                              for (_, acc) in state], axis=0)
        o_ref[0, qi * CHUNK:(qi + 1) * CHUNK, :] = ot.T.astype(BF16)

    if moba:
        g = 0
        ws = weights(0)
        maxima = produce(0, ws, 0, bufs[0])
        for qi in range(nchunk):
            state = fresh_state()
            ws_next = weights(qi + 1) if qi + 1 < nchunk else None
            for c in range(qi, -1, -1):
                nxt = None
                if c > 0:
                    nxt = produce(qi, ws, c - 1, bufs[(g + 1) % 2])
                elif ws_next is not None:
                    nxt = produce(qi + 1, ws_next, qi + 1, bufs[(g + 1) % 2])
                state = consume(c, bufs[g % 2], maxima, state, own=(c == qi))
                maxima = nxt
                g += 1
            ws = ws_next
            finish(qi, state)
        return

    own_bufs, far_bufs = bufs[0:2], bufs[2:4]
    kall = k_ref[0][:, 0:PAIR_W].astype(F32) ** 2
    lane = lax.broadcasted_iota(jnp.int32, (1, PAIR_W), 1)
    key_norm = [jnp.sqrt(jnp.max(jnp.sum(jnp.where((lane >= hh * HEAD_DIM) & (lane < (hh + 1) * HEAD_DIM),
                                                      kall, 0.0), axis=1, keepdims=True)))
                for hh in range(2)]

    def gap(qn, decay_rows, hh, m, newer_queries_only):
        mine = (lane >= N_SPLIT * hh) & (lane < N_SPLIT * (hh + 1))
        newest = jnp.max(jnp.sum(jnp.where(mine, decay_rows, 0.0), axis=1, keepdims=True))
        bound = qn[hh] * key_norm[hh] + newest - m
        if newer_queries_only:
            qlane = lax.broadcasted_iota(jnp.int32, bound.shape, 1)
            bound = jnp.where(qlane >= BLK, bound, -jnp.inf)
        return jnp.max(bound)

    def only_band_matters(qi, state):
        base = (qi - 2) * CHUNK
        older_blk = k_ref[0, base + BLK - BF16_TILE_ROWS:base + BLK, PAIR_W:2 * PAIR_W].astype(F32)
        newer_blk = k_ref[0, base + CHUNK - BF16_TILE_ROWS:base + CHUNK, PAIR_W:2 * PAIR_W].astype(F32)
        qn = query_norms(qi)
        worst = None
        for hh in range(2):
            g = jnp.maximum(gap(qn, older_blk, hh, state[hh][0], False),
                            gap(qn, newer_blk, hh, state[hh][0], True))
            worst = g if worst is None else jnp.maximum(worst, g)
        return worst < -UNDERFLOW_BITS

    band_buf = far_bufs[1]

    def produce_band(qi, ws):
        c = qi - 2
        kq = k_ref[0, c * CHUNK + BLK:(c + 1) * CHUNK, :]
        maxima = []
        for hh in range(2):
            s = jnp.dot(kq, ws[hh][:, :BLK], preferred_element_type=F32)
            band_buf[hidx[hh], :BLK, :BLK] = s
            maxima.append(jnp.max(s, axis=0, keepdims=True))
        return tuple(maxima)

    def consume_band(qi, maxima, state):
        c = qi - 2
        new = []
        for hh in range(2):
            m_old, acc = state[hh]
            m_left = jnp.maximum(m_old[:, :BLK], maxima[hh])
            p = jnp.exp2(band_buf[hidx[hh], :BLK, :BLK] - m_left).astype(BF16)
            vth = vt_ref[0, c, hh * HEAD_ROWS:(hh + 1) * HEAD_ROWS, BLK:]
            left = jnp.exp2(m_old[:, :BLK] - m_left) * acc[:, :BLK] + jnp.dot(vth, p, preferred_element_type=F32)
            new.append((jnp.concatenate([m_left, m_old[:, BLK:]], axis=1),
                        jnp.concatenate([left, acc[:, BLK:]], axis=1)))
        return tuple(new)

    ws = weights(0)
    maxima = produce(0, ws, 0, own_bufs[0])
    for qi in range(nchunk):
        state = fresh_state()
        ws_next = weights(qi + 1) if qi + 1 < nchunk else None
        own_src, next_dst = own_bufs[qi % 2], own_bufs[(qi + 1) % 2]

        def next_own_scores(qi=qi, ws_next=ws_next, next_dst=next_dst):
            return produce(qi + 1, ws_next, qi + 1, next_dst)

        if qi < NEAR_CHUNKS:
            src = own_src
            for j, c in enumerate(range(qi, -1, -1)):
                if c > 0:
                    dst = far_bufs[j % 2]
                    nxt = produce(qi, ws, c - 1, dst)
                elif ws_next is not None:
                    next_own = next_own_scores()
                state = consume(c, src, maxima, state, own=(j == 0))
                if c > 0:
                    src, maxima = dst, nxt
        else:
            nxt = produce(qi, ws, qi - 1, far_bufs[0])
            state = consume(qi, own_src, maxima, state, own=True)
            band_mx = produce_band(qi, ws)
            state = consume(qi - 1, far_bufs[0], nxt, state)

            def band_only(st, qi=qi, band_mx=band_mx, ws_next=ws_next):
                own_mx = next_own_scores() if ws_next is not None else band_mx
                return consume_band(qi, band_mx, st), own_mx

            def everything(st, qi=qi, ws=ws, ws_next=ws_next, band_mx=band_mx):
                own_mx = next_own_scores() if ws_next is not None else band_mx
                first_far = qi - 2
                mx = produce(qi, ws, first_far, far_bufs[0])
                for i, c in enumerate(range(first_far, -1, -1)):
                    nx = produce(qi, ws, c - 1, far_bufs[(i + 1) % 2]) if c > 0 else None
                    st = consume(c, far_bufs[i % 2], mx, st)
                    mx = nx
                return st, own_mx

            state, next_own = lax.cond(only_band_matters(qi, state), band_only, everything, state)
        finish(qi, state)
        ws, maxima = ws_next, next_own


def _block_indicator():
    ind = np.zeros((CHUNK, PAIR_W), np.float32)
    for r in range(BLOCKS_PER_CHUNK):
        ind[r * BLK:(r + 1) * BLK, r] = 1.0
    ind[:, BLOCKS_PER_CHUNK:BLOCKS_PER_CHUNK + N_SPLIT] = 1.0
    return jnp.asarray(ind, BF16)


def _attention(qt, k, vt, sel=None, tables=None, rel_bias=None):
    moba = sel is not None
    B, S, _ = k.shape
    nchunk = S // CHUNK
    kw = k.shape[-1] // N_PAIRS
    vt_off = 0 if moba else N_PAIRS
    in_specs = [pl.BlockSpec((1, nchunk, PAIR_W, CHUNK), lambda b, p: (b, 0, p, 0)),
                pl.BlockSpec((1, S, kw), lambda b, p: (b, 0, p)),
                pl.BlockSpec((1, nchunk, PAIR_ROWS, CHUNK), lambda b, p: (b, 0, p + vt_off, 0))]
    args = [qt, k, vt]
    if moba:
        in_specs = [pl.BlockSpec(memory_space=pltpu.SMEM)] + in_specs
        in_specs += [pl.BlockSpec((1, 2 * MAX_BLOCKS, S), lambda b, p: (b, p, 0)),
                     pl.BlockSpec((2, CHUNK, CHUNK), lambda b, p: (p, 0, 0)),
                     pl.BlockSpec((2, CORNER, CORNER), lambda b, p: (p, 0, 0)),
                     pl.BlockSpec((CHUNK, PAIR_W), lambda b, p: (0, 0))]
        args = [rel_bias] + args + [sel, *tables, _block_indicator()]
    return pl.pallas_call(
        functools.partial(_attn_kernel, moba=moba, nchunk=nchunk),
        grid=(B, N_PAIRS),
        in_specs=in_specs,
        out_specs=pl.BlockSpec((1, S, PAIR_W), lambda b, p: (b, 0, p)),
        out_shape=jax.ShapeDtypeStruct((B, S, ATT_W), BF16),
        scratch_shapes=[pltpu.VMEM((2, CHUNK, CHUNK), F32)] * (2 if moba else 4),
        compiler_params=pltpu.CompilerParams(
            dimension_semantics=("arbitrary", "arbitrary"), vmem_limit_bytes=VMEM_LIMIT),
        name="moba_attn" if moba else "fox_attn",
    )(*args)


def _out_ffn_kernel(moba_ref, fox_ref, conv_ref, x_ref, wo_ref, wg_ref, wu_ref, wd_ref,
                    gpm_ref, gpf_ref, gqf_ref, o_ref):
    mix_in = jnp.concatenate([moba_ref[...], fox_ref[...], conv_ref[...]], axis=1)
    mixed = jnp.dot(mix_in, wo_ref[...], preferred_element_type=F32)
    x1 = x_ref[...] + _rms(mixed, gpm_ref[...])
    h2 = _rms(x1, gpf_ref[...]).astype(BF16)
    ff = jnp.zeros(x1.shape, F32)
    lo = 0
    for width in FFN_CHUNKS:
        gate = jnp.dot(h2, wg_ref[:, lo:lo + width], preferred_element_type=F32)
        up = jnp.dot(h2, wu_ref[:, lo:lo + width], preferred_element_type=F32)
        act = (gate * (1.0 / (1.0 + jnp.exp(-gate))) * up).astype(BF16)
        ff = ff + jnp.dot(act, wd_ref[lo:lo + width, :], preferred_element_type=F32)
        lo += width
    o_ref[...] = x1 + _rms(ff, gqf_ref[...])


def _out_ffn(moba, fox, conv, x, wo, wg, wu, wd, gpm, gpf, gqf):
    M, D = x.shape
    tm = FFN_TILE
    row = lambda w: pl.BlockSpec((tm, w), lambda m: (m, 0))
    const = lambda shape: pl.BlockSpec(shape, lambda m: (0, 0), pipeline_mode=pl.Buffered(1))
    return pl.pallas_call(
        _out_ffn_kernel,
        grid=(M // tm,),
        in_specs=[row(ATT_W), row(ATT_W), row(CONV_CH), row(D),
                  const(wo.shape), const(wg.shape), const(wu.shape), const(wd.shape),
                  const((1, D)), const((1, D)), const((1, D))],
        out_specs=row(D),
        out_shape=jax.ShapeDtypeStruct((M, D), F32),
        compiler_params=pltpu.CompilerParams(
            dimension_semantics=("arbitrary",), vmem_limit_bytes=VMEM_LIMIT),
        name="out_ffn",
    )(moba, fox, conv, x, wo, wg, wu, wd, gpm, gpf, gqf)


def _cumsum_constants():
    tri = np.tril(np.ones((BLK, BLK), np.float32))
    perm = np.zeros((N_SPLIT * LANES, ATT_W), np.float32)
    for s in range(N_SPLIT):
        for h in range(N_HEADS):
            perm[s * LANES + h, PAIR_W * (h // 2) + N_SPLIT * (h % 2) + s] = 1.0
    return jnp.asarray(tri, BF16), jnp.asarray(perm, BF16)


def kernel(x, w_in, b_f, conv_w, w_out, rel_bias, g_pre_mix, g_post_mix, g_pre_ffn, g_post_ffn,
           w_gate, w_up, w_down):
    B, S, D = x.shape
    depth = w_in.shape[0]
    assert D == D_MODEL and S % CHUNK == 0 and S // BLK <= MAX_BLOCKS
    assert (B * S) % FFN_TILE == 0

    rel_bias = rel_bias.astype(F32)
    tables = _bias_tiles(rel_bias)
    w_in_t = jnp.swapaxes(w_in, 1, 2)
    tri, perm = _cumsum_constants()
    a = ATT_W
    for l in range(depth):
        wl = w_in_t[l]
        wqkv = wl[0:6 * a].astype(BF16)
        wtail = jnp.concatenate([wl[6 * a:6 * a + N_HEADS], jnp.zeros((LANES - N_HEADS, D), wl.dtype),
                                 wl[6 * a + N_HEADS:]], axis=0).astype(BF16)
        bfp = jnp.pad(b_f[l].astype(F32), (0, LANES - N_HEADS)).reshape(1, LANES)
        row = lambda g: g[l].astype(F32).reshape(1, D)

        mq, mk, fq, fkp, vt, sel, conv = _inproj(
            x, row(g_pre_mix), wqkv, wtail, bfp, conv_w[l].astype(F32), tri, perm)
        moba = _attention(mq, mk, vt, sel, tables, rel_bias)
        fox = _attention(fq, fkp, vt)
        x = _out_ffn(moba.reshape(B * S, a), fox.reshape(B * S, a), conv.reshape(B * S, CONV_CH),
                     x.reshape(B * S, D), w_out[l].astype(BF16), w_gate[l].astype(BF16),
                     w_up[l].astype(BF16), w_down[l].astype(BF16),
                     row(g_post_mix), row(g_pre_ffn), row(g_post_ffn)).reshape(B, S, D)
    return x
```

```python
import functools
import math

import jax
import jax.numpy as jnp
import numpy as np
from jax import lax
from jax.experimental import pallas as pl
from jax.experimental.pallas import tpu as pltpu

D_MODEL = 1024
HEAD_DIM = 64
N_HEADS = 6
ATT_W = N_HEADS * HEAD_DIM
PAIR_W = 2 * HEAD_DIM
N_PAIRS = N_HEADS // 2
CONV_CH = 256
CONV_WIDTH = 3
BLK = 256
BLK_SHIFT = BLK.bit_length() - 1
CHUNK = 512
BLOCKS_PER_CHUNK = CHUNK // BLK
assert BLOCKS_PER_CHUNK == 2
MAX_BLOCKS = 16
MAX_BLOCKS_SHIFT = MAX_BLOCKS.bit_length() - 1
HEAD_SHIFT = HEAD_DIM.bit_length() - 1
BF16_TILE_ROWS = 16
TOPK = 3
REL_BUCKETS = 32
REL_MAX_DIST = 128
D_FF = 2816
RMS_EPS = 1e-6
NEG = -1e30
LANES = 128
SUBLANES = 8
N_SPLIT = 3
LOG2E = math.log2(math.e)
CORNER = REL_MAX_DIST
ONES_ROWS = BF16_TILE_ROWS
HEAD_ROWS = HEAD_DIM + ONES_ROWS
PAIR_ROWS = 2 * HEAD_ROWS
XROWS = BF16_TILE_ROWS
NEAR_CHUNKS = 2
UNDERFLOW_BITS = 152.0

F32 = jnp.float32
BF16 = jnp.bfloat16
NT_DIMS = (((1,), (1,)), ((), ()))

FFN_TILE = 512
FFN_CHUNKS = (1536, 1280)
assert sum(FFN_CHUNKS) == D_FF
VMEM_LIMIT = 50 * 1024 * 1024


def _split_bf16(v, n):
    terms = []
    for _ in range(n):
        t = v.astype(BF16)
        terms.append(t)
        v = v - t.astype(F32)
    return terms


def _rms(v, g):
    return v * lax.rsqrt(jnp.mean(v * v, axis=-1, keepdims=True) + RMS_EPS) * g


def _bucket_tiles():
    max_exact = REL_BUCKETS // 2

    def bucket(dist):
        n = np.maximum(dist, 0)
        nf = np.maximum(n, 1).astype(np.float32)
        large = max_exact + (np.log(nf / np.float32(max_exact))
                             / np.float32(math.log(REL_MAX_DIST / max_exact))
                             * np.float32(REL_BUCKETS - max_exact)).astype(np.int32)
        large = np.minimum(large, REL_BUCKETS - 1)
        return np.where(n < max_exact, n, large).astype(np.int32)

    key = np.arange(CHUNK, dtype=np.int32)[:, None]
    qry = np.arange(CHUNK, dtype=np.int32)[None, :]
    d0 = qry - key
    own = np.where(d0 >= 0, bucket(d0), -1).astype(np.int32)
    prev = bucket(d0 + CHUNK)
    corner = prev[CHUNK - CORNER:, :CORNER]
    outside = prev.copy()
    outside[CHUNK - CORNER:, :CORNER] = REL_BUCKETS - 1
    assert (outside == REL_BUCKETS - 1).all()
    return own, np.ascontiguousarray(corner)


def _bias_tile_kernel(rb_ref, own_bm_ref, corner_bm_ref, own_ref, corner_ref):
    h = pl.program_id(0)
    for bm_ref, out_ref in ((own_bm_ref, own_ref), (corner_bm_ref, corner_ref)):
        bm = bm_ref[...]
        t = jnp.full(bm.shape, NEG, F32)
        for b in range(REL_BUCKETS):
            t = jnp.where(bm == b, rb_ref[b, h] * LOG2E, t)
        out_ref[0] = t


def _bias_tiles(rel_bias):
    own_bm, corner_bm = _bucket_tiles()
    return pl.pallas_call(
        _bias_tile_kernel,
        grid=(N_HEADS,),
        in_specs=[pl.BlockSpec(memory_space=pltpu.SMEM),
                  pl.BlockSpec((CHUNK, CHUNK), lambda h: (0, 0)),
                  pl.BlockSpec((CORNER, CORNER), lambda h: (0, 0))],
        out_specs=[pl.BlockSpec((1, CHUNK, CHUNK), lambda h: (h, 0, 0)),
                   pl.BlockSpec((1, CORNER, CORNER), lambda h: (h, 0, 0))],
        out_shape=[jax.ShapeDtypeStruct((N_HEADS, CHUNK, CHUNK), F32),
                   jax.ShapeDtypeStruct((N_HEADS, CORNER, CORNER), F32)],
        name="bias_tiles",
    )(rel_bias, jnp.asarray(own_bm), jnp.asarray(corner_bm))


def _inproj_kernel(x_ref, g_ref, wqkv_ref, wtail_ref, bf_ref, cw_ref, tri_ref, perm_ref,
                   mq_ref, mk_ref, fq_ref, fkp_ref, vt_ref, sel_ref, conv_ref,
                   km_ref, cum_ref, uc_ref):
    tm = CHUNK
    st = pl.program_id(1)

    @pl.when(st == 0)
    def _():
        km_ref[...] = jnp.zeros_like(km_ref)
        cum_ref[...] = jnp.zeros_like(cum_ref)
        uc_ref[...] = jnp.zeros_like(uc_ref)

    hn = _rms(x_ref[0], g_ref[...]).astype(BF16)

    qkv = lax.dot_general(hn, wqkv_ref[...], NT_DIMS, preferred_element_type=F32)
    tail = lax.dot_general(hn, wtail_ref[...], NT_DIMS, preferred_element_type=F32)
    mq = qkv[:, 0 * ATT_W:1 * ATT_W]
    mk = qkv[:, 1 * ATT_W:2 * ATT_W]
    mv = qkv[:, 2 * ATT_W:3 * ATT_W]
    fq = qkv[:, 3 * ATT_W:4 * ATT_W]
    fk = qkv[:, 4 * ATT_W:5 * ATT_W]
    fv = qkv[:, 5 * ATT_W:6 * ATT_W]
    scale = HEAD_DIM ** -0.5 * LOG2E
    mqt = mq.T
    mq_ref[0, 0] = (mqt * scale).astype(BF16)
    mk_ref[0] = mk.astype(BF16)
    fq_ref[0, 0] = (fq * scale).T.astype(BF16)
    ones = jnp.ones((ONES_ROWS, tm), BF16)
    for kind, v in enumerate((mv, fv)):
        vt = v.T.astype(BF16)
        for h in range(N_HEADS):
            base = (kind * N_HEADS + h) * HEAD_ROWS
            vt_ref[0, 0, base:base + HEAD_DIM, :] = vt[h * HEAD_DIM:(h + 1) * HEAD_DIM]
            vt_ref[0, 0, base + HEAD_DIM:base + HEAD_ROWS, :] = ones

    rows = lax.broadcasted_iota(jnp.int32, km_ref.shape, 0)
    cols = lax.broadcasted_iota(jnp.int32, km_ref.shape, 1)
    same_head = (rows >> MAX_BLOCKS_SHIFT) == (cols >> HEAD_SHIFT)
    km = km_ref[...]
    for r in range(BLOCKS_PER_CHUNK):
        kmean = jnp.mean(mk[r * BLK:(r + 1) * BLK], axis=0, keepdims=True)
        n = st * BLOCKS_PER_CHUNK + r
        km = jnp.where(same_head & ((rows & (MAX_BLOCKS - 1)) == n), kmean, km)
    km_ref[...] = km

    km_hi, km_lo = _split_bf16(km, 2)
    q_hi, q_lo = _split_bf16(mqt, 2)
    gate = (jnp.dot(km_hi, q_hi, preferred_element_type=F32)
            + jnp.dot(km_hi, q_lo, preferred_element_type=F32)
            + jnp.dot(km_lo, q_hi, preferred_element_type=F32))

    nidx = lax.broadcasted_iota(jnp.int32, (MAX_BLOCKS, tm), 0)
    own = st * BLOCKS_PER_CHUNK + (lax.broadcasted_iota(jnp.int32, (MAX_BLOCKS, tm), 1) >> BLK_SHIFT)
    valid = nidx < own
    for h in range(N_HEADS):
        gh = jnp.where(valid, gate[h * MAX_BLOCKS:(h + 1) * MAX_BLOCKS], -jnp.inf)
        rank = jnp.zeros((MAX_BLOCKS, tm), jnp.int32)
        for m in range(MAX_BLOCKS):
            gm = gh[m:m + 1, :]
            gt = (gm > gh).astype(jnp.int32)
            ge = (gm >= gh).astype(jnp.int32)
            rank = rank + jnp.where(nidx > m, ge, gt)
        keep = ((rank < TOPK) & valid) | (nidx == own)
        sel_ref[0, h * MAX_BLOCKS:(h + 1) * MAX_BLOCKS, :] = jnp.where(keep, 0.0, NEG)

    fl = tail[:, 0:LANES] + bf_ref[...]
    lf = jnp.minimum(fl, 0.0) - jnp.log1p(jnp.exp(-jnp.abs(fl)))
    lf_terms = jnp.concatenate(_split_bf16(lf, N_SPLIT), axis=1)
    tri = tri_ref[...]
    carry = cum_ref[...]
    groups = []
    for r in range(BLOCKS_PER_CHUNK):
        part = jnp.dot(tri, lf_terms[r * BLK:(r + 1) * BLK], preferred_element_type=F32)
        grp = carry + sum(part[:, t * LANES:(t + 1) * LANES] for t in range(N_SPLIT))
        carry = grp[BLK - 1:BLK, :]
        groups.append(grp)
    cum = jnp.concatenate(groups, axis=0)
    cum_ref[...] = carry
    neg_terms = jnp.concatenate(_split_bf16(cum * -LOG2E, N_SPLIT), axis=1)
    kb = jnp.dot(neg_terms, perm_ref[...], preferred_element_type=F32).astype(BF16)
    fkb = fk.astype(BF16)
    for p in range(N_PAIRS):
        fkp_ref[0, :, 2 * p * PAIR_W:(2 * p + 1) * PAIR_W] = fkb[:, p * PAIR_W:(p + 1) * PAIR_W]
        fkp_ref[0, :, (2 * p + 1) * PAIR_W:(2 * p + 2) * PAIR_W] = kb[:, p * PAIR_W:(p + 1) * PAIR_W]

    cv = tail[:, LANES:]
    cvb = cv[:, 0:CONV_CH]
    u = cv[:, CONV_CH:2 * CONV_CH] * cv[:, 2 * CONV_CH:3 * CONV_CH]
    uc = uc_ref[...]
    prev1 = uc[SUBLANES - 1:SUBLANES]
    prev2 = uc[SUBLANES - 2:SUBLANES - 1]
    ridx = lax.broadcasted_iota(jnp.int32, u.shape, 0)
    u1 = jnp.where(ridx == 0, prev1, pltpu.roll(u, 1, 0))
    u2 = jnp.where(ridx == 0, prev2, jnp.where(ridx == 1, prev1, pltpu.roll(u, 2, 0)))
    cw = cw_ref[...]
    y = cw[0:1] * u2 + cw[1:2] * u1 + cw[2:3] * u
    conv_ref[0] = (cvb * y).astype(BF16)
    uc_ref[...] = u[tm - SUBLANES:tm]


def _inproj(x, g, wqkv, wtail, bfp, cw, tri, perm):
    B, S, D = x.shape
    tm = CHUNK
    const = lambda shape: pl.BlockSpec(shape, lambda b, s: (0,) * len(shape))
    row = lambda w: pl.BlockSpec((1, tm, w), lambda b, s: (b, s, 0))
    colmajor = lambda w: pl.BlockSpec((1, 1, w, tm), lambda b, s: (b, s, 0, 0))
    return pl.pallas_call(
        _inproj_kernel,
        grid=(B, S // tm),
        in_specs=[row(D), const((1, D)), const(wqkv.shape), const(wtail.shape),
                  const(bfp.shape), const(cw.shape), const(tri.shape), const(perm.shape)],
        out_specs=[colmajor(ATT_W), row(ATT_W), colmajor(ATT_W), row(2 * ATT_W),
                   pl.BlockSpec((1, 1, 2 * N_HEADS * HEAD_ROWS, tm), lambda b, s: (b, s, 0, 0)),
                   pl.BlockSpec((1, N_HEADS * MAX_BLOCKS, tm), lambda b, s: (b, 0, s)),
                   row(CONV_CH)],
        out_shape=[jax.ShapeDtypeStruct((B, S // tm, ATT_W, tm), BF16),
                   jax.ShapeDtypeStruct((B, S, ATT_W), BF16),
                   jax.ShapeDtypeStruct((B, S // tm, ATT_W, tm), BF16),
                   jax.ShapeDtypeStruct((B, S, 2 * ATT_W), BF16),
                   jax.ShapeDtypeStruct((B, S // tm, 2 * N_HEADS * HEAD_ROWS, tm), BF16),
                   jax.ShapeDtypeStruct((B, N_HEADS * MAX_BLOCKS, S), F32),
                   jax.ShapeDtypeStruct((B, S, CONV_CH), BF16)],
        scratch_shapes=[pltpu.VMEM((N_HEADS * MAX_BLOCKS, ATT_W), F32),
                        pltpu.VMEM((1, LANES), F32),
                        pltpu.VMEM((SUBLANES, CONV_CH), F32)],
        compiler_params=pltpu.CompilerParams(
            dimension_semantics=("arbitrary", "arbitrary"), vmem_limit_bytes=VMEM_LIMIT),
        name="inproj",
    )(x, g, wqkv, wtail, bfp, cw, tri, perm)


def _attn_kernel(*refs, moba, nchunk):
    if moba:
        rb_ref, qt_ref, k_ref, vt_ref, sel_ref, own_ref, corner_ref, ind_ref, o_ref, *bufs = refs
        pair = pl.program_id(1)
        xrow = lax.broadcasted_iota(jnp.int32, (XROWS, CHUNK), 0)
        far_rows, near_fix = [], []
        for hh in range(2):
            fb = rb_ref[REL_BUCKETS - 1, 2 * pair + hh] * LOG2E
            terms = _split_bf16(jnp.full((XROWS, CHUNK), fb, F32), N_SPLIT)
            rows = jnp.zeros((XROWS, CHUNK), F32)
            for t in range(N_SPLIT):
                rows = jnp.where(xrow == BLOCKS_PER_CHUNK + t, terms[t].astype(F32), rows)
            far_rows.append(rows)
            near_fix.append(corner_ref[hh] - fb)
        xpad = jnp.zeros((PAIR_W - XROWS, CHUNK), BF16)
    else:
        qt_ref, k_ref, vt_ref, o_ref, *bufs = refs
    hidx = [jnp.minimum(pl.program_id(0), 0) + hh for hh in range(2)]

    def weights(qi):
        qt = qt_ref[0, qi]
        frow = lax.broadcasted_iota(jnp.int32, qt.shape, 0)
        zero = jnp.zeros_like(qt)
        ws = []
        for hh in range(2):
            w = jnp.where((frow >= hh * HEAD_DIM) & (frow < (hh + 1) * HEAD_DIM), qt, zero)
            if not moba:
                pick = (frow >= N_SPLIT * hh) & (frow < N_SPLIT * (hh + 1))
                w = jnp.concatenate([w, jnp.where(pick, 1.0, 0.0).astype(BF16)], axis=0)
            ws.append(w)
        return ws

    def query_norms(qi):
        q2 = qt_ref[0, qi].astype(F32) ** 2
        return [jnp.sqrt(jnp.sum(q2[hh * HEAD_DIM:(hh + 1) * HEAD_DIM], axis=0, keepdims=True))
                for hh in range(2)]

    def produce(qi, ws, c, dst_ref):
        own, near = c == qi, c == qi - 1
        kc = k_ref[0, c * CHUNK:(c + 1) * CHUNK, :]
        if moba:
            kc = jnp.concatenate([kc, ind_ref[...]], axis=1)
        maxima = []
        for hh in range(2):
            w = ws[hh]
            if moba:
                rows = jnp.zeros((XROWS, CHUNK), F32) if own else far_rows[hh]
                for r in range(BLOCKS_PER_CHUNK):
                    n = hh * MAX_BLOCKS + c * BLOCKS_PER_CHUNK + r
                    sel = sel_ref[0, n:n + 1, qi * CHUNK:(qi + 1) * CHUNK]
                    rows = jnp.where(xrow == r, sel, rows)
                w = jnp.concatenate([w, rows.astype(BF16), xpad], axis=0)
            if own:
                top = jnp.dot(kc[:BLK], w, preferred_element_type=F32)
                bot = jnp.dot(kc[BLK:], w[:, BLK:], preferred_element_type=F32)
                if moba:
                    top = top + own_ref[hh, :BLK, :]
                    bot = bot + own_ref[hh, BLK:, BLK:]
                else:
                    kidx = lax.broadcasted_iota(jnp.int32, (BLK, BLK), 0)
                    qidx = lax.broadcasted_iota(jnp.int32, (BLK, BLK), 1)
                    causal = kidx <= qidx
                    top = jnp.concatenate([jnp.where(causal, top[:, :BLK], NEG), top[:, BLK:]], axis=1)
                    bot = jnp.where(causal, bot, NEG)
                dst_ref[hidx[hh], :BLK, :] = top
                dst_ref[hidx[hh], BLK:, BLK:] = bot
                mt = jnp.max(top, axis=0, keepdims=True)
                mb = jnp.max(bot, axis=0, keepdims=True)
                maxima.append(jnp.concatenate([mt[:, :BLK], jnp.maximum(mt[:, BLK:], mb)], axis=1))
                continue
            s = jnp.dot(kc, w, preferred_element_type=F32)
            if moba and near:
                lo = CHUNK - CORNER
                fixed = jnp.concatenate([s[lo:, :CORNER] + near_fix[hh], s[lo:, CORNER:]], axis=1)
                s = jnp.concatenate([s[:lo], fixed], axis=0)
            dst_ref[hidx[hh]] = s
            maxima.append(jnp.max(s, axis=0, keepdims=True))
        return tuple(maxima)

    def consume(c, src_ref, maxima, state, own=False):
        new = []
        for hh in range(2):
            m_old, acc = state[hh]
            m_new = jnp.maximum(m_old, maxima[hh])
            vth = vt_ref[0, c, hh * HEAD_ROWS:(hh + 1) * HEAD_ROWS, :]
            if own:
                p_top = jnp.exp2(src_ref[hidx[hh], :BLK, :] - m_new).astype(BF16)
                p_bot = jnp.exp2(src_ref[hidx[hh], BLK:, BLK:] - m_new[:, BLK:]).astype(BF16)
                pv_top = jnp.dot(vth[:, :BLK], p_top, preferred_element_type=F32)
                pv_bot = jnp.dot(vth[:, BLK:], p_bot, preferred_element_type=F32)
                pv = jnp.concatenate([pv_top[:, :BLK], pv_top[:, BLK:] + pv_bot], axis=1)
            else:
                p = jnp.exp2(src_ref[hidx[hh]] - m_new).astype(BF16)
                pv = jnp.dot(vth, p, preferred_element_type=F32)
            new.append((m_new, jnp.exp2(m_old - m_new) * acc + pv))
        return tuple(new)

    def fresh_state():
        return tuple((jnp.full((1, CHUNK), NEG, F32), jnp.zeros((HEAD_ROWS, CHUNK), F32))
                     for _ in range(2))

    def finish(qi, state):
        ot = jnp.concatenate([acc[:HEAD_DIM] * (1.0 / acc[HEAD_DIM:HEAD_DIM + 1])
                              for (_, acc) in state], axis=0)
        o_ref[0, qi * CHUNK:(qi + 1) * CHUNK, :] = ot.T.astype(BF16)

    def dense_program():
        g = 0
        ws = weights(0)
        maxima = produce(0, ws, 0, bufs[0])
        for qi in range(nchunk):
            state = fresh_state()
            ws_next = weights(qi + 1) if qi + 1 < nchunk else None
            for c in range(qi, -1, -1):
                nxt = None
                if c > 0:
                    nxt = produce(qi, ws, c - 1, bufs[(g + 1) % 2])
                elif ws_next is not None:
                    nxt = produce(qi + 1, ws_next, qi + 1, bufs[(g + 1) % 2])
                state = consume(c, bufs[g % 2], maxima, state, own=(c == qi))
                maxima = nxt
                g += 1
            ws = ws_next
            finish(qi, state)

    if moba:
        dense_program()
        return

    own_bufs, far_bufs = bufs[0:2], bufs[2:4]
    kall = k_ref[0][:, 0:PAIR_W].astype(F32) ** 2
    lane = lax.broadcasted_iota(jnp.int32, (1, PAIR_W), 1)
    key_norm = [jnp.sqrt(jnp.max(jnp.sum(jnp.where((lane >= hh * HEAD_DIM) & (lane < (hh + 1) * HEAD_DIM),
                                                      kall, 0.0), axis=1, keepdims=True)))
                for hh in range(2)]

    def gap(qn, decay_rows, hh, m, newer_queries_only):
        mine = (lane >= N_SPLIT * hh) & (lane < N_SPLIT * (hh + 1))
        newest = jnp.max(jnp.sum(jnp.where(mine, decay_rows, 0.0), axis=1, keepdims=True))
        bound = qn[hh] * key_norm[hh] + newest - m
        if newer_queries_only:
            qlane = lax.broadcasted_iota(jnp.int32, bound.shape, 1)
            bound = jnp.where(qlane >= BLK, bound, -jnp.inf)
        return jnp.max(bound)

    def skipped_gap(qi, state):
        base = (qi - 2) * CHUNK
        older_blk = k_ref[0, base + BLK - BF16_TILE_ROWS:base + BLK, PAIR_W:2 * PAIR_W].astype(F32)
        newer_blk = k_ref[0, base + CHUNK - BF16_TILE_ROWS:base + CHUNK, PAIR_W:2 * PAIR_W].astype(F32)
        qn = query_norms(qi)
        worst = None
        for hh in range(2):
            g = jnp.maximum(gap(qn, older_blk, hh, state[hh][0], False),
                            gap(qn, newer_blk, hh, state[hh][0], True))
            worst = g if worst is None else jnp.maximum(worst, g)
        return worst

    band_buf = far_bufs[1]

    def produce_band(qi, ws):
        c = qi - 2
        kq = k_ref[0, c * CHUNK + BLK:(c + 1) * CHUNK, :]
        maxima = []
        for hh in range(2):
            s = jnp.dot(kq, ws[hh][:, :BLK], preferred_element_type=F32)
            band_buf[hidx[hh], :BLK, :BLK] = s
            maxima.append(jnp.max(s, axis=0, keepdims=True))
        return tuple(maxima)

    def consume_band(qi, maxima, state):
        c = qi - 2
        new = []
        for hh in range(2):
            m_old, acc = state[hh]
            m_left = jnp.maximum(m_old[:, :BLK], maxima[hh])
            p = jnp.exp2(band_buf[hidx[hh], :BLK, :BLK] - m_left).astype(BF16)
            vth = vt_ref[0, c, hh * HEAD_ROWS:(hh + 1) * HEAD_ROWS, BLK:]
            left = jnp.exp2(m_old[:, :BLK] - m_left) * acc[:, :BLK] + jnp.dot(vth, p, preferred_element_type=F32)
            new.append((jnp.concatenate([m_left, m_old[:, BLK:]], axis=1),
                        jnp.concatenate([left, acc[:, BLK:]], axis=1)))
        return tuple(new)

    ws = weights(0)
    maxima = produce(0, ws, 0, own_bufs[0])
    worst = jnp.float32(-jnp.inf)
    for qi in range(nchunk):
        state = fresh_state()
        ws_next = weights(qi + 1) if qi + 1 < nchunk else None
        own_src, next_dst = own_bufs[qi % 2], own_bufs[(qi + 1) % 2]
        chunks = list(range(qi, max(qi - NEAR_CHUNKS, -1), -1))
        band = qi >= NEAR_CHUNKS
        src = own_src
        for j, c in enumerate(chunks):
            last = j == len(chunks) - 1
            if not last:
                dst = far_bufs[j % 2]
                nxt = produce(qi, ws, c - 1, dst)
            elif band:
                band_mx = produce_band(qi, ws)
            elif ws_next is not None:
                next_own = produce(qi + 1, ws_next, qi + 1, next_dst)
            state = consume(c, src, maxima, state, own=(j == 0))
            if not last:
                src, maxima = dst, nxt
        if band:
            worst = jnp.maximum(worst, skipped_gap(qi, state))
            if ws_next is not None:
                next_own = produce(qi + 1, ws_next, qi + 1, next_dst)
            state = consume_band(qi, band_mx, state)
        finish(qi, state)
        ws, maxima = ws_next, next_own

    @pl.when(worst >= -UNDERFLOW_BITS)
    def _():
        dense_program()


def _block_indicator():
    ind = np.zeros((CHUNK, PAIR_W), np.float32)
    for r in range(BLOCKS_PER_CHUNK):
        ind[r * BLK:(r + 1) * BLK, r] = 1.0
    ind[:, BLOCKS_PER_CHUNK:BLOCKS_PER_CHUNK + N_SPLIT] = 1.0
    return jnp.asarray(ind, BF16)


def _attention(qt, k, vt, sel=None, tables=None, rel_bias=None):
    moba = sel is not None
    B, S, _ = k.shape
    nchunk = S // CHUNK
    kw = k.shape[-1] // N_PAIRS
    vt_off = 0 if moba else N_PAIRS
    in_specs = [pl.BlockSpec((1, nchunk, PAIR_W, CHUNK), lambda b, p: (b, 0, p, 0)),
                pl.BlockSpec((1, S, kw), lambda b, p: (b, 0, p)),
                pl.BlockSpec((1, nchunk, PAIR_ROWS, CHUNK), lambda b, p: (b, 0, p + vt_off, 0))]
    args = [qt, k, vt]
    if moba:
        in_specs = [pl.BlockSpec(memory_space=pltpu.SMEM)] + in_specs
        in_specs += [pl.BlockSpec((1, 2 * MAX_BLOCKS, S), lambda b, p: (b, p, 0)),
                     pl.BlockSpec((2, CHUNK, CHUNK), lambda b, p: (p, 0, 0)),
                     pl.BlockSpec((2, CORNER, CORNER), lambda b, p: (p, 0, 0)),
                     pl.BlockSpec((CHUNK, PAIR_W), lambda b, p: (0, 0))]
        args = [rel_bias] + args + [sel, *tables, _block_indicator()]
    return pl.pallas_call(
        functools.partial(_attn_kernel, moba=moba, nchunk=nchunk),
        grid=(B, N_PAIRS),
        in_specs=in_specs,
        out_specs=pl.BlockSpec((1, S, PAIR_W), lambda b, p: (b, 0, p)),
        out_shape=jax.ShapeDtypeStruct((B, S, ATT_W), BF16),
        scratch_shapes=[pltpu.VMEM((2, CHUNK, CHUNK), F32)] * (2 if moba else 4),
        compiler_params=pltpu.CompilerParams(
            dimension_semantics=("arbitrary", "arbitrary"), vmem_limit_bytes=VMEM_LIMIT),
        name="moba_attn" if moba else "fox_attn",
    )(*args)


def _out_ffn_kernel(moba_ref, fox_ref, conv_ref, x_ref, wo_ref, wg_ref, wu_ref, wd_ref,
                    gpm_ref, gpf_ref, gqf_ref, o_ref):
    mix_in = jnp.concatenate([moba_ref[...], fox_ref[...], conv_ref[...]], axis=1)
    mixed = jnp.dot(mix_in, wo_ref[...], preferred_element_type=F32)
    x1 = x_ref[...] + _rms(mixed, gpm_ref[...])
    h2 = _rms(x1, gpf_ref[...]).astype(BF16)
    ff = jnp.zeros(x1.shape, F32)
    lo = 0
    for width in FFN_CHUNKS:
        gate = jnp.dot(h2, wg_ref[:, lo:lo + width], preferred_element_type=F32)
        up = jnp.dot(h2, wu_ref[:, lo:lo + width], preferred_element_type=F32)
        act = (gate * (1.0 / (1.0 + jnp.exp(-gate))) * up).astype(BF16)
        ff = ff + jnp.dot(act, wd_ref[lo:lo + width, :], preferred_element_type=F32)
        lo += width
    o_ref[...] = x1 + _rms(ff, gqf_ref[...])


def _out_ffn(moba, fox, conv, x, wo, wg, wu, wd, gpm, gpf, gqf):
    M, D = x.shape
    tm = FFN_TILE
    row = lambda w: pl.BlockSpec((tm, w), lambda m: (m, 0))
    const = lambda shape: pl.BlockSpec(shape, lambda m: (0, 0), pipeline_mode=pl.Buffered(1))
    return pl.pallas_call(
        _out_ffn_kernel,
        grid=(M // tm,),
        in_specs=[row(ATT_W), row(ATT_W), row(CONV_CH), row(D),
                  const(wo.shape), const(wg.shape), const(wu.shape), const(wd.shape),
                  const((1, D)), const((1, D)), const((1, D))],
        out_specs=row(D),
        out_shape=jax.ShapeDtypeStruct((M, D), F32),
        compiler_params=pltpu.CompilerParams(
            dimension_semantics=("arbitrary",), vmem_limit_bytes=VMEM_LIMIT),
        name="out_ffn",
    )(moba, fox, conv, x, wo, wg, wu, wd, gpm, gpf, gqf)


def _cumsum_constants():
    tri = np.tril(np.ones((BLK, BLK), np.float32))
    perm = np.zeros((N_SPLIT * LANES, ATT_W), np.float32)
    for s in range(N_SPLIT):
        for h in range(N_HEADS):
            perm[s * LANES + h, PAIR_W * (h // 2) + N_SPLIT * (h % 2) + s] = 1.0
    return jnp.asarray(tri, BF16), jnp.asarray(perm, BF16)


def kernel(x, w_in, b_f, conv_w, w_out, rel_bias, g_pre_mix, g_post_mix, g_pre_ffn, g_post_ffn,
           w_gate, w_up, w_down):
    B, S, D = x.shape
    depth = w_in.shape[0]
    assert D == D_MODEL and S % CHUNK == 0 and S // BLK <= MAX_BLOCKS
    assert (B * S) % FFN_TILE == 0

    rel_bias = rel_bias.astype(F32)
    tables = _bias_tiles(rel_bias)
    w_in_t = jnp.swapaxes(w_in, 1, 2)
    tri, perm = _cumsum_constants()
    a = ATT_W
    for l in range(depth):
        wl = w_in_t[l]
        wqkv = wl[0:6 * a].astype(BF16)
        wtail = jnp.concatenate([wl[6 * a:6 * a + N_HEADS], jnp.zeros((LANES - N_HEADS, D), wl.dtype),
                                 wl[6 * a + N_HEADS:]], axis=0).astype(BF16)
        bfp = jnp.pad(b_f[l].astype(F32), (0, LANES - N_HEADS)).reshape(1, LANES)
        row = lambda g: g[l].astype(F32).reshape(1, D)

        mq, mk, fq, fkp, vt, sel, conv = _inproj(
            x, row(g_pre_mix), wqkv, wtail, bfp, conv_w[l].astype(F32), tri, perm)
        moba = _attention(mq, mk, vt, sel, tables, rel_bias)
        fox = _attention(fq, fkp, vt)
        x = _out_ffn(moba.reshape(B * S, a), fox.reshape(B * S, a), conv.reshape(B * S, CONV_CH),
                     x.reshape(B * S, D), w_out[l].astype(BF16), w_gate[l].astype(BF16),
                     w_up[l].astype(BF16), w_down[l].astype(BF16),
                     row(g_post_mix), row(g_pre_ffn), row(g_post_ffn)).reshape(B, S, D)
    return x
```

```python
import functools
import math

import jax
import jax.numpy as jnp
import numpy as np
from jax import lax
from jax.experimental import pallas as pl
from jax.experimental.pallas import tpu as pltpu

D_MODEL = 1024
HEAD_DIM = 64
N_HEADS = 6
ATT_W = N_HEADS * HEAD_DIM
PAIR_W = 2 * HEAD_DIM
N_PAIRS = N_HEADS // 2
CONV_CH = 256
CONV_WIDTH = 3
BLK = 256
BLK_SHIFT = BLK.bit_length() - 1
CHUNK = 512
BLOCKS_PER_CHUNK = CHUNK // BLK
assert BLOCKS_PER_CHUNK == 2
MAX_BLOCKS = 16
MAX_BLOCKS_SHIFT = MAX_BLOCKS.bit_length() - 1
HEAD_SHIFT = HEAD_DIM.bit_length() - 1
BF16_TILE_ROWS = 16
TOPK = 3
REL_BUCKETS = 32
REL_MAX_DIST = 128
D_FF = 2816
RMS_EPS = 1e-6
NEG = -1e30
LANES = 128
SUBLANES = 8
N_SPLIT = 3
LOG2E = math.log2(math.e)
CORNER = REL_MAX_DIST
ONES_ROWS = BF16_TILE_ROWS
HEAD_ROWS = HEAD_DIM + ONES_ROWS
PAIR_ROWS = 2 * HEAD_ROWS
XROWS = BF16_TILE_ROWS
NEAR_CHUNKS = 2
UNDERFLOW_BITS = 152.0

F32 = jnp.float32
BF16 = jnp.bfloat16
NT_DIMS = (((1,), (1,)), ((), ()))

FFN_TILE = 512
FFN_CHUNKS = (1536, 1280)
assert sum(FFN_CHUNKS) == D_FF
VMEM_LIMIT = 50 * 1024 * 1024


def _split_bf16(v, n):
    terms = []
    for _ in range(n):
        t = v.astype(BF16)
        terms.append(t)
        v = v - t.astype(F32)
    return terms


def _rms(v, g):
    return v * lax.rsqrt(jnp.mean(v * v, axis=-1, keepdims=True) + RMS_EPS) * g


def _bucket_tiles():
    max_exact = REL_BUCKETS // 2

    def bucket(dist):
        n = np.maximum(dist, 0)
        nf = np.maximum(n, 1).astype(np.float32)
        large = max_exact + (np.log(nf / np.float32(max_exact))
                             / np.float32(math.log(REL_MAX_DIST / max_exact))
                             * np.float32(REL_BUCKETS - max_exact)).astype(np.int32)
        large = np.minimum(large, REL_BUCKETS - 1)
        return np.where(n < max_exact, n, large).astype(np.int32)

    key = np.arange(CHUNK, dtype=np.int32)[:, None]
    qry = np.arange(CHUNK, dtype=np.int32)[None, :]
    d0 = qry - key
    own = np.where(d0 >= 0, bucket(d0), -1).astype(np.int32)
    prev = bucket(d0 + CHUNK)
    corner = prev[CHUNK - CORNER:, :CORNER]
    outside = prev.copy()
    outside[CHUNK - CORNER:, :CORNER] = REL_BUCKETS - 1
    assert (outside == REL_BUCKETS - 1).all()
    return own, np.ascontiguousarray(corner)


def _bias_tile_kernel(rb_ref, own_bm_ref, corner_bm_ref, own_ref, corner_ref):
    h = pl.program_id(0)
    for bm_ref, out_ref in ((own_bm_ref, own_ref), (corner_bm_ref, corner_ref)):
        bm = bm_ref[...]
        t = jnp.full(bm.shape, NEG, F32)
        for b in range(REL_BUCKETS):
            t = jnp.where(bm == b, rb_ref[b, h] * LOG2E, t)
        out_ref[0] = t


def _bias_tiles(rel_bias):
    own_bm, corner_bm = _bucket_tiles()
    return pl.pallas_call(
        _bias_tile_kernel,
        grid=(N_HEADS,),
        in_specs=[pl.BlockSpec(memory_space=pltpu.SMEM),
                  pl.BlockSpec((CHUNK, CHUNK), lambda h: (0, 0)),
                  pl.BlockSpec((CORNER, CORNER), lambda h: (0, 0))],
        out_specs=[pl.BlockSpec((1, CHUNK, CHUNK), lambda h: (h, 0, 0)),
                   pl.BlockSpec((1, CORNER, CORNER), lambda h: (h, 0, 0))],
        out_shape=[jax.ShapeDtypeStruct((N_HEADS, CHUNK, CHUNK), F32),
                   jax.ShapeDtypeStruct((N_HEADS, CORNER, CORNER), F32)],
        name="bias_tiles",
    )(rel_bias, jnp.asarray(own_bm), jnp.asarray(corner_bm))


def _inproj_kernel(x_ref, g_ref, wqkv_ref, wtail_ref, bf_ref, cw_ref, tri_ref, perm_ref,
                   mq_ref, mk_ref, fq_ref, fkp_ref, vt_ref, sel_ref, conv_ref,
                   km_ref, cum_ref, uc_ref):
    tm = CHUNK
    st = pl.program_id(1)

    @pl.when(st == 0)
    def _():
        km_ref[...] = jnp.zeros_like(km_ref)
        cum_ref[...] = jnp.zeros_like(cum_ref)
        uc_ref[...] = jnp.zeros_like(uc_ref)

    hn = _rms(x_ref[0], g_ref[...]).astype(BF16)

    qkv = lax.dot_general(hn, wqkv_ref[...], NT_DIMS, preferred_element_type=F32)
    tail = lax.dot_general(hn, wtail_ref[...], NT_DIMS, preferred_element_type=F32)
    mq = qkv[:, 0 * ATT_W:1 * ATT_W]
    mk = qkv[:, 1 * ATT_W:2 * ATT_W]
    mv = qkv[:, 2 * ATT_W:3 * ATT_W]
    fq = qkv[:, 3 * ATT_W:4 * ATT_W]
    fk = qkv[:, 4 * ATT_W:5 * ATT_W]
    fv = qkv[:, 5 * ATT_W:6 * ATT_W]
    scale = HEAD_DIM ** -0.5 * LOG2E
    mqt = mq.T
    mq_ref[0, 0] = (mqt * scale).astype(BF16)
    mk_ref[0] = mk.astype(BF16)
    fq_ref[0, 0] = (fq * scale).T.astype(BF16)
    ones = jnp.ones((ONES_ROWS, tm), BF16)
    for kind, v in enumerate((mv, fv)):
        vt = v.T.astype(BF16)
        for h in range(N_HEADS):
            base = (kind * N_HEADS + h) * HEAD_ROWS
            vt_ref[0, 0, base:base + HEAD_DIM, :] = vt[h * HEAD_DIM:(h + 1) * HEAD_DIM]
            vt_ref[0, 0, base + HEAD_DIM:base + HEAD_ROWS, :] = ones

    rows = lax.broadcasted_iota(jnp.int32, km_ref.shape, 0)
    cols = lax.broadcasted_iota(jnp.int32, km_ref.shape, 1)
    same_head = (rows >> MAX_BLOCKS_SHIFT) == (cols >> HEAD_SHIFT)
    km = km_ref[...]
    for r in range(BLOCKS_PER_CHUNK):
        kmean = jnp.mean(mk[r * BLK:(r + 1) * BLK], axis=0, keepdims=True)
        n = st * BLOCKS_PER_CHUNK + r
        km = jnp.where(same_head & ((rows & (MAX_BLOCKS - 1)) == n), kmean, km)
    km_ref[...] = km

    km_hi, km_lo = _split_bf16(km, 2)
    q_hi, q_lo = _split_bf16(mqt, 2)
    gate = (jnp.dot(km_hi, q_hi, preferred_element_type=F32)
            + jnp.dot(km_hi, q_lo, preferred_element_type=F32)
            + jnp.dot(km_lo, q_hi, preferred_element_type=F32))

    nidx = lax.broadcasted_iota(jnp.int32, (MAX_BLOCKS, tm), 0)
    own = st * BLOCKS_PER_CHUNK + (lax.broadcasted_iota(jnp.int32, (MAX_BLOCKS, tm), 1) >> BLK_SHIFT)
    valid = nidx < own
    for h in range(N_HEADS):
        gh = jnp.where(valid, gate[h * MAX_BLOCKS:(h + 1) * MAX_BLOCKS], -jnp.inf)
        rank = jnp.zeros((MAX_BLOCKS, tm), jnp.int32)
        for m in range(MAX_BLOCKS):
            gm = gh[m:m + 1, :]
            gt = (gm > gh).astype(jnp.int32)
            ge = (gm >= gh).astype(jnp.int32)
            rank = rank + jnp.where(nidx > m, ge, gt)
        keep = ((rank < TOPK) & valid) | (nidx == own)
        sel_ref[0, h * MAX_BLOCKS:(h + 1) * MAX_BLOCKS, :] = jnp.where(keep, 0.0, NEG)

    fl = tail[:, 0:LANES] + bf_ref[...]
    lf = jnp.minimum(fl, 0.0) - jnp.log1p(jnp.exp(-jnp.abs(fl)))
    lf_terms = jnp.concatenate(_split_bf16(lf, N_SPLIT), axis=1)
    tri = tri_ref[...]
    carry = cum_ref[...]
    groups = []
    for r in range(BLOCKS_PER_CHUNK):
        part = jnp.dot(tri, lf_terms[r * BLK:(r + 1) * BLK], preferred_element_type=F32)
        grp = carry + sum(part[:, t * LANES:(t + 1) * LANES] for t in range(N_SPLIT))
        carry = grp[BLK - 1:BLK, :]
        groups.append(grp)
    cum = jnp.concatenate(groups, axis=0)
    cum_ref[...] = carry
    neg_terms = jnp.concatenate(_split_bf16(cum * -LOG2E, N_SPLIT), axis=1)
    kb = jnp.dot(neg_terms, perm_ref[...], preferred_element_type=F32).astype(BF16)
    fkb = fk.astype(BF16)
    for p in range(N_PAIRS):
        fkp_ref[0, :, 2 * p * PAIR_W:(2 * p + 1) * PAIR_W] = fkb[:, p * PAIR_W:(p + 1) * PAIR_W]
        fkp_ref[0, :, (2 * p + 1) * PAIR_W:(2 * p + 2) * PAIR_W] = kb[:, p * PAIR_W:(p + 1) * PAIR_W]

    cv = tail[:, LANES:]
    cvb = cv[:, 0:CONV_CH]
    u = cv[:, CONV_CH:2 * CONV_CH] * cv[:, 2 * CONV_CH:3 * CONV_CH]
    uc = uc_ref[...]
    prev1 = uc[SUBLANES - 1:SUBLANES]
    prev2 = uc[SUBLANES - 2:SUBLANES - 1]
    ridx = lax.broadcasted_iota(jnp.int32, u.shape, 0)
    u1 = jnp.where(ridx == 0, prev1, pltpu.roll(u, 1, 0))
    u2 = jnp.where(ridx == 0, prev2, jnp.where(ridx == 1, prev1, pltpu.roll(u, 2, 0)))
    cw = cw_ref[...]
    y = cw[0:1] * u2 + cw[1:2] * u1 + cw[2:3] * u
    conv_ref[0] = (cvb * y).astype(BF16)
    uc_ref[...] = u[tm - SUBLANES:tm]


def _inproj(x, g, wqkv, wtail, bfp, cw, tri, perm):
    B, S, D = x.shape
    tm = CHUNK
    const = lambda shape: pl.BlockSpec(shape, lambda b, s: (0,) * len(shape))
    row = lambda w: pl.BlockSpec((1, tm, w), lambda b, s: (b, s, 0))
    colmajor = lambda w: pl.BlockSpec((1, 1, w, tm), lambda b, s: (b, s, 0, 0))
    return pl.pallas_call(
        _inproj_kernel,
        grid=(B, S // tm),
        in_specs=[row(D), const((1, D)), const(wqkv.shape), const(wtail.shape),
                  const(bfp.shape), const(cw.shape), const(tri.shape), const(perm.shape)],
        out_specs=[colmajor(ATT_W), row(ATT_W), colmajor(ATT_W), row(2 * ATT_W),
                   pl.BlockSpec((1, 1, 2 * N_HEADS * HEAD_ROWS, tm), lambda b, s: (b, s, 0, 0)),
                   pl.BlockSpec((1, N_HEADS * MAX_BLOCKS, tm), lambda b, s: (b, 0, s)),
                   row(CONV_CH)],
        out_shape=[jax.ShapeDtypeStruct((B, S // tm, ATT_W, tm), BF16),
                   jax.ShapeDtypeStruct((B, S, ATT_W), BF16),
                   jax.ShapeDtypeStruct((B, S // tm, ATT_W, tm), BF16),
                   jax.ShapeDtypeStruct((B, S, 2 * ATT_W), BF16),
                   jax.ShapeDtypeStruct((B, S // tm, 2 * N_HEADS * HEAD_ROWS, tm), BF16),
                   jax.ShapeDtypeStruct((B, N_HEADS * MAX_BLOCKS, S), F32),
                   jax.ShapeDtypeStruct((B, S, CONV_CH), BF16)],
        scratch_shapes=[pltpu.VMEM((N_HEADS * MAX_BLOCKS, ATT_W), F32),
                        pltpu.VMEM((1, LANES), F32),
                        pltpu.VMEM((SUBLANES, CONV_CH), F32)],
        compiler_params=pltpu.CompilerParams(
            dimension_semantics=("arbitrary", "arbitrary"), vmem_limit_bytes=VMEM_LIMIT),
        name="inproj",
    )(x, g, wqkv, wtail, bfp, cw, tri, perm)


def _attn_kernel(*refs, moba, nchunk):
    if moba:
        rb_ref, qt_ref, k_ref, vt_ref, sel_ref, own_ref, corner_ref, ind_ref, o_ref, *bufs = refs
        pair = pl.program_id(1)
        xrow = lax.broadcasted_iota(jnp.int32, (XROWS, CHUNK), 0)
        far_rows, near_fix = [], []
        for hh in range(2):
            fb = rb_ref[REL_BUCKETS - 1, 2 * pair + hh] * LOG2E
            terms = _split_bf16(jnp.full((XROWS, CHUNK), fb, F32), N_SPLIT)
            rows = jnp.zeros((XROWS, CHUNK), F32)
            for t in range(N_SPLIT):
                rows = jnp.where(xrow == BLOCKS_PER_CHUNK + t, terms[t].astype(F32), rows)
            far_rows.append(rows)
            near_fix.append(corner_ref[hh] - fb)
        xpad = jnp.zeros((PAIR_W - XROWS, CHUNK), BF16)
    else:
        qt_ref, k_ref, vt_ref, o_ref, *bufs = refs
    hidx = [jnp.minimum(pl.program_id(0), 0) + hh for hh in range(2)]

    def weights(qi):
        qt = qt_ref[0, qi]
        frow = lax.broadcasted_iota(jnp.int32, qt.shape, 0)
        zero = jnp.zeros_like(qt)
        ws = []
        for hh in range(2):
            w = jnp.where((frow >= hh * HEAD_DIM) & (frow < (hh + 1) * HEAD_DIM), qt, zero)
            if not moba:
                pick = (frow >= N_SPLIT * hh) & (frow < N_SPLIT * (hh + 1))
                w = jnp.concatenate([w, jnp.where(pick, 1.0, 0.0).astype(BF16)], axis=0)
            ws.append(w)
        return ws

    def query_norms(qi):
        q2 = qt_ref[0, qi].astype(F32) ** 2
        return [jnp.sqrt(jnp.sum(q2[hh * HEAD_DIM:(hh + 1) * HEAD_DIM], axis=0, keepdims=True))
                for hh in range(2)]

    def produce(qi, ws, c, dst_ref):
        own, near = c == qi, c == qi - 1
        kc = k_ref[0, c * CHUNK:(c + 1) * CHUNK, :]
        if moba:
            kc = jnp.concatenate([kc, ind_ref[...]], axis=1)
        maxima = []
        for hh in range(2):
            w = ws[hh]
            if moba:
                rows = jnp.zeros((XROWS, CHUNK), F32) if own else far_rows[hh]
                for r in range(BLOCKS_PER_CHUNK):
                    n = hh * MAX_BLOCKS + c * BLOCKS_PER_CHUNK + r
                    sel = sel_ref[0, n:n + 1, qi * CHUNK:(qi + 1) * CHUNK]
                    rows = jnp.where(xrow == r, sel, rows)
                w = jnp.concatenate([w, rows.astype(BF16), xpad], axis=0)
            if own:
                top = jnp.dot(kc[:BLK], w, preferred_element_type=F32)
                bot = jnp.dot(kc[BLK:], w[:, BLK:], preferred_element_type=F32)
                if moba:
                    top = top + own_ref[hh, :BLK, :]
                    bot = bot + own_ref[hh, BLK:, BLK:]
                else:
                    kidx = lax.broadcasted_iota(jnp.int32, (BLK, BLK), 0)
                    qidx = lax.broadcasted_iota(jnp.int32, (BLK, BLK), 1)
                    causal = kidx <= qidx
                    top = jnp.concatenate([jnp.where(causal, top[:, :BLK], NEG), top[:, BLK:]], axis=1)
                    bot = jnp.where(causal, bot, NEG)
                dst_ref[hidx[hh], :BLK, :] = top
                dst_ref[hidx[hh], BLK:, BLK:] = bot
                mt = jnp.max(top, axis=0, keepdims=True)
                mb = jnp.max(bot, axis=0, keepdims=True)
                maxima.append(jnp.concatenate([mt[:, :BLK], jnp.maximum(mt[:, BLK:], mb)], axis=1))
                continue
            s = jnp.dot(kc, w, preferred_element_type=F32)
            if moba and near:
                lo = CHUNK - CORNER
                fixed = jnp.concatenate([s[lo:, :CORNER] + near_fix[hh], s[lo:, CORNER:]], axis=1)
                s = jnp.concatenate([s[:lo], fixed], axis=0)
            dst_ref[hidx[hh]] = s
            maxima.append(jnp.max(s, axis=0, keepdims=True))
        return tuple(maxima)

    def consume(c, src_ref, maxima, state, own=False):
        new = []
        for hh in range(2):
            m_old, acc = state[hh]
            m_new = jnp.maximum(m_old, maxima[hh])
            vth = vt_ref[0, c, hh * HEAD_ROWS:(hh + 1) * HEAD_ROWS, :]
            if own:
                p_top = jnp.exp2(src_ref[hidx[hh], :BLK, :] - m_new).astype(BF16)
                p_bot = jnp.exp2(src_ref[hidx[hh], BLK:, BLK:] - m_new[:, BLK:]).astype(BF16)
                pv_top = jnp.dot(vth[:, :BLK], p_top, preferred_element_type=F32)
                pv_bot = jnp.dot(vth[:, BLK:], p_bot, preferred_element_type=F32)
                pv = jnp.concatenate([pv_top[:, :BLK], pv_top[:, BLK:] + pv_bot], axis=1)
            else:
                p = jnp.exp2(src_ref[hidx[hh]] - m_new).astype(BF16)
                pv = jnp.dot(vth, p, preferred_element_type=F32)
            new.append((m_new, jnp.exp2(m_old - m_new) * acc + pv))
        return tuple(new)

    def fresh_state():
        return tuple((jnp.full((1, CHUNK), NEG, F32), jnp.zeros((HEAD_ROWS, CHUNK), F32))
                     for _ in range(2))

    def finish(qi, state):
        ot = jnp.concatenate([acc[:HEAD_DIM] * (1.0 / acc[HEAD_DIM:HEAD_DIM + 1])
                              for (_, acc) in state], axis=0)
        o_ref[0, qi * CHUNK:(qi + 1) * CHUNK, :] = ot.T.astype(BF16)

    def dense_tile(qi):
        ws = weights(qi)
        maxima = produce(qi, ws, qi, bufs[0])
        state = fresh_state()
        for g, c in enumerate(range(qi, -1, -1)):
            nxt = produce(qi, ws, c - 1, bufs[(g + 1) % 2]) if c > 0 else None
            state = consume(c, bufs[g % 2], maxima, state, own=(c == qi))
            maxima = nxt
        finish(qi, state)

    def dense_program():
        g = 0
        ws = weights(0)
        maxima = produce(0, ws, 0, bufs[0])
        for qi in range(nchunk):
            state = fresh_state()
            ws_next = weights(qi + 1) if qi + 1 < nchunk else None
            for c in range(qi, -1, -1):
                nxt = None
                if c > 0:
                    nxt = produce(qi, ws, c - 1, bufs[(g + 1) % 2])
                elif ws_next is not None:
                    nxt = produce(qi + 1, ws_next, qi + 1, bufs[(g + 1) % 2])
                state = consume(c, bufs[g % 2], maxima, state, own=(c == qi))
                maxima = nxt
                g += 1
            ws = ws_next
            finish(qi, state)

    if moba:
        dense_program()
        return

    own_bufs, far_bufs = bufs[0:2], bufs[2:4]
    kall = k_ref[0][:, 0:PAIR_W].astype(F32) ** 2
    lane = lax.broadcasted_iota(jnp.int32, (1, PAIR_W), 1)
    key_norm = [jnp.sqrt(jnp.max(jnp.sum(jnp.where((lane >= hh * HEAD_DIM) & (lane < (hh + 1) * HEAD_DIM),
                                                      kall, 0.0), axis=1, keepdims=True)))
                for hh in range(2)]

    def gap(qn, decay_rows, hh, m, newer_queries_only):
        mine = (lane >= N_SPLIT * hh) & (lane < N_SPLIT * (hh + 1))
        newest = jnp.max(jnp.sum(jnp.where(mine, decay_rows, 0.0), axis=1, keepdims=True))
        bound = qn[hh] * key_norm[hh] + newest - m
        if newer_queries_only:
            qlane = lax.broadcasted_iota(jnp.int32, bound.shape, 1)
            bound = jnp.where(qlane >= BLK, bound, -jnp.inf)
        return jnp.max(bound)

    def skipped_gap(qi, state):
        base = (qi - 2) * CHUNK
        older_blk = k_ref[0, base + BLK - BF16_TILE_ROWS:base + BLK, PAIR_W:2 * PAIR_W].astype(F32)
        newer_blk = k_ref[0, base + CHUNK - BF16_TILE_ROWS:base + CHUNK, PAIR_W:2 * PAIR_W].astype(F32)
        qn = query_norms(qi)
        worst = None
        for hh in range(2):
            g = jnp.maximum(gap(qn, older_blk, hh, state[hh][0], False),
                            gap(qn, newer_blk, hh, state[hh][0], True))
            worst = g if worst is None else jnp.maximum(worst, g)
        return worst

    band_buf = far_bufs[1]

    def produce_band(qi, ws):
        c = qi - 2
        kq = k_ref[0, c * CHUNK + BLK:(c + 1) * CHUNK, :]
        maxima = []
        for hh in range(2):
            s = jnp.dot(kq, ws[hh][:, :BLK], preferred_element_type=F32)
            band_buf[hidx[hh], :BLK, :BLK] = s
            maxima.append(jnp.max(s, axis=0, keepdims=True))
        return tuple(maxima)

    def consume_band(qi, maxima, state):
        c = qi - 2
        new = []
        for hh in range(2):
            m_old, acc = state[hh]
            m_left = jnp.maximum(m_old[:, :BLK], maxima[hh])
            p = jnp.exp2(band_buf[hidx[hh], :BLK, :BLK] - m_left).astype(BF16)
            vth = vt_ref[0, c, hh * HEAD_ROWS:(hh + 1) * HEAD_ROWS, BLK:]
            left = jnp.exp2(m_old[:, :BLK] - m_left) * acc[:, :BLK] + jnp.dot(vth, p, preferred_element_type=F32)
            new.append((jnp.concatenate([m_left, m_old[:, BLK:]], axis=1),
                        jnp.concatenate([left, acc[:, BLK:]], axis=1)))
        return tuple(new)

    ws = weights(0)
    maxima = produce(0, ws, 0, own_bufs[0])
    gaps = {}
    for qi in range(nchunk):
        state = fresh_state()
        ws_next = weights(qi + 1) if qi + 1 < nchunk else None
        own_src, next_dst = own_bufs[qi % 2], own_bufs[(qi + 1) % 2]
        chunks = list(range(qi, max(qi - NEAR_CHUNKS, -1), -1))
        band = qi >= NEAR_CHUNKS
        src = own_src
        for j, c in enumerate(chunks):
            last = j == len(chunks) - 1
            if not last:
                dst = far_bufs[j % 2]
                nxt = produce(qi, ws, c - 1, dst)
            elif band:
                band_mx = produce_band(qi, ws)
            elif ws_next is not None:
                next_own = produce(qi + 1, ws_next, qi + 1, next_dst)
            state = consume(c, src, maxima, state, own=(j == 0))
            if not last:
                src, maxima = dst, nxt
        if band:
            gaps[qi] = skipped_gap(qi, state)
            if ws_next is not None:
                next_own = produce(qi + 1, ws_next, qi + 1, next_dst)
            state = consume_band(qi, band_mx, state)
        finish(qi, state)
        ws, maxima = ws_next, next_own

    for qi, worst in gaps.items():
        pl.when(worst >= -UNDERFLOW_BITS)(functools.partial(dense_tile, qi))


def _block_indicator():
    ind = np.zeros((CHUNK, PAIR_W), np.float32)
    for r in range(BLOCKS_PER_CHUNK):
        ind[r * BLK:(r + 1) * BLK, r] = 1.0
    ind[:, BLOCKS_PER_CHUNK:BLOCKS_PER_CHUNK + N_SPLIT] = 1.0
    return jnp.asarray(ind, BF16)


def _attention(qt, k, vt, sel=None, tables=None, rel_bias=None):
    moba = sel is not None
    B, S, _ = k.shape
    nchunk = S // CHUNK
    kw = k.shape[-1] // N_PAIRS
    vt_off = 0 if moba else N_PAIRS
    in_specs = [pl.BlockSpec((1, nchunk, PAIR_W, CHUNK), lambda b, p: (b, 0, p, 0)),
                pl.BlockSpec((1, S, kw), lambda b, p: (b, 0, p)),
                pl.BlockSpec((1, nchunk, PAIR_ROWS, CHUNK), lambda b, p: (b, 0, p + vt_off, 0))]
    args = [qt, k, vt]
    if moba:
        in_specs = [pl.BlockSpec(memory_space=pltpu.SMEM)] + in_specs
        in_specs += [pl.BlockSpec((1, 2 * MAX_BLOCKS, S), lambda b, p: (b, p, 0)),
                     pl.BlockSpec((2, CHUNK, CHUNK), lambda b, p: (p, 0, 0)),
                     pl.BlockSpec((2, CORNER, CORNER), lambda b, p: (p, 0, 0)),
                     pl.BlockSpec((CHUNK, PAIR_W), lambda b, p: (0, 0))]
        args = [rel_bias] + args + [sel, *tables, _block_indicator()]
    return pl.pallas_call(
        functools.partial(_attn_kernel, moba=moba, nchunk=nchunk),
        grid=(B, N_PAIRS),
        in_specs=in_specs,
        out_specs=pl.BlockSpec((1, S, PAIR_W), lambda b, p: (b, 0, p)),
        out_shape=jax.ShapeDtypeStruct((B, S, ATT_W), BF16),
        scratch_shapes=[pltpu.VMEM((2, CHUNK, CHUNK), F32)] * (2 if moba else 4),
        compiler_params=pltpu.CompilerParams(
            dimension_semantics=("arbitrary", "arbitrary"), vmem_limit_bytes=VMEM_LIMIT),
        name="moba_attn" if moba else "fox_attn",
    )(*args)


def _out_ffn_kernel(moba_ref, fox_ref, conv_ref, x_ref, wo_ref, wg_ref, wu_ref, wd_ref,
                    gpm_ref, gpf_ref, gqf_ref, o_ref):
    mix_in = jnp.concatenate([moba_ref[...], fox_ref[...], conv_ref[...]], axis=1)
    mixed = jnp.dot(mix_in, wo_ref[...], preferred_element_type=F32)
    x1 = x_ref[...] + _rms(mixed, gpm_ref[...])
    h2 = _rms(x1, gpf_ref[...]).astype(BF16)
    ff = jnp.zeros(x1.shape, F32)
    lo = 0
    for width in FFN_CHUNKS:
        gate = jnp.dot(h2, wg_ref[:, lo:lo + width], preferred_element_type=F32)
        up = jnp.dot(h2, wu_ref[:, lo:lo + width], preferred_element_type=F32)
        act = (gate * (1.0 / (1.0 + jnp.exp(-gate))) * up).astype(BF16)
        ff = ff + jnp.dot(act, wd_ref[lo:lo + width, :], preferred_element_type=F32)
        lo += width
    o_ref[...] = x1 + _rms(ff, gqf_ref[...])


def _out_ffn(moba, fox, conv, x, wo, wg, wu, wd, gpm, gpf, gqf):
    M, D = x.shape
    tm = FFN_TILE
    row = lambda w: pl.BlockSpec((tm, w), lambda m: (m, 0))
    const = lambda shape: pl.BlockSpec(shape, lambda m: (0, 0), pipeline_mode=pl.Buffered(1))
    return pl.pallas_call(
        _out_ffn_kernel,
        grid=(M // tm,),
        in_specs=[row(ATT_W), row(ATT_W), row(CONV_CH), row(D),
                  const(wo.shape), const(wg.shape), const(wu.shape), const(wd.shape),
                  const((1, D)), const((1, D)), const((1, D))],
        out_specs=row(D),
        out_shape=jax.ShapeDtypeStruct((M, D), F32),
        compiler_params=pltpu.CompilerParams(
            dimension_semantics=("arbitrary",), vmem_limit_bytes=VMEM_LIMIT),
        name="out_ffn",
    )(moba, fox, conv, x, wo, wg, wu, wd, gpm, gpf, gqf)


def _cumsum_constants():
    tri = np.tril(np.ones((BLK, BLK), np.float32))
    perm = np.zeros((N_SPLIT * LANES, ATT_W), np.float32)
    for s in range(N_SPLIT):
        for h in range(N_HEADS):
            perm[s * LANES + h, PAIR_W * (h // 2) + N_SPLIT * (h % 2) + s] = 1.0
    return jnp.asarray(tri, BF16), jnp.asarray(perm, BF16)


def kernel(x, w_in, b_f, conv_w, w_out, rel_bias, g_pre_mix, g_post_mix, g_pre_ffn, g_post_ffn,
           w_gate, w_up, w_down):
    B, S, D = x.shape
    depth = w_in.shape[0]
    assert D == D_MODEL and S % CHUNK == 0 and S // BLK <= MAX_BLOCKS
    assert (B * S) % FFN_TILE == 0

    rel_bias = rel_bias.astype(F32)
    tables = _bias_tiles(rel_bias)
    w_in_t = jnp.swapaxes(w_in, 1, 2)
    tri, perm = _cumsum_constants()
    a = ATT_W
    for l in range(depth):
        wl = w_in_t[l]
        wqkv = wl[0:6 * a].astype(BF16)
        wtail = jnp.concatenate([wl[6 * a:6 * a + N_HEADS], jnp.zeros((LANES - N_HEADS, D), wl.dtype),
                                 wl[6 * a + N_HEADS:]], axis=0).astype(BF16)
        bfp = jnp.pad(b_f[l].astype(F32), (0, LANES - N_HEADS)).reshape(1, LANES)
        row = lambda g: g[l].astype(F32).reshape(1, D)

        mq, mk, fq, fkp, vt, sel, conv = _inproj(
            x, row(g_pre_mix), wqkv, wtail, bfp, conv_w[l].astype(F32), tri, perm)
        moba = _attention(mq, mk, vt, sel, tables, rel_bias)
        fox = _attention(fq, fkp, vt)
        x = _out_ffn(moba.reshape(B * S, a), fox.reshape(B * S, a), conv.reshape(B * S, CONV_CH),
                     x.reshape(B * S, D), w_out[l].astype(BF16), w_gate[l].astype(BF16),
                     w_up[l].astype(BF16), w_down[l].astype(BF16),
                     row(g_post_mix), row(g_pre_ffn), row(g_post_ffn)).reshape(B, S, D)
    return x
```

```python
import functools
import math

import jax
import jax.numpy as jnp
import numpy as np
from jax import lax
from jax.experimental import pallas as pl
from jax.experimental.pallas import tpu as pltpu

D_MODEL = 1024
HEAD_DIM = 64
N_HEADS = 6
ATT_W = N_HEADS * HEAD_DIM
PAIR_W = 2 * HEAD_DIM
N_PAIRS = N_HEADS // 2
CONV_CH = 256
CONV_WIDTH = 3
BLK = 256
BLK_SHIFT = BLK.bit_length() - 1
CHUNK = 512
BLOCKS_PER_CHUNK = CHUNK // BLK
assert BLOCKS_PER_CHUNK == 2
MAX_BLOCKS = 16
MAX_BLOCKS_SHIFT = MAX_BLOCKS.bit_length() - 1
HEAD_SHIFT = HEAD_DIM.bit_length() - 1
BF16_TILE_ROWS = 16
TOPK = 3
REL_BUCKETS = 32
REL_MAX_DIST = 128
D_FF = 2816
RMS_EPS = 1e-6
NEG = -1e30
LANES = 128
SUBLANES = 8
N_SPLIT = 3
LOG2E = math.log2(math.e)
CORNER = REL_MAX_DIST
ONES_ROWS = BF16_TILE_ROWS
HEAD_ROWS = HEAD_DIM + ONES_ROWS
PAIR_ROWS = 2 * HEAD_ROWS
XROWS = BF16_TILE_ROWS
NEAR_CHUNKS = 2
UNDERFLOW_BITS = 152.0

F32 = jnp.float32
BF16 = jnp.bfloat16
NT_DIMS = (((1,), (1,)), ((), ()))

FFN_TILE = 512
FFN_CHUNKS = (1536, 1280)
assert sum(FFN_CHUNKS) == D_FF
VMEM_LIMIT = 50 * 1024 * 1024


def _split_bf16(v, n):
    terms = []
    for _ in range(n):
        t = v.astype(BF16)
        terms.append(t)
        v = v - t.astype(F32)
    return terms


def _rms(v, g):
    return v * lax.rsqrt(jnp.mean(v * v, axis=-1, keepdims=True) + RMS_EPS) * g


def _bucket_tiles():
    max_exact = REL_BUCKETS // 2

    def bucket(dist):
        n = np.maximum(dist, 0)
        nf = np.maximum(n, 1).astype(np.float32)
        large = max_exact + (np.log(nf / np.float32(max_exact))
                             / np.float32(math.log(REL_MAX_DIST / max_exact))
                             * np.float32(REL_BUCKETS - max_exact)).astype(np.int32)
        large = np.minimum(large, REL_BUCKETS - 1)
        return np.where(n < max_exact, n, large).astype(np.int32)

    key = np.arange(CHUNK, dtype=np.int32)[:, None]
    qry = np.arange(CHUNK, dtype=np.int32)[None, :]
    d0 = qry - key
    own = np.where(d0 >= 0, bucket(d0), -1).astype(np.int32)
    prev = bucket(d0 + CHUNK)
    corner = prev[CHUNK - CORNER:, :CORNER]
    outside = prev.copy()
    outside[CHUNK - CORNER:, :CORNER] = REL_BUCKETS - 1
    assert (outside == REL_BUCKETS - 1).all()
    return own, np.ascontiguousarray(corner)


def _bias_tile_kernel(rb_ref, own_bm_ref, corner_bm_ref, own_ref, corner_ref):
    h = pl.program_id(0)
    for bm_ref, out_ref in ((own_bm_ref, own_ref), (corner_bm_ref, corner_ref)):
        bm = bm_ref[...]
        t = jnp.full(bm.shape, NEG, F32)
        for b in range(REL_BUCKETS):
            t = jnp.where(bm == b, rb_ref[b, h] * LOG2E, t)
        out_ref[0] = t


def _bias_tiles(rel_bias):
    own_bm, corner_bm = _bucket_tiles()
    return pl.pallas_call(
        _bias_tile_kernel,
        grid=(N_HEADS,),
        in_specs=[pl.BlockSpec(memory_space=pltpu.SMEM),
                  pl.BlockSpec((CHUNK, CHUNK), lambda h: (0, 0)),
                  pl.BlockSpec((CORNER, CORNER), lambda h: (0, 0))],
        out_specs=[pl.BlockSpec((1, CHUNK, CHUNK), lambda h: (h, 0, 0)),
                   pl.BlockSpec((1, CORNER, CORNER), lambda h: (h, 0, 0))],
        out_shape=[jax.ShapeDtypeStruct((N_HEADS, CHUNK, CHUNK), F32),
                   jax.ShapeDtypeStruct((N_HEADS, CORNER, CORNER), F32)],
        name="bias_tiles",
    )(rel_bias, jnp.asarray(own_bm), jnp.asarray(corner_bm))


def _inproj_kernel(x_ref, g_ref, wqkv_ref, wtail_ref, bf_ref, cw_ref, tri_ref, perm_ref,
                   mq_ref, mk_ref, fq_ref, fkp_ref, vt_ref, sel_ref, conv_ref,
                   km_ref, cum_ref, uc_ref):
    tm = CHUNK
    st = pl.program_id(1)

    @pl.when(st == 0)
    def _():
        km_ref[...] = jnp.zeros_like(km_ref)
        cum_ref[...] = jnp.zeros_like(cum_ref)
        uc_ref[...] = jnp.zeros_like(uc_ref)

    hn = _rms(x_ref[0], g_ref[...]).astype(BF16)

    qkv = lax.dot_general(hn, wqkv_ref[...], NT_DIMS, preferred_element_type=F32)
    tail = lax.dot_general(hn, wtail_ref[...], NT_DIMS, preferred_element_type=F32)
    mq = qkv[:, 0 * ATT_W:1 * ATT_W]
    mk = qkv[:, 1 * ATT_W:2 * ATT_W]
    mv = qkv[:, 2 * ATT_W:3 * ATT_W]
    fq = qkv[:, 3 * ATT_W:4 * ATT_W]
    fk = qkv[:, 4 * ATT_W:5 * ATT_W]
    fv = qkv[:, 5 * ATT_W:6 * ATT_W]
    scale = HEAD_DIM ** -0.5 * LOG2E
    mqt = mq.T
    mq_ref[0, 0] = (mqt * scale).astype(BF16)
    mk_ref[0] = mk.astype(BF16)
    fq_ref[0, 0] = (fq * scale).T.astype(BF16)
    ones = jnp.ones((ONES_ROWS, tm), BF16)
    for kind, v in enumerate((mv, fv)):
        vt = v.T.astype(BF16)
        for h in range(N_HEADS):
            base = (kind * N_HEADS + h) * HEAD_ROWS
            vt_ref[0, 0, base:base + HEAD_DIM, :] = vt[h * HEAD_DIM:(h + 1) * HEAD_DIM]
            vt_ref[0, 0, base + HEAD_DIM:base + HEAD_ROWS, :] = ones

    rows = lax.broadcasted_iota(jnp.int32, km_ref.shape, 0)
    cols = lax.broadcasted_iota(jnp.int32, km_ref.shape, 1)
    same_head = (rows >> MAX_BLOCKS_SHIFT) == (cols >> HEAD_SHIFT)
    km = km_ref[...]
    for r in range(BLOCKS_PER_CHUNK):
        kmean = jnp.mean(mk[r * BLK:(r + 1) * BLK], axis=0, keepdims=True)
        n = st * BLOCKS_PER_CHUNK + r
        km = jnp.where(same_head & ((rows & (MAX_BLOCKS - 1)) == n), kmean, km)
    km_ref[...] = km

    km_hi, km_lo = _split_bf16(km, 2)
    q_hi, q_lo = _split_bf16(mqt, 2)
    gate = (jnp.dot(km_hi, q_hi, preferred_element_type=F32)
            + jnp.dot(km_hi, q_lo, preferred_element_type=F32)
            + jnp.dot(km_lo, q_hi, preferred_element_type=F32))

    nidx = lax.broadcasted_iota(jnp.int32, (MAX_BLOCKS, tm), 0)
    own = st * BLOCKS_PER_CHUNK + (lax.broadcasted_iota(jnp.int32, (MAX_BLOCKS, tm), 1) >> BLK_SHIFT)
    valid = nidx < own
    for h in range(N_HEADS):
        gh = jnp.where(valid, gate[h * MAX_BLOCKS:(h + 1) * MAX_BLOCKS], -jnp.inf)
        rank = jnp.zeros((MAX_BLOCKS, tm), jnp.int32)
        for m in range(MAX_BLOCKS):
            gm = gh[m:m + 1, :]
            gt = (gm > gh).astype(jnp.int32)
            ge = (gm >= gh).astype(jnp.int32)
            rank = rank + jnp.where(nidx > m, ge, gt)
        keep = ((rank < TOPK) & valid) | (nidx == own)
        sel_ref[0, h * MAX_BLOCKS:(h + 1) * MAX_BLOCKS, :] = jnp.where(keep, 0.0, NEG)

    fl = tail[:, 0:LANES] + bf_ref[...]
    lf = jnp.minimum(fl, 0.0) - jnp.log1p(jnp.exp(-jnp.abs(fl)))
    lf_terms = jnp.concatenate(_split_bf16(lf, N_SPLIT), axis=1)
    tri = tri_ref[...]
    carry = cum_ref[...]
    groups = []
    for r in range(BLOCKS_PER_CHUNK):
        part = jnp.dot(tri, lf_terms[r * BLK:(r + 1) * BLK], preferred_element_type=F32)
        grp = carry + sum(part[:, t * LANES:(t + 1) * LANES] for t in range(N_SPLIT))
        carry = grp[BLK - 1:BLK, :]
        groups.append(grp)
    cum = jnp.concatenate(groups, axis=0)
    cum_ref[...] = carry
    neg_terms = jnp.concatenate(_split_bf16(cum * -LOG2E, N_SPLIT), axis=1)
    kb = jnp.dot(neg_terms, perm_ref[...], preferred_element_type=F32).astype(BF16)
    fkb = fk.astype(BF16)
    for p in range(N_PAIRS):
        fkp_ref[0, :, 2 * p * PAIR_W:(2 * p + 1) * PAIR_W] = fkb[:, p * PAIR_W:(p + 1) * PAIR_W]
        fkp_ref[0, :, (2 * p + 1) * PAIR_W:(2 * p + 2) * PAIR_W] = kb[:, p * PAIR_W:(p + 1) * PAIR_W]

    cv = tail[:, LANES:]
    cvb = cv[:, 0:CONV_CH]
    u = cv[:, CONV_CH:2 * CONV_CH] * cv[:, 2 * CONV_CH:3 * CONV_CH]
    uc = uc_ref[...]
    prev1 = uc[SUBLANES - 1:SUBLANES]
    prev2 = uc[SUBLANES - 2:SUBLANES - 1]
    ridx = lax.broadcasted_iota(jnp.int32, u.shape, 0)
    u1 = jnp.where(ridx == 0, prev1, pltpu.roll(u, 1, 0))
    u2 = jnp.where(ridx == 0, prev2, jnp.where(ridx == 1, prev1, pltpu.roll(u, 2, 0)))
    cw = cw_ref[...]
    y = cw[0:1] * u2 + cw[1:2] * u1 + cw[2:3] * u
    conv_ref[0] = (cvb * y).astype(BF16)
    uc_ref[...] = u[tm - SUBLANES:tm]


def _inproj(x, g, wqkv, wtail, bfp, cw, tri, perm):
    B, S, D = x.shape
    tm = CHUNK
    const = lambda shape: pl.BlockSpec(shape, lambda b, s: (0,) * len(shape))
    row = lambda w: pl.BlockSpec((1, tm, w), lambda b, s: (b, s, 0))
    colmajor = lambda w: pl.BlockSpec((1, 1, w, tm), lambda b, s: (b, s, 0, 0))
    return pl.pallas_call(
        _inproj_kernel,
        grid=(B, S // tm),
        in_specs=[row(D), const((1, D)), const(wqkv.shape), const(wtail.shape),
                  const(bfp.shape), const(cw.shape), const(tri.shape), const(perm.shape)],
        out_specs=[colmajor(ATT_W), row(ATT_W), colmajor(ATT_W), row(2 * ATT_W),
                   pl.BlockSpec((1, 1, 2 * N_HEADS * HEAD_ROWS, tm), lambda b, s: (b, s, 0, 0)),
                   pl.BlockSpec((1, N_HEADS * MAX_BLOCKS, tm), lambda b, s: (b, 0, s)),
                   row(CONV_CH)],
        out_shape=[jax.ShapeDtypeStruct((B, S // tm, ATT_W, tm), BF16),
                   jax.ShapeDtypeStruct((B, S, ATT_W), BF16),
                   jax.ShapeDtypeStruct((B, S // tm, ATT_W, tm), BF16),
                   jax.ShapeDtypeStruct((B, S, 2 * ATT_W), BF16),
                   jax.ShapeDtypeStruct((B, S // tm, 2 * N_HEADS * HEAD_ROWS, tm), BF16),
                   jax.ShapeDtypeStruct((B, N_HEADS * MAX_BLOCKS, S), F32),
                   jax.ShapeDtypeStruct((B, S, CONV_CH), BF16)],
        scratch_shapes=[pltpu.VMEM((N_HEADS * MAX_BLOCKS, ATT_W), F32),
                        pltpu.VMEM((1, LANES), F32),
                        pltpu.VMEM((SUBLANES, CONV_CH), F32)],
        compiler_params=pltpu.CompilerParams(
            dimension_semantics=("arbitrary", "arbitrary"), vmem_limit_bytes=VMEM_LIMIT),
        name="inproj",
    )(x, g, wqkv, wtail, bfp, cw, tri, perm)


def _attn_kernel(*refs, moba, nchunk):
    if moba:
        rb_ref, qt_ref, k_ref, vt_ref, sel_ref, own_ref, corner_ref, ind_ref, o_ref, *bufs = refs
        pair = pl.program_id(1)
        xrow = lax.broadcasted_iota(jnp.int32, (XROWS, CHUNK), 0)
        far_rows, near_fix = [], []
        for hh in range(2):
            fb = rb_ref[REL_BUCKETS - 1, 2 * pair + hh] * LOG2E
            terms = _split_bf16(jnp.full((XROWS, CHUNK), fb, F32), N_SPLIT)
            rows = jnp.zeros((XROWS, CHUNK), F32)
            for t in range(N_SPLIT):
                rows = jnp.where(xrow == BLOCKS_PER_CHUNK + t, terms[t].astype(F32), rows)
            far_rows.append(rows)
            near_fix.append(corner_ref[hh] - fb)
        xpad = jnp.zeros((PAIR_W - XROWS, CHUNK), BF16)
    else:
        qt_ref, k_ref, vt_ref, o_ref, *bufs = refs
    hidx = [jnp.minimum(pl.program_id(0), 0) + hh for hh in range(2)]

    def weights(qi):
        qt = qt_ref[0, qi]
        frow = lax.broadcasted_iota(jnp.int32, qt.shape, 0)
        zero = jnp.zeros_like(qt)
        ws = []
        for hh in range(2):
            w = jnp.where((frow >= hh * HEAD_DIM) & (frow < (hh + 1) * HEAD_DIM), qt, zero)
            if not moba:
                pick = (frow >= N_SPLIT * hh) & (frow < N_SPLIT * (hh + 1))
                w = jnp.concatenate([w, jnp.where(pick, 1.0, 0.0).astype(BF16)], axis=0)
            ws.append(w)
        return ws

    def query_norms(qi):
        q2 = qt_ref[0, qi].astype(F32) ** 2
        return [jnp.sqrt(jnp.sum(q2[hh * HEAD_DIM:(hh + 1) * HEAD_DIM], axis=0, keepdims=True))
                for hh in range(2)]

    def produce(qi, ws, c, dst_ref):
        own, near = c == qi, c == qi - 1
        kc = k_ref[0, c * CHUNK:(c + 1) * CHUNK, :]
        if moba:
            kc = jnp.concatenate([kc, ind_ref[...]], axis=1)
        maxima = []
        for hh in range(2):
            w = ws[hh]
            if moba:
                rows = jnp.zeros((XROWS, CHUNK), F32) if own else far_rows[hh]
                for r in range(BLOCKS_PER_CHUNK):
                    n = hh * MAX_BLOCKS + c * BLOCKS_PER_CHUNK + r
                    sel = sel_ref[0, n:n + 1, qi * CHUNK:(qi + 1) * CHUNK]
                    rows = jnp.where(xrow == r, sel, rows)
                w = jnp.concatenate([w, rows.astype(BF16), xpad], axis=0)
            if own:
                top = jnp.dot(kc[:BLK], w, preferred_element_type=F32)
                bot = jnp.dot(kc[BLK:], w[:, BLK:], preferred_element_type=F32)
                if moba:
                    top = top + own_ref[hh, :BLK, :]
                    bot = bot + own_ref[hh, BLK:, BLK:]
                else:
                    kidx = lax.broadcasted_iota(jnp.int32, (BLK, BLK), 0)
                    qidx = lax.broadcasted_iota(jnp.int32, (BLK, BLK), 1)
                    causal = kidx <= qidx
                    top = jnp.concatenate([jnp.where(causal, top[:, :BLK], NEG), top[:, BLK:]], axis=1)
                    bot = jnp.where(causal, bot, NEG)
                dst_ref[hidx[hh], :BLK, :] = top
                dst_ref[hidx[hh], BLK:, BLK:] = bot
                mt = jnp.max(top, axis=0, keepdims=True)
                mb = jnp.max(bot, axis=0, keepdims=True)
                maxima.append(jnp.concatenate([mt[:, :BLK], jnp.maximum(mt[:, BLK:], mb)], axis=1))
                continue
            s = jnp.dot(kc, w, preferred_element_type=F32)
            if moba and near:
                lo = CHUNK - CORNER
                fixed = jnp.concatenate([s[lo:, :CORNER] + near_fix[hh], s[lo:, CORNER:]], axis=1)
                s = jnp.concatenate([s[:lo], fixed], axis=0)
            dst_ref[hidx[hh]] = s
            maxima.append(jnp.max(s, axis=0, keepdims=True))
        return tuple(maxima)

    def consume(c, src_ref, maxima, state, own=False):
        new = []
        for hh in range(2):
            m_old, acc = state[hh]
            m_new = jnp.maximum(m_old, maxima[hh])
            vth = vt_ref[0, c, hh * HEAD_ROWS:(hh + 1) * HEAD_ROWS, :]
            if own:
                p_top = jnp.exp2(src_ref[hidx[hh], :BLK, :] - m_new).astype(BF16)
                p_bot = jnp.exp2(src_ref[hidx[hh], BLK:, BLK:] - m_new[:, BLK:]).astype(BF16)
                pv_top = jnp.dot(vth[:, :BLK], p_top, preferred_element_type=F32)
                pv_bot = jnp.dot(vth[:, BLK:], p_bot, preferred_element_type=F32)
                pv = jnp.concatenate([pv_top[:, :BLK], pv_top[:, BLK:] + pv_bot], axis=1)
            else:
                p = jnp.exp2(src_ref[hidx[hh]] - m_new).astype(BF16)
                pv = jnp.dot(vth, p, preferred_element_type=F32)
            new.append((m_new, jnp.exp2(m_old - m_new) * acc + pv))
        return tuple(new)

    def fresh_state():
        return tuple((jnp.full((1, CHUNK), NEG, F32), jnp.zeros((HEAD_ROWS, CHUNK), F32))
                     for _ in range(2))

    def finish(qi, state):
        ot = jnp.concatenate([acc[:HEAD_DIM] * (1.0 / acc[HEAD_DIM:HEAD_DIM + 1])
                              for (_, acc) in state], axis=0)
        o_ref[0, qi * CHUNK:(qi + 1) * CHUNK, :] = ot.T.astype(BF16)

    def dense_tile(qi):
        ws = weights(qi)
        maxima = produce(qi, ws, qi, bufs[0])
        state = fresh_state()
        for g, c in enumerate(range(qi, -1, -1)):
            nxt = produce(qi, ws, c - 1, bufs[(g + 1) % 2]) if c > 0 else None
            state = consume(c, bufs[g % 2], maxima, state, own=(c == qi))
            maxima = nxt
        finish(qi, state)

    def dense_program():
        g = 0
        ws = weights(0)
        maxima = produce(0, ws, 0, bufs[0])
        for qi in range(nchunk):
            state = fresh_state()
            ws_next = weights(qi + 1) if qi + 1 < nchunk else None
            for c in range(qi, -1, -1):
                nxt = None
                if c > 0:
                    nxt = produce(qi, ws, c - 1, bufs[(g + 1) % 2])
                elif ws_next is not None:
                    nxt = produce(qi + 1, ws_next, qi + 1, bufs[(g + 1) % 2])
                state = consume(c, bufs[g % 2], maxima, state, own=(c == qi))
                maxima = nxt
                g += 1
            ws = ws_next
            finish(qi, state)

    if moba:
        dense_program()
        return

    own_bufs, far_bufs = bufs[0:2], bufs[2:4]
    lane = lax.broadcasted_iota(jnp.int32, (1, PAIR_W), 1)
    key_norm, running = [], [None, None]
    for c in range(nchunk):
        ksq = k_ref[0, c * CHUNK:(c + 1) * CHUNK, 0:PAIR_W].astype(F32) ** 2
        for hh in range(2):
            mine = (lane >= hh * HEAD_DIM) & (lane < (hh + 1) * HEAD_DIM)
            norm = jnp.sqrt(jnp.max(jnp.sum(jnp.where(mine, ksq, 0.0), axis=1, keepdims=True)))
            running[hh] = norm if running[hh] is None else jnp.maximum(running[hh], norm)
        key_norm.append(tuple(running))

    def gap(qn, knorm, decay_rows, hh, m, newer_queries_only):
        mine = (lane >= N_SPLIT * hh) & (lane < N_SPLIT * (hh + 1))
        newest = jnp.max(jnp.sum(jnp.where(mine, decay_rows, 0.0), axis=1, keepdims=True))
        bound = qn[hh] * knorm[hh] + newest - m
        if newer_queries_only:
            qlane = lax.broadcasted_iota(jnp.int32, bound.shape, 1)
            bound = jnp.where(qlane >= BLK, bound, -jnp.inf)
        return jnp.max(bound)

    def skipped_gap(qi, state):
        base = (qi - 2) * CHUNK
        older_blk = k_ref[0, base + BLK - BF16_TILE_ROWS:base + BLK, PAIR_W:2 * PAIR_W].astype(F32)
        newer_blk = k_ref[0, base + CHUNK - BF16_TILE_ROWS:base + CHUNK, PAIR_W:2 * PAIR_W].astype(F32)
        qn = query_norms(qi)
        worst = None
        for hh in range(2):
            g = jnp.maximum(gap(qn, key_norm[qi - 2], older_blk, hh, state[hh][0], False),
                            gap(qn, key_norm[qi - 2], newer_blk, hh, state[hh][0], True))
            worst = g if worst is None else jnp.maximum(worst, g)
        return worst

    band_buf = far_bufs[1]

    def produce_band(qi, ws):
        c = qi - 2
        kq = k_ref[0, c * CHUNK + BLK:(c + 1) * CHUNK, :]
        maxima = []
        for hh in range(2):
            s = jnp.dot(kq, ws[hh][:, :BLK], preferred_element_type=F32)
            band_buf[hidx[hh], :BLK, :BLK] = s
            maxima.append(jnp.max(s, axis=0, keepdims=True))
        return tuple(maxima)

    def consume_band(qi, maxima, state):
        c = qi - 2
        new = []
        for hh in range(2):
            m_old, acc = state[hh]
            m_left = jnp.maximum(m_old[:, :BLK], maxima[hh])
            p = jnp.exp2(band_buf[hidx[hh], :BLK, :BLK] - m_left).astype(BF16)
            vth = vt_ref[0, c, hh * HEAD_ROWS:(hh + 1) * HEAD_ROWS, BLK:]
            left = jnp.exp2(m_old[:, :BLK] - m_left) * acc[:, :BLK] + jnp.dot(vth, p, preferred_element_type=F32)
            new.append((jnp.concatenate([m_left, m_old[:, BLK:]], axis=1),
                        jnp.concatenate([left, acc[:, BLK:]], axis=1)))
        return tuple(new)

    ws = weights(0)
    maxima = produce(0, ws, 0, own_bufs[0])
    gaps = {}
    for qi in range(nchunk):
        state = fresh_state()
        ws_next = weights(qi + 1) if qi + 1 < nchunk else None
        own_src, next_dst = own_bufs[qi % 2], own_bufs[(qi + 1) % 2]
        chunks = list(range(qi, max(qi - NEAR_CHUNKS, -1), -1))
        band = qi >= NEAR_CHUNKS
        src = own_src
        for j, c in enumerate(chunks):
            last = j == len(chunks) - 1
            if not last:
                dst = far_bufs[j % 2]
                nxt = produce(qi, ws, c - 1, dst)
            elif band:
                band_mx = produce_band(qi, ws)
            elif ws_next is not None:
                next_own = produce(qi + 1, ws_next, qi + 1, next_dst)
            state = consume(c, src, maxima, state, own=(j == 0))
            if not last:
                src, maxima = dst, nxt
        if band:
            gaps[qi] = skipped_gap(qi, state)
            if ws_next is not None:
                next_own = produce(qi + 1, ws_next, qi + 1, next_dst)
            state = consume_band(qi, band_mx, state)
        finish(qi, state)
        ws, maxima = ws_next, next_own

    for qi, worst in gaps.items():
        pl.when(worst >= -UNDERFLOW_BITS)(functools.partial(dense_tile, qi))


def _block_indicator():
    ind = np.zeros((CHUNK, PAIR_W), np.float32)
    for r in range(BLOCKS_PER_CHUNK):
        ind[r * BLK:(r + 1) * BLK, r] = 1.0
    ind[:, BLOCKS_PER_CHUNK:BLOCKS_PER_CHUNK + N_SPLIT] = 1.0
    return jnp.asarray(ind, BF16)


def _attention(qt, k, vt, sel=None, tables=None, rel_bias=None):
    moba = sel is not None
    B, S, _ = k.shape
    nchunk = S // CHUNK
    kw = k.shape[-1] // N_PAIRS
    vt_off = 0 if moba else N_PAIRS
    in_specs = [pl.BlockSpec((1, nchunk, PAIR_W, CHUNK), lambda b, p: (b, 0, p, 0)),
                pl.BlockSpec((1, S, kw), lambda b, p: (b, 0, p)),
                pl.BlockSpec((1, nchunk, PAIR_ROWS, CHUNK), lambda b, p: (b, 0, p + vt_off, 0))]
    args = [qt, k, vt]
    if moba:
        in_specs = [pl.BlockSpec(memory_space=pltpu.SMEM)] + in_specs
        in_specs += [pl.BlockSpec((1, 2 * MAX_BLOCKS, S), lambda b, p: (b, p, 0)),
                     pl.BlockSpec((2, CHUNK, CHUNK), lambda b, p: (p, 0, 0)),
                     pl.BlockSpec((2, CORNER, CORNER), lambda b, p: (p, 0, 0)),
                     pl.BlockSpec((CHUNK, PAIR_W), lambda b, p: (0, 0))]
        args = [rel_bias] + args + [sel, *tables, _block_indicator()]
    return pl.pallas_call(
        functools.partial(_attn_kernel, moba=moba, nchunk=nchunk),
        grid=(B, N_PAIRS),
        in_specs=in_specs,
        out_specs=pl.BlockSpec((1, S, PAIR_W), lambda b, p: (b, 0, p)),
        out_shape=jax.ShapeDtypeStruct((B, S, ATT_W), BF16),
        scratch_shapes=[pltpu.VMEM((2, CHUNK, CHUNK), F32)] * (2 if moba else 4),
        compiler_params=pltpu.CompilerParams(
            dimension_semantics=("arbitrary", "arbitrary"), vmem_limit_bytes=VMEM_LIMIT),
        name="moba_attn" if moba else "fox_attn",
    )(*args)


def _out_ffn_kernel(moba_ref, fox_ref, conv_ref, x_ref, wo_ref, wg_ref, wu_ref, wd_ref,
                    gpm_ref, gpf_ref, gqf_ref, o_ref):
    mix_in = jnp.concatenate([moba_ref[...], fox_ref[...], conv_ref[...]], axis=1)
    mixed = jnp.dot(mix_in, wo_ref[...], preferred_element_type=F32)
    x1 = x_ref[...] + _rms(mixed, gpm_ref[...])
    h2 = _rms(x1, gpf_ref[...]).astype(BF16)
    ff = jnp.zeros(x1.shape, F32)
    lo = 0
    for width in FFN_CHUNKS:
        gate = jnp.dot(h2, wg_ref[:, lo:lo + width], preferred_element_type=F32)
        up = jnp.dot(h2, wu_ref[:, lo:lo + width], preferred_element_type=F32)
        act = (gate * (1.0 / (1.0 + jnp.exp(-gate))) * up).astype(BF16)
        ff = ff + jnp.dot(act, wd_ref[lo:lo + width, :], preferred_element_type=F32)
        lo += width
    o_ref[...] = x1 + _rms(ff, gqf_ref[...])


def _out_ffn(moba, fox, conv, x, wo, wg, wu, wd, gpm, gpf, gqf):
    M, D = x.shape
    tm = FFN_TILE
    row = lambda w: pl.BlockSpec((tm, w), lambda m: (m, 0))
    const = lambda shape: pl.BlockSpec(shape, lambda m: (0, 0), pipeline_mode=pl.Buffered(1))
    return pl.pallas_call(
        _out_ffn_kernel,
        grid=(M // tm,),
        in_specs=[row(ATT_W), row(ATT_W), row(CONV_CH), row(D),
                  const(wo.shape), const(wg.shape), const(wu.shape), const(wd.shape),
                  const((1, D)), const((1, D)), const((1, D))],
        out_specs=row(D),
        out_shape=jax.ShapeDtypeStruct((M, D), F32),
        compiler_params=pltpu.CompilerParams(
            dimension_semantics=("arbitrary",), vmem_limit_bytes=VMEM_LIMIT),
        name="out_ffn",
    )(moba, fox, conv, x, wo, wg, wu, wd, gpm, gpf, gqf)


def _cumsum_constants():
    tri = np.tril(np.ones((BLK, BLK), np.float32))
    perm = np.zeros((N_SPLIT * LANES, ATT_W), np.float32)
    for s in range(N_SPLIT):
        for h in range(N_HEADS):
            perm[s * LANES + h, PAIR_W * (h // 2) + N_SPLIT * (h % 2) + s] = 1.0
    return jnp.asarray(tri, BF16), jnp.asarray(perm, BF16)


def kernel(x, w_in, b_f, conv_w, w_out, rel_bias, g_pre_mix, g_post_mix, g_pre_ffn, g_post_ffn,
           w_gate, w_up, w_down):
    B, S, D = x.shape
    depth = w_in.shape[0]
    assert D == D_MODEL and S % CHUNK == 0 and S // BLK <= MAX_BLOCKS
    assert (B * S) % FFN_TILE == 0

    rel_bias = rel_bias.astype(F32)
    tables = _bias_tiles(rel_bias)
    w_in_t = jnp.swapaxes(w_in, 1, 2)
    tri, perm = _cumsum_constants()
    a = ATT_W
    for l in range(depth):
        wl = w_in_t[l]
        wqkv = wl[0:6 * a].astype(BF16)
        wtail = jnp.concatenate([wl[6 * a:6 * a + N_HEADS], jnp.zeros((LANES - N_HEADS, D), wl.dtype),
                                 wl[6 * a + N_HEADS:]], axis=0).astype(BF16)
        bfp = jnp.pad(b_f[l].astype(F32), (0, LANES - N_HEADS)).reshape(1, LANES)
        row = lambda g: g[l].astype(F32).reshape(1, D)

        mq, mk, fq, fkp, vt, sel, conv = _inproj(
            x, row(g_pre_mix), wqkv, wtail, bfp, conv_w[l].astype(F32), tri, perm)
        moba = _attention(mq, mk, vt, sel, tables, rel_bias)
        fox = _attention(fq, fkp, vt)
        x = _out_ffn(moba.reshape(B * S, a), fox.reshape(B * S, a), conv.reshape(B * S, CONV_CH),
                     x.reshape(B * S, D), w_out[l].astype(BF16), w_gate[l].astype(BF16),
                     w_up[l].astype(BF16), w_down[l].astype(BF16),
                     row(g_post_mix), row(g_pre_ffn), row(g_post_ffn)).reshape(B, S, D)
    return x
```
